```python
import math
import jax
import jax.numpy as jnp
from jax import lax
import numpy as np

D_MODEL = 2048
BATCH = 4
SEQ = 2048
DEPTH = 1
DEC_BATCH = 128
DEC_SEQ = 1
PAST_LEN = 16384
PAGE_SIZE = 128

POOL_WIDTH = D_MODEL // 2
POOL_WINDOWS = (2, 4, 8, 16)
POOL_GROUPS = len(POOL_WINDOWS)
POOL_GROUP_DIM = POOL_WIDTH // POOL_GROUPS
POOL_BUF = max(POOL_WINDOWS) - 1

SSM_WIDTH = D_MODEL // 2
SSM_GROUP_CH = 16
SSM_GROUPS = SSM_WIDTH // SSM_GROUP_CH
SSM_STATE = 64
SSM_DT_MIN = 0.001
SSM_DT_MAX = 0.1

XA_HEADS = 4
XA_HEAD_DIM = D_MODEL // 8
XA_WIDTH = XA_HEADS * XA_HEAD_DIM
XA_SCALE = XA_HEAD_DIM ** -0.5
N_MEM = 256

N_BRANCH = 3
OFF_SSM = POOL_WIDTH
OFF_XA = POOL_WIDTH + SSM_WIDTH
OFF_GATE = POOL_WIDTH + SSM_WIDTH + XA_WIDTH
IN_WIDTH = OFF_GATE + N_BRANCH * D_MODEL

D_FF = 256 * ((8 * D_MODEL // 3 + 255) // 256)
RMS_EPS = 1e-6

kernel_name = 'hybrid_pool_s5_memxattn_step'


def rmsnorm(x, g):
    xf = x.astype(jnp.float32)
    y = xf * lax.rsqrt(jnp.mean(xf * xf, axis=-1, keepdims=True) + RMS_EPS)
    return (y * g.astype(jnp.float32)).astype(x.dtype)


def swiglu(x, w_gate, w_up, w_down):
    return (jax.nn.silu(x @ w_gate) * (x @ w_up)) @ w_down


def multiscale_pool(u, buf, pos, w_grp, scale, w_proj):
    b, l, _ = u.shape
    ext = jnp.concatenate([buf.astype(jnp.float32), u.astype(jnp.float32)], axis=1)
    cs = jnp.concatenate([jnp.zeros_like(ext[:, :1]), jnp.cumsum(ext, axis=1)], axis=1)
    end = cs[:, POOL_BUF + 1:]
    means = []
    for k, w in enumerate(POOL_WINDOWS):
        ch = slice(k * POOL_GROUP_DIM, (k + 1) * POOL_GROUP_DIM)
        start = POOL_BUF + 1 - w
        win_sum = end[..., ch] - cs[:, start:start + l, ch]
        count = jnp.minimum(pos + 1, w).astype(jnp.float32)[None, :, None]
        means.append(win_sum / count)
    diff = jnp.concatenate(means, axis=-1) - ext[:, POOL_BUF:]
    diff = diff.astype(u.dtype).reshape(b, l, POOL_GROUPS, POOL_GROUP_DIM)
    z = jnp.einsum('blgc,gcd->blgd', diff, w_grp).reshape(b, l, POOL_WIDTH) * scale
    return z @ w_proj, ext[:, -POOL_BUF:].astype(buf.dtype)


def _cplx_affine_combine(e1, e2):
    a1r, a1i, b1r, b1i = e1
    a2r, a2i, b2r, b2i = e2
    return (a1r * a2r - a1i * a2i,
            a1r * a2i + a1i * a2r,
            a2r * b1r - a2i * b1i + b2r,
            a2r * b1i + a2i * b1r + b2i)


def s5_ssm(u, h_re, h_im, a_re, a_im, log_step, b_re, b_im, c_re, c_im, d_skip):
    bt, l, _ = u.shape
    f32 = jnp.float32
    uf = u.astype(f32).reshape(bt, l, SSM_GROUPS, SSM_GROUP_CH)
    a_re = a_re.astype(f32)
    a_im = a_im.astype(f32)
    dt = jnp.exp(log_step.astype(f32))[:, None]
    mag = jnp.exp(a_re * dt)
    ang = a_im * dt
    lb_re = mag * jnp.cos(ang)
    lb_im = mag * jnp.sin(ang)
    den = a_re * a_re + a_im * a_im
    n_re = lb_re - 1.0
    f_re = (n_re * a_re + lb_im * a_im) / den
    f_im = (lb_im * a_re - n_re * a_im) / den
    b_re = b_re.astype(f32)
    b_im = b_im.astype(f32)
    bb_re = f_re[..., None] * b_re - f_im[..., None] * b_im
    bb_im = f_re[..., None] * b_im + f_im[..., None] * b_re
    bu_re = jnp.einsum('blgh,gnh->blgn', uf, bb_re)
    bu_im = jnp.einsum('blgh,gnh->blgn', uf, bb_im)
    h_re = h_re.astype(f32)
    h_im = h_im.astype(f32)
    bu_re = bu_re.at[:, 0].add(lb_re * h_re - lb_im * h_im)
    bu_im = bu_im.at[:, 0].add(lb_re * h_im + lb_im * h_re)
    a_seq_re = jnp.broadcast_to(lb_re, (1, l) + lb_re.shape)
    a_seq_im = jnp.broadcast_to(lb_im, (1, l) + lb_im.shape)
    _, _, x_re, x_im = lax.associative_scan(
        _cplx_affine_combine, (a_seq_re, a_seq_im, bu_re, bu_im), axis=1)
    y = (jnp.einsum('blgn,ghn->blgh', x_re, c_re.astype(f32))
         - jnp.einsum('blgn,ghn->blgh', x_im, c_im.astype(f32)))
    y = y.reshape(bt, l, SSM_WIDTH) + d_skip.astype(f32) * uf.reshape(bt, l, SSM_WIDTH)
    return y.astype(u.dtype), x_re[:, -1], x_im[:, -1]


def memory_kv(mem, g_mem, w_mem_k, w_mem_v):
    b, m, _ = mem.shape
    mn = rmsnorm(mem, g_mem)
    k = (mn @ w_mem_k).reshape(b, m, XA_HEADS, XA_HEAD_DIM)
    v = (mn @ w_mem_v).reshape(b, m, XA_HEADS, XA_HEAD_DIM)
    return k, v


def cross_attend(q, mem_k, mem_v, w_o):
    b, l, _ = q.shape
    qh = q.reshape(b, l, XA_HEADS, XA_HEAD_DIM).astype(jnp.float32)
    s = jnp.einsum('blhd,bmhd->bhlm', qh, mem_k.astype(jnp.float32)) * XA_SCALE
    p = jax.nn.softmax(s, axis=-1)
    o = jnp.einsum('bhlm,bmhd->blhd', p, mem_v.astype(jnp.float32)).astype(q.dtype)
    return o.reshape(b, l, XA_WIDTH) @ w_o


def decoder_layer(x, pos, mem_k, mem_v, pool_buf, h_re, h_im, p):
    b, l, _ = x.shape
    h = x + 0.5 * rmsnorm(swiglu(rmsnorm(x, p['g_ff1_pre']), p['w_ff1_gate'], p['w_ff1_up'],
                                  p['w_ff1_down']), p['g_ff1_post'])
    xn = rmsnorm(h, p['g_mix_pre'])
    proj = xn @ p['w_in']
    u_pool, u_ssm, q, gate_logits = jnp.split(proj, [OFF_SSM, OFF_XA, OFF_GATE], axis=-1)
    o_pool, new_buf = multiscale_pool(u_pool, pool_buf, pos, p['w_pool_grp'], p['pool_scale'],
                                      p['w_pool_out'])
    y_ssm, new_re, new_im = s5_ssm(u_ssm, h_re, h_im, p['ssm_a_re'], p['ssm_a_im'], p['ssm_log_step'],
                                   p['ssm_b_re'], p['ssm_b_im'], p['ssm_c_re'], p['ssm_c_im'],
                                   p['ssm_d'])
    g = jax.nn.gelu(y_ssm)
    o_ssm = (g @ p['w_glu_val']) * jax.nn.sigmoid(g @ p['w_glu_gate'])
    o_xa = cross_attend(q, mem_k, mem_v, p['w_xa_out'])
    gates = jax.nn.sigmoid(gate_logits.reshape(b, l, N_BRANCH, D_MODEL))
    merged = gates[:, :, 0] * o_pool + gates[:, :, 1] * o_ssm + gates[:, :, 2] * o_xa
    h = h + rmsnorm(merged @ p['w_out'], p['g_mix_post'])
    h = h + 0.5 * rmsnorm(swiglu(rmsnorm(h, p['g_ff2_pre']), p['w_ff2_gate'], p['w_ff2_up'],
                                  p['w_ff2_down']), p['g_ff2_post'])
    return h, new_buf, new_re, new_im


def setup_inputs(seed: int = 0) -> dict:
    key = jax.random.key(seed)
    keys = jax.random.split(key, 48)
    counter = [0]
    f32 = jnp.float32

    def nk():
        k = keys[counter[0]]
        counter[0] += 1
        return k

    def dense(shape, fan_in):
        return jax.random.normal(nk(), shape, f32) * fan_in ** -0.5

    def gain():
        return 1.0 + 0.02 * jax.random.normal(nk(), (DEPTH, D_MODEL), f32)

    L = DEPTH
    inp = {}
    inp['x_prompt'] = jax.random.normal(nk(), (BATCH, SEQ, D_MODEL), f32)
    inp['x_sample'] = jax.random.normal(nk(), (DEC_BATCH, DEC_SEQ, D_MODEL), f32)
    inp['mem_prompt'] = jax.random.normal(nk(), (BATCH, N_MEM, D_MODEL), f32)
    inp['cache_mem_k'] = jax.random.normal(nk(), (L, DEC_BATCH, N_MEM, XA_HEADS, XA_HEAD_DIM), f32)
    inp['cache_mem_v'] = jax.random.normal(nk(), (L, DEC_BATCH, N_MEM, XA_HEADS, XA_HEAD_DIM), f32)
    inp['state_pool'] = jax.random.normal(nk(), (L, DEC_BATCH, POOL_BUF, POOL_WIDTH), f32)
    inp['state_ssm_re'] = 0.1 * jax.random.normal(nk(), (L, DEC_BATCH, SSM_GROUPS, SSM_STATE), f32)
    inp['state_ssm_im'] = 0.1 * jax.random.normal(nk(), (L, DEC_BATCH, SSM_GROUPS, SSM_STATE), f32)
    inp['g_ff1_pre'] = gain()
    inp['w_ff1_gate'] = dense((L, D_MODEL, D_FF), D_MODEL)
    inp['w_ff1_up'] = dense((L, D_MODEL, D_FF), D_MODEL)
    inp['w_ff1_down'] = dense((L, D_FF, D_MODEL), D_FF)
    inp['g_ff1_post'] = gain()
    inp['g_mix_pre'] = gain()
    inp['w_in'] = dense((L, D_MODEL, IN_WIDTH), D_MODEL)
    inp['w_pool_grp'] = dense((L, POOL_GROUPS, POOL_GROUP_DIM, POOL_GROUP_DIM), POOL_GROUP_DIM)
    inp['pool_scale'] = 1.0 + 0.1 * jax.random.normal(nk(), (L, POOL_WIDTH), f32)
    inp['w_pool_out'] = dense((L, POOL_WIDTH, D_MODEL), POOL_WIDTH)
    inp['ssm_a_re'] = -0.5 * jnp.exp(0.05 * jax.random.normal(nk(), (L, SSM_GROUPS, SSM_STATE), f32))
    n_idx = jnp.arange(SSM_STATE, dtype=f32)[None, None, :]
    inp['ssm_a_im'] = math.pi * n_idx + 0.01 * jax.random.normal(nk(), (L, SSM_GROUPS, SSM_STATE), f32)
    inp['ssm_log_step'] = jax.random.uniform(nk(), (L, SSM_GROUPS), f32,
                                             minval=math.log(SSM_DT_MIN), maxval=math.log(SSM_DT_MAX))
    inp['ssm_b_re'] = dense((L, SSM_GROUPS, SSM_STATE, SSM_GROUP_CH), 2 * SSM_GROUP_CH)
    inp['ssm_b_im'] = dense((L, SSM_GROUPS, SSM_STATE, SSM_GROUP_CH), 2 * SSM_GROUP_CH)
    inp['ssm_c_re'] = dense((L, SSM_GROUPS, SSM_GROUP_CH, SSM_STATE), 2 * SSM_STATE)
    inp['ssm_c_im'] = dense((L, SSM_GROUPS, SSM_GROUP_CH, SSM_STATE), 2 * SSM_STATE)
    inp['ssm_d'] = jax.random.normal(nk(), (L, SSM_WIDTH), f32)
    inp['w_glu_val'] = dense((L, SSM_WIDTH, D_MODEL), SSM_WIDTH)
    inp['w_glu_gate'] = dense((L, SSM_WIDTH, D_MODEL), SSM_WIDTH)
    inp['g_mem'] = gain()
    inp['w_mem_k'] = dense((L, D_MODEL, XA_WIDTH), D_MODEL)
    inp['w_mem_v'] = dense((L, D_MODEL, XA_WIDTH), D_MODEL)
    inp['w_xa_out'] = dense((L, XA_WIDTH, D_MODEL), XA_WIDTH)
    inp['w_out'] = dense((L, D_MODEL, D_MODEL), D_MODEL)
    inp['g_mix_post'] = gain()
    inp['g_ff2_pre'] = gain()
    inp['w_ff2_gate'] = dense((L, D_MODEL, D_FF), D_MODEL)
    inp['w_ff2_up'] = dense((L, D_MODEL, D_FF), D_MODEL)
    inp['w_ff2_down'] = dense((L, D_FF, D_MODEL), D_FF)
    inp['g_ff2_post'] = gain()
    return inp


def reference(x_prompt, x_sample, mem_prompt, cache_mem_k, cache_mem_v, state_pool, state_ssm_re,
              state_ssm_im, g_ff1_pre, w_ff1_gate, w_ff1_up, w_ff1_down, g_ff1_post, g_mix_pre, w_in,
              w_pool_grp, pool_scale, w_pool_out, ssm_a_re, ssm_a_im, ssm_log_step, ssm_b_re, ssm_b_im,
              ssm_c_re, ssm_c_im, ssm_d, w_glu_val, w_glu_gate, g_mem, w_mem_k, w_mem_v, w_xa_out,
              w_out, g_mix_post, g_ff2_pre, w_ff2_gate, w_ff2_up, w_ff2_down, g_ff2_post):
    weights = dict(g_ff1_pre=g_ff1_pre, w_ff1_gate=w_ff1_gate, w_ff1_up=w_ff1_up, w_ff1_down=w_ff1_down,
                   g_ff1_post=g_ff1_post, g_mix_pre=g_mix_pre, w_in=w_in, w_pool_grp=w_pool_grp,
                   pool_scale=pool_scale, w_pool_out=w_pool_out, ssm_a_re=ssm_a_re, ssm_a_im=ssm_a_im,
                   ssm_log_step=ssm_log_step, ssm_b_re=ssm_b_re, ssm_b_im=ssm_b_im, ssm_c_re=ssm_c_re,
                   ssm_c_im=ssm_c_im, ssm_d=ssm_d, w_glu_val=w_glu_val, w_glu_gate=w_glu_gate,
                   w_xa_out=w_xa_out, w_out=w_out, g_mix_post=g_mix_post, g_ff2_pre=g_ff2_pre,
                   w_ff2_gate=w_ff2_gate, w_ff2_up=w_ff2_up, w_ff2_down=w_ff2_down, g_ff2_post=g_ff2_post)
    bp, lp, _ = x_prompt.shape
    ls = x_sample.shape[1]
    pos_p = jnp.arange(lp, dtype=jnp.int32)
    pos_s = PAST_LEN + jnp.arange(ls, dtype=jnp.int32)
    hp, hs = x_prompt, x_sample
    mk_l, mv_l, pp_l, rp_l, ip_l, ps_l, rs_l, is_l = [], [], [], [], [], [], [], []
    for layer in range(DEPTH):
        p = {name: w[layer] for name, w in weights.items()}
        mk, mv = memory_kv(mem_prompt, g_mem[layer], w_mem_k[layer], w_mem_v[layer])
        pool0 = jnp.zeros((bp, POOL_BUF, POOL_WIDTH), x_prompt.dtype)
        ssm0 = jnp.zeros((bp, SSM_GROUPS, SSM_STATE), jnp.float32)
        hp, pb, sr, si = decoder_layer(hp, pos_p, mk, mv, pool0, ssm0, ssm0, p)
        hs, pbs, srs, sis = decoder_layer(hs, pos_s, cache_mem_k[layer], cache_mem_v[layer],
                                          state_pool[layer], state_ssm_re[layer], state_ssm_im[layer], p)
        mk_l.append(mk.astype(cache_mem_k.dtype))
        mv_l.append(mv.astype(cache_mem_v.dtype))
        pp_l.append(pb.astype(state_pool.dtype))
        rp_l.append(sr.astype(state_ssm_re.dtype))
        ip_l.append(si.astype(state_ssm_im.dtype))
        ps_l.append(pbs.astype(state_pool.dtype))
        rs_l.append(srs.astype(state_ssm_re.dtype))
        is_l.append(sis.astype(state_ssm_im.dtype))
    return (hp, hs, jnp.stack(mk_l), jnp.stack(mv_l), jnp.stack(pp_l), jnp.stack(rp_l), jnp.stack(ip_l),
            jnp.stack(ps_l), jnp.stack(rs_l), jnp.stack(is_l))
```

```python
import functools
import math

import jax
import jax.numpy as jnp
from jax import lax
from jax.experimental import pallas as pl
from jax.experimental.pallas import tpu as pltpu

F32 = jnp.float32
BF16 = jnp.bfloat16

D_MODEL = 2048
PAST_LEN = 16384
POOL_WIDTH = D_MODEL // 2
POOL_WINDOWS = (2, 4, 8, 16)
POOL_GROUP_DIM = POOL_WIDTH // len(POOL_WINDOWS)
POOL_BUF = max(POOL_WINDOWS) - 1
SSM_WIDTH = D_MODEL // 2
SSM_GROUP_CH = 16
SSM_GROUPS = SSM_WIDTH // SSM_GROUP_CH
SSM_STATE = 64
SSM_LANES = SSM_GROUPS * SSM_STATE
XA_HEADS = 4
XA_HEAD_DIM = D_MODEL // 8
XA_WIDTH = XA_HEADS * XA_HEAD_DIM
XA_SCALE = XA_HEAD_DIM ** -0.5
N_MEM = 256
OFF_GATE = POOL_WIDTH + SSM_WIDTH + XA_WIDTH
IN_WIDTH = OFF_GATE + 3 * D_MODEL
RMS_EPS = 1e-6

SUBLANES = 8
LANES = 128
MXU_DIM = 256
VMEM_LIMIT_BYTES = 56 * 1024 * 1024

SSM_CHUNKS = SSM_WIDTH // MXU_DIM
SSM_CHUNK_LANES = SSM_LANES // SSM_CHUNKS
SCAN_LANES = 2 * LANES


def _params(*sem):
    return pltpu.CompilerParams(dimension_semantics=sem, vmem_limit_bytes=VMEM_LIMIT_BYTES)


def _rms(x, g):
    return (x * lax.rsqrt(jnp.mean(x * x, axis=-1, keepdims=True) + RMS_EPS)) * g


def _dot(a, b):
    return jnp.dot(a, b, preferred_element_type=F32)


def _ffn_kernel(x_ref, gpre_ref, wg_ref, wu_ref, wd_ref, gpost_ref, o_ref, xn_ref, acc_ref):
    j = pl.program_id(1)

    @pl.when(j == 0)
    def _():
        xn_ref[...] = _rms(x_ref[...], gpre_ref[...]).astype(BF16)
        acc_ref[...] = jnp.zeros_like(acc_ref)

    xn = xn_ref[...]
    g = _dot(xn, wg_ref[...])
    u = _dot(xn, wu_ref[...])
    hid = (g * jax.nn.sigmoid(g)) * u
    acc_ref[...] += _dot(hid.astype(BF16), wd_ref[...])

    @pl.when(j == pl.num_programs(1) - 1)
    def _():
        o_ref[...] = x_ref[...] + 0.5 * _rms(acc_ref[...], gpost_ref[...])


def _ffn(x, g_pre, w_gate, w_up, w_down, g_post, tm, tf):
    m, d = x.shape
    f = w_gate.shape[1]
    return pl.pallas_call(
        _ffn_kernel,
        grid=(m // tm, f // tf),
        in_specs=[
            pl.BlockSpec((tm, d), lambda i, j: (i, 0)),
            pl.BlockSpec((1, d), lambda i, j: (0, 0)),
            pl.BlockSpec((d, tf), lambda i, j: (0, j)),
            pl.BlockSpec((d, tf), lambda i, j: (0, j)),
            pl.BlockSpec((tf, d), lambda i, j: (j, 0)),
            pl.BlockSpec((1, d), lambda i, j: (0, 0)),
        ],
        out_specs=pl.BlockSpec((tm, d), lambda i, j: (i, 0)),
        out_shape=jax.ShapeDtypeStruct((m, d), F32),
        scratch_shapes=[pltpu.VMEM((tm, d), BF16), pltpu.VMEM((tm, d), F32)],
        compiler_params=_params("parallel", "arbitrary"),
        name="ffn",
    )(x, g_pre, w_gate, w_up, w_down, g_post)


def _normproj_kernel(x_ref, g_ref, w_ref, o_ref, xn_ref):
    @pl.when(pl.program_id(1) == 0)
    def _():
        xn_ref[...] = _rms(x_ref[...], g_ref[...]).astype(BF16)

    o_ref[...] = _dot(xn_ref[...], w_ref[...])


def _normproj(x, g, w, tm, tn):
    m, d = x.shape
    n = w.shape[1]
    return pl.pallas_call(
        _normproj_kernel,
        grid=(m // tm, n // tn),
        in_specs=[
            pl.BlockSpec((tm, d), lambda i, j: (i, 0)),
            pl.BlockSpec((1, d), lambda i, j: (0, 0)),
            pl.BlockSpec((d, tn), lambda i, j: (0, j)),
        ],
        out_specs=pl.BlockSpec((tm, tn), lambda i, j: (i, j)),
        out_shape=jax.ShapeDtypeStruct((m, n), F32),
        scratch_shapes=[pltpu.VMEM((tm, d), BF16)],
        compiler_params=_params("parallel", "arbitrary"),
        name="normproj",
    )(x, g, w)


def _pool_group_out(k, diff, wgrp_ref, scale_ref, z_ref):
    ch = slice(k * POOL_GROUP_DIM, (k + 1) * POOL_GROUP_DIM)
    zk = _dot(diff.astype(BF16), wgrp_ref[k]) * scale_ref[:, ch]
    z_ref[:, ch] = zk.astype(BF16)


def _pool_seq_kernel(u_ref, wgrp_ref, scale_ref, z_ref, ext_ref, *, halo):
    t = pl.program_id(1)
    tt = u_ref.shape[0]

    @pl.when(t == 0)
    def _():
        ext_ref[0:halo, :] = jnp.zeros((halo, POOL_WIDTH), F32)

    u = u_ref[...]
    ext_ref[halo:halo + tt, :] = u
    pos = t * tt + lax.broadcasted_iota(jnp.int32, (tt, 1), 0)
    for k, w in enumerate(POOL_WINDOWS):
        ch = slice(k * POOL_GROUP_DIM, (k + 1) * POOL_GROUP_DIM)
        a = ext_ref[:, ch]
        d = 1
        while d < w:
            a = a + pltpu.roll(a, d, axis=0)
            d *= 2
        count = jnp.minimum(pos + 1, w).astype(F32)
        diff = a[halo:, :] / count - u[:, ch]
        _pool_group_out(k, diff, wgrp_ref, scale_ref, z_ref)
    ext_ref[0:halo, :] = ext_ref[tt:tt + halo, :]


def _pool_seq(proj, w_grp, scale, batch, seq, tt):
    halo = 2 * SUBLANES
    nt = seq // tt
    return pl.pallas_call(
        functools.partial(_pool_seq_kernel, halo=halo),
        grid=(batch, nt),
        in_specs=[
            pl.BlockSpec((tt, POOL_WIDTH), lambda b, t: (b * nt + t, 0)),
            pl.BlockSpec(w_grp.shape, lambda b, t: (0, 0, 0)),
            pl.BlockSpec((1, POOL_WIDTH), lambda b, t: (0, 0)),
        ],
        out_specs=pl.BlockSpec((tt, POOL_WIDTH), lambda b, t: (b * nt + t, 0)),
        out_shape=jax.ShapeDtypeStruct((batch * seq, POOL_WIDTH), BF16),
        scratch_shapes=[pltpu.VMEM((tt + halo, POOL_WIDTH), F32)],
        compiler_params=_params("parallel", "arbitrary"),
        name="pool_seq",
    )(proj, w_grp, scale)


def _pool_step_kernel(u_ref, buf_ref, wgrp_ref, scale_ref, z_ref, nbuf_ref):
    u = u_ref[...]
    for k, w in enumerate(POOL_WINDOWS):
        ch = slice(k * POOL_GROUP_DIM, (k + 1) * POOL_GROUP_DIM)
        s = u[:, ch]
        for j in range(1, w):
            row = POOL_BUF - j
            s = s + buf_ref[:, row * POOL_WIDTH + k * POOL_GROUP_DIM:
                            row * POOL_WIDTH + (k + 1) * POOL_GROUP_DIM]
        diff = s / float(min(PAST_LEN + 1, w)) - u[:, ch]
        _pool_group_out(k, diff, wgrp_ref, scale_ref, z_ref)
    keep = (POOL_BUF - 1) * POOL_WIDTH
    nbuf_ref[:, 0:keep] = buf_ref[:, POOL_WIDTH:]
    nbuf_ref[:, keep:] = u


def _pool_step(proj, buf, w_grp, scale, tb):
    m = proj.shape[0]
    return pl.pallas_call(
        _pool_step_kernel,
        grid=(m // tb,),
        in_specs=[
            pl.BlockSpec((tb, POOL_WIDTH), lambda i: (i, 0)),
            pl.BlockSpec((tb, POOL_BUF * POOL_WIDTH), lambda i: (i, 0)),
            pl.BlockSpec(w_grp.shape, lambda i: (0, 0, 0)),
            pl.BlockSpec((1, POOL_WIDTH), lambda i: (0, 0)),
        ],
        out_specs=[
            pl.BlockSpec((tb, POOL_WIDTH), lambda i: (i, 0)),
            pl.BlockSpec((tb, POOL_BUF * POOL_WIDTH), lambda i: (i, 0)),
        ],
        out_shape=[
            jax.ShapeDtypeStruct((m, POOL_WIDTH), BF16),
            jax.ShapeDtypeStruct((m, POOL_BUF * POOL_WIDTH), F32),
        ],
        compiler_params=_params("parallel"),
        name="pool_step",
    )(proj, buf, w_grp, scale)


def _ssm_disc_kernel(a_re_ref, a_im_ref, ls_ref, b_re_ref, b_im_ref, bb_re_ref, bb_im_ref, pw_re_ref, pw_im_ref):
    a_re = a_re_ref[...]
    a_im = a_im_ref[...]
    dt = jnp.exp(ls_ref[...])
    mag = jnp.exp(a_re * dt)
    ang = a_im * dt
    lb_re = mag * jnp.cos(ang)
    lb_im = mag * jnp.sin(ang)
    den = a_re * a_re + a_im * a_im
    n_re = lb_re - 1.0
    f_re = (n_re * a_re + lb_im * a_im) / den
    f_im = (lb_im * a_re - n_re * a_im) / den
    b_re = b_re_ref[...]
    b_im = b_im_ref[...]
    bb_re_ref[...] = f_re * b_re - f_im * b_im
    bb_im_ref[...] = f_re * b_im + f_im * b_re
    p_re, p_im = lb_re, lb_im
    pw_re_ref[0] = p_re
    pw_im_ref[0] = p_im
    for r in range(1, SUBLANES):
        p_re, p_im = p_re * lb_re - p_im * lb_im, p_re * lb_im + p_im * lb_re
        pw_re_ref[r] = p_re
        pw_im_ref[r] = p_im


def _ssm_disc(a_re, a_im, log_step, b_re, b_im):
    g, n, h = b_re.shape
    col = (g, n, 1)
    return pl.pallas_call(
        _ssm_disc_kernel,
        out_shape=[
            jax.ShapeDtypeStruct((g, n, h), F32),
            jax.ShapeDtypeStruct((g, n, h), F32),
            jax.ShapeDtypeStruct((SUBLANES,) + col, F32),
            jax.ShapeDtypeStruct((SUBLANES,) + col, F32),
        ],
        name="ssm_disc",
    )(a_re.reshape(col), a_im.reshape(col), jnp.broadcast_to(log_step[:, None, None], col), b_re, b_im)


def _ssm_in(ub, bd_ref, x_ref):
    for c in range(SSM_CHUNKS):
        x_ref[:, c * SSM_CHUNK_LANES:(c + 1) * SSM_CHUNK_LANES] = _dot(
            ub[:, c * MXU_DIM:(c + 1) * MXU_DIM], bd_ref[c])


def _ssm_out(u, xr, xi, cdr_ref, cdi_ref, d_ref, g_ref):
    for c in range(SSM_CHUNKS):
        ch = slice(c * MXU_DIM, (c + 1) * MXU_DIM)
        y = _dot(xr(c), cdr_ref[c]) - _dot(xi(c), cdi_ref[c])
        y = y + d_ref[:, ch] * u[:, ch]
        g_ref[:, ch] = jax.nn.gelu(y).astype(BF16)


def _ssm_seq_kernel(u_ref, h_re_ref, h_im_ref, tab_ref, bdr_ref, bdi_ref, cdr_ref, cdi_ref, d_ref,
                    g_ref, last_re_ref, last_im_ref, xr_ref, xi_ref, cr_ref, ci_ref):
    t = pl.program_id(1)
    tt = u_ref.shape[0]

    @pl.when(t == 0)
    def _():
        cr_ref[...] = jnp.broadcast_to(h_re_ref[...], cr_ref.shape)
        ci_ref[...] = jnp.broadcast_to(h_im_ref[...], ci_ref.shape)

    u = u_ref[...]
    ub = u.astype(BF16)
    _ssm_in(ub, bdr_ref, xr_ref)
    _ssm_in(ub, bdi_ref, xi_ref)

    for lc in range(SSM_LANES // SCAN_LANES):
        sl = slice(lc * SCAN_LANES, (lc + 1) * SCAN_LANES)
        steps = [(tab_ref[2 * i, :, sl], tab_ref[2 * i + 1, :, sl], 1 << i) for i in range(3)]
        p_re = tab_ref[6, :, sl]
        p_im = tab_ref[7, :, sl]

        def block(i, carry, sl=sl, steps=steps, p_re=p_re, p_im=p_im):
            c_re, c_im = carry
            rows = pl.ds(pl.multiple_of(i * SUBLANES, SUBLANES), SUBLANES)
            x_re = xr_ref[rows, sl]
            x_im = xi_ref[rows, sl]
            for a_re, a_im, d in steps:
                s_re = pltpu.roll(x_re, d, axis=0)
                s_im = pltpu.roll(x_im, d, axis=0)
                x_re, x_im = (x_re + (a_re * s_re - a_im * s_im),
                              x_im + (a_re * s_im + a_im * s_re))
            x_re, x_im = (x_re + (p_re * c_re - p_im * c_im),
                          x_im + (p_re * c_im + p_im * c_re))
            xr_ref[rows, sl] = x_re
            xi_ref[rows, sl] = x_im
            shape = (SUBLANES, SCAN_LANES)
            return (jnp.broadcast_to(x_re[SUBLANES - 1:SUBLANES, :], shape),
                    jnp.broadcast_to(x_im[SUBLANES - 1:SUBLANES, :], shape))

        c_re, c_im = lax.fori_loop(0, tt // SUBLANES, block, (cr_ref[:, sl], ci_ref[:, sl]))
        cr_ref[:, sl] = c_re
        ci_ref[:, sl] = c_im

    def chunk(ref):
        return lambda c: ref[:, c * SSM_CHUNK_LANES:(c + 1) * SSM_CHUNK_LANES].astype(BF16)

    _ssm_out(u, chunk(xr_ref), chunk(xi_ref), cdr_ref, cdi_ref, d_ref, g_ref)

    @pl.when(t == pl.num_programs(1) - 1)
    def _():
        last_re_ref[...] = cr_ref[0:1, :]
        last_im_ref[...] = ci_ref[0:1, :]


def _const_spec(shape):
    return pl.BlockSpec(shape, lambda *_: (0,) * len(shape))


def _ssm_seq(proj, h_re, h_im, tab, bdr, bdi, cdr, cdi, d_skip, batch, seq, tt):
    nt = seq // tt
    col_block = POOL_WIDTH // SSM_WIDTH
    state_spec = pl.BlockSpec((None, 1, SSM_LANES), lambda b, t: (b, 0, 0))
    state_shape = jax.ShapeDtypeStruct((batch, 1, SSM_LANES), F32)
    return pl.pallas_call(
        _ssm_seq_kernel,
        grid=(batch, nt),
        in_specs=[
            pl.BlockSpec((tt, SSM_WIDTH), lambda b, t: (b * nt + t, col_block)),
            state_spec, state_spec,
            _const_spec(tab.shape),
            _const_spec(bdr.shape), _const_spec(bdi.shape),
            _const_spec(cdr.shape), _const_spec(cdi.shape),
            _const_spec(d_skip.shape),
        ],
        out_specs=[pl.BlockSpec((tt, SSM_WIDTH), lambda b, t: (b * nt + t, 0)), state_spec, state_spec],
        out_shape=[jax.ShapeDtypeStruct((batch * seq, SSM_WIDTH), BF16), state_shape, state_shape],
        scratch_shapes=[
            pltpu.VMEM((tt, SSM_LANES), F32), pltpu.VMEM((tt, SSM_LANES), F32),
            pltpu.VMEM((SUBLANES, SSM_LANES), F32), pltpu.VMEM((SUBLANES, SSM_LANES), F32),
        ],
        compiler_params=_params("parallel", "arbitrary"),
        name="ssm_seq",
    )(proj, h_re, h_im, tab, bdr, bdi, cdr, cdi, d_skip)


def _ssm_step_kernel(u_ref, h_re_ref, h_im_ref, lb_ref, bdr_ref, bdi_ref, cdr_ref, cdi_ref, d_ref,
                     g_ref, x_re_ref, x_im_ref):
    u = u_ref[...]
    ub = u.astype(BF16)
    _ssm_in(ub, bdr_ref, x_re_ref)
    _ssm_in(ub, bdi_ref, x_im_ref)
    lb_re = lb_ref[0:1, :]
    lb_im = lb_ref[1:2, :]
    h_re = h_re_ref[...]
    h_im = h_im_ref[...]
    x_re_ref[...] += lb_re * h_re - lb_im * h_im
    x_im_ref[...] += lb_re * h_im + lb_im * h_re

    def chunk(ref):
        return lambda c: ref[:, c * SSM_CHUNK_LANES:(c + 1) * SSM_CHUNK_LANES].astype(BF16)

    _ssm_out(u, chunk(x_re_ref), chunk(x_im_ref), cdr_ref, cdi_ref, d_ref, g_ref)


def _ssm_step(proj, h_re, h_im, lb, bdr, bdi, cdr, cdi, d_skip):
    m = proj.shape[0]
    state_spec = pl.BlockSpec((m, SSM_LANES), lambda i: (0, 0))
    state_shape = jax.ShapeDtypeStruct((m, SSM_LANES), F32)
    return pl.pallas_call(
        _ssm_step_kernel,
        grid=(1,),
        in_specs=[
            pl.BlockSpec((m, SSM_WIDTH), lambda i: (0, 1)),
            state_spec, state_spec,
            _const_spec(lb.shape),
            _const_spec(bdr.shape), _const_spec(bdi.shape),
            _const_spec(cdr.shape), _const_spec(cdi.shape),
            _const_spec(d_skip.shape),
        ],
        out_specs=[pl.BlockSpec((m, SSM_WIDTH), lambda i: (0, 0)), state_spec, state_spec],
        out_shape=[jax.ShapeDtypeStruct((m, SSM_WIDTH), BF16), state_shape, state_shape],
        compiler_params=_params("arbitrary"),
        name="ssm_step",
    )(proj, h_re, h_im, lb, bdr, bdi, cdr, cdi, d_skip)


def _xa_seq_kernel(q_ref, k_ref, v_ref, o_ref):
    for h in range(XA_HEADS):
        ch = slice(h * XA_HEAD_DIM, (h + 1) * XA_HEAD_DIM)
        q = q_ref[:, ch].astype(BF16)
        k = k_ref[:, ch].astype(BF16)
        v = v_ref[:, ch].astype(BF16)
        s = lax.dot_general(q, k, (((1,), (1,)), ((), ())), preferred_element_type=F32) * XA_SCALE
        e = jnp.exp(s - jnp.max(s, axis=-1, keepdims=True))
        p = e / jnp.sum(e, axis=-1, keepdims=True)
        o_ref[:, ch] = _dot(p.astype(BF16), v).astype(BF16)


def _xa_seq(proj, mem_k, mem_v, batch, seq, tq):
    nt = seq // tq
    q_block = (POOL_WIDTH + SSM_WIDTH) // XA_WIDTH
    kv_spec = pl.BlockSpec((N_MEM, XA_WIDTH), lambda b, t: (b, 0))
    return pl.pallas_call(
        _xa_seq_kernel,
        grid=(batch, nt),
        in_specs=[pl.BlockSpec((tq, XA_WIDTH), lambda b, t: (b * nt + t, q_block)), kv_spec, kv_spec],
        out_specs=pl.BlockSpec((tq, XA_WIDTH), lambda b, t: (b * nt + t, 0)),
        out_shape=jax.ShapeDtypeStruct((batch * seq, XA_WIDTH), BF16),
        compiler_params=_params("parallel", "parallel"),
        name="xa_seq",
    )(proj, mem_k, mem_v)


def _xa_step_kernel(q_ref, k_ref, v_ref, o_ref):
    for h in range(XA_HEADS):
        ch = slice(h * XA_HEAD_DIM, (h + 1) * XA_HEAD_DIM)
        q = q_ref[:, :, ch]
        s = jnp.sum(k_ref[:, :, ch] * q, axis=-1, keepdims=True) * XA_SCALE
        e = jnp.exp(s - jnp.max(s, axis=1, keepdims=True))
        p = e / jnp.sum(e, axis=1, keepdims=True)
        o_ref[:, :, ch] = jnp.sum(p * v_ref[:, :, ch], axis=1, keepdims=True).astype(BF16)


def _xa_step(q3, cache_k, cache_v, tb):
    m = q3.shape[0]
    kv_spec = pl.BlockSpec((tb, N_MEM, XA_WIDTH), lambda i: (i, 0, 0))
    q_spec = pl.BlockSpec((tb, 1, XA_WIDTH), lambda i: (i, 0, 0))
    return pl.pallas_call(
        _xa_step_kernel,
        grid=(m // tb,),
        in_specs=[q_spec, kv_spec, kv_spec],
        out_specs=q_spec,
        out_shape=jax.ShapeDtypeStruct((m, 1, XA_WIDTH), BF16),
        compiler_params=_params("parallel"),
        name="xa_step",
    )(q3, cache_k, cache_v)


def _merge_kernel(z_ref, g_ref, o_ref, ga_ref, gb_ref, gc_ref, wpo_ref, wv_ref, wg_ref, wxo_ref, wout_ref,
                  h_ref, gpost_ref, out_ref, acc_ref):
    n = pl.program_id(1)

    @pl.when(n == 0)
    def _():
        acc_ref[...] = jnp.zeros_like(acc_ref)

    g = g_ref[...]
    o_pool = _dot(z_ref[...], wpo_ref[...])
    o_ssm = _dot(g, wv_ref[...]) * jax.nn.sigmoid(_dot(g, wg_ref[...]))
    o_xa = _dot(o_ref[...], wxo_ref[...])
    merged = (jax.nn.sigmoid(ga_ref[...]) * o_pool + jax.nn.sigmoid(gb_ref[...]) * o_ssm
              + jax.nn.sigmoid(gc_ref[...]) * o_xa)
    acc_ref[...] += _dot(merged.astype(BF16), wout_ref[...])

    @pl.when(n == pl.num_programs(1) - 1)
    def _():
        out_ref[...] = h_ref[...] + _rms(acc_ref[...], gpost_ref[...])


def _merge(z, g, o, proj, w_pool_out, w_glu_val, w_glu_gate, w_xa_out, w_out, h, g_post, tm, tn):
    m, d = h.shape
    half = z.shape[1]
    gate0 = OFF_GATE // tn
    per_gate = d // tn
    branch_spec = pl.BlockSpec((tm, half), lambda i, n: (i, 0))
    w_spec = pl.BlockSpec((half, tn), lambda i, n: (0, n))

    def gate_spec(k):
        return pl.BlockSpec((tm, tn), lambda i, n: (i, gate0 + k * per_gate + n))

    return pl.pallas_call(
        _merge_kernel,
        grid=(m // tm, d // tn),
        in_specs=[
            branch_spec, branch_spec, branch_spec,
            gate_spec(0), gate_spec(1), gate_spec(2),
            w_spec, w_spec, w_spec, w_spec,
            pl.BlockSpec((tn, d), lambda i, n: (n, 0)),
            pl.BlockSpec((tm, d), lambda i, n: (i, 0)),
            pl.BlockSpec((1, d), lambda i, n: (0, 0)),
        ],
        out_specs=pl.BlockSpec((tm, d), lambda i, n: (i, 0)),
        out_shape=jax.ShapeDtypeStruct((m, d), F32),
        scratch_shapes=[pltpu.VMEM((tm, d), F32)],
        compiler_params=_params("parallel", "arbitrary"),
        name="merge",
    )(z, g, o, proj, proj, proj, w_pool_out, w_glu_val, w_glu_gate, w_xa_out, w_out, h, g_post)


def _block_diag_in(bb):
    gpc = SSM_GROUPS // SSM_CHUNKS
    x = bb.reshape(SSM_CHUNKS, gpc, SSM_STATE, SSM_GROUP_CH).transpose(0, 1, 3, 2)
    eye = jnp.eye(gpc, dtype=bb.dtype)
    y = x[:, :, :, None, :] * eye[None, :, None, :, None]
    return y.reshape(SSM_CHUNKS, gpc * SSM_GROUP_CH, gpc * SSM_STATE)


def _block_diag_out(c):
    gpc = SSM_GROUPS // SSM_CHUNKS
    x = c.reshape(SSM_CHUNKS, gpc, SSM_GROUP_CH, SSM_STATE).transpose(0, 1, 3, 2)
    eye = jnp.eye(gpc, dtype=c.dtype)
    y = x[:, :, :, None, :] * eye[None, :, None, :, None]
    return y.reshape(SSM_CHUNKS, gpc * SSM_STATE, gpc * SSM_GROUP_CH)


def _scan_table(pw_re, pw_im):
    pw_re = pw_re.reshape(SUBLANES, SSM_LANES)
    pw_im = pw_im.reshape(SUBLANES, SSM_LANES)
    row = jnp.arange(SUBLANES)[:, None]
    tiles = []
    for d in (1, 2, 4):
        keep = row >= d
        tiles.append(jnp.where(keep, pw_re[d - 1][None, :], 0.0))
        tiles.append(jnp.where(keep, pw_im[d - 1][None, :], 0.0))
    tiles += [pw_re, pw_im]
    return jnp.stack(tiles)


def _row_tile(m, target):
    return target if m % target == 0 else m


def _layer(x, w, mixers):
    m = x.shape[0]
    tm = _row_tile(m, 512)
    h = _ffn(x, w['g_ff1_pre'], w['w_ff1_gate'], w['w_ff1_up'], w['w_ff1_down'], w['g_ff1_post'], tm, 512)
    proj = _normproj(h, w['g_mix_pre'], w['w_in'], tm, 1024)
    z, g, o, states = mixers(proj)
    h = _merge(z, g, o, proj, w['w_pool_out'], w['w_glu_val'], w['w_glu_gate'], w['w_xa_out'], w['w_out'],
               h, w['g_mix_post'], tm, 512)
    h = _ffn(h, w['g_ff2_pre'], w['w_ff2_gate'], w['w_ff2_up'], w['w_ff2_down'], w['g_ff2_post'], tm, 512)
    return h, states


def kernel(x_prompt, x_sample, mem_prompt, cache_mem_k, cache_mem_v, state_pool, state_ssm_re, state_ssm_im, g_ff1_pre, w_ff1_gate, w_ff1_up, w_ff1_down, g_ff1_post, g_mix_pre, w_in, w_pool_grp, pool_scale, w_pool_out, ssm_a_re, ssm_a_im, ssm_log_step, ssm_b_re, ssm_b_im, ssm_c_re, ssm_c_im, ssm_d, w_glu_val, w_glu_gate, g_mem, w_mem_k, w_mem_v, w_xa_out, w_out, g_mix_post, g_ff2_pre, w_ff2_gate, w_ff2_up, w_ff2_down, g_ff2_post):
    bp, lp, d = x_prompt.shape
    bs, ls, _ = x_sample.shape
    depth = w_in.shape[0]
    assert depth == 1 and ls == 1 and lp >= POOL_BUF

    mats = dict(w_ff1_gate=w_ff1_gate, w_ff1_up=w_ff1_up, w_ff1_down=w_ff1_down, w_in=w_in,
                w_pool_out=w_pool_out, w_glu_val=w_glu_val, w_glu_gate=w_glu_gate, w_xa_out=w_xa_out,
                w_out=w_out, w_ff2_gate=w_ff2_gate, w_ff2_up=w_ff2_up, w_ff2_down=w_ff2_down,
                w_mem_k=w_mem_k, w_mem_v=w_mem_v)
    gains = dict(g_ff1_pre=g_ff1_pre, g_ff1_post=g_ff1_post, g_mix_pre=g_mix_pre, g_mix_post=g_mix_post,
                 g_ff2_pre=g_ff2_pre, g_ff2_post=g_ff2_post, g_mem=g_mem)
    w = {k: v[0].astype(BF16) for k, v in mats.items()}
    w.update({k: v for k, v in gains.items()})
    w_grp = w_pool_grp[0].astype(BF16)
    scale = pool_scale
    d_skip = ssm_d

    bb_re, bb_im, pw_re, pw_im = _ssm_disc(ssm_a_re[0], ssm_a_im[0], ssm_log_step[0], ssm_b_re[0], ssm_b_im[0])
    bdr = _block_diag_in(bb_re.astype(BF16))
    bdi = _block_diag_in(bb_im.astype(BF16))
    cdr = _block_diag_out(ssm_c_re[0].astype(BF16))
    cdi = _block_diag_out(ssm_c_im[0].astype(BF16))
    tab = _scan_table(pw_re, pw_im)
    lb = jnp.stack([pw_re[0].reshape(SSM_LANES), pw_im[0].reshape(SSM_LANES)])

    mem = mem_prompt.reshape(bp * N_MEM, d)
    mem_k = _normproj(mem, w['g_mem'], w['w_mem_k'], 512, 1024)
    mem_v = _normproj(mem, w['g_mem'], w['w_mem_v'], 512, 1024)

    def prompt_mixers(proj):
        z = _pool_seq(proj, w_grp, scale, bp, lp, 512)
        zeros = jnp.zeros((bp, 1, SSM_LANES), F32)
        g, last_re, last_im = _ssm_seq(proj, zeros, zeros, tab, bdr, bdi, cdr, cdi, d_skip, bp, lp, 256)
        o = _xa_seq(proj, mem_k, mem_v, bp, lp, 512)
        pool_rows = proj.reshape(bp, lp, IN_WIDTH)[:, lp - POOL_BUF:, :POOL_WIDTH]
        return z, g, o, (pool_rows, last_re, last_im)

    y_prompt, (pool_p, re_p, im_p) = _layer(x_prompt.reshape(bp * lp, d), w, prompt_mixers)

    def sample_mixers(proj):
        z, new_buf = _pool_step(proj, state_pool[0].reshape(bs, POOL_BUF * POOL_WIDTH), w_grp, scale, 32)
        g, x_re, x_im = _ssm_step(proj, state_ssm_re[0].reshape(bs, SSM_LANES),
                                  state_ssm_im[0].reshape(bs, SSM_LANES), lb, bdr, bdi, cdr, cdi, d_skip)
        q3 = proj[:, POOL_WIDTH + SSM_WIDTH:OFF_GATE].reshape(bs, 1, XA_WIDTH)
        o = _xa_step(q3, cache_mem_k[0].reshape(bs, N_MEM, XA_WIDTH),
                     cache_mem_v[0].reshape(bs, N_MEM, XA_WIDTH), 4)
        return z, g, o.reshape(bs, XA_WIDTH), (new_buf, x_re, x_im)

    y_sample, (pool_s, re_s, im_s) = _layer(x_sample.reshape(bs * ls, d), w, sample_mixers)

    kv_shape = (1, bp, N_MEM, XA_HEADS, XA_HEAD_DIM)
    st_shape = (SSM_GROUPS, SSM_STATE)
    return (y_prompt.reshape(bp, lp, d), y_sample.reshape(bs, ls, d),
            mem_k.reshape(kv_shape), mem_v.reshape(kv_shape),
            pool_p[None],
            re_p.reshape((1, bp) + st_shape), im_p.reshape((1, bp) + st_shape),
            pool_s.reshape(1, bs, POOL_BUF, POOL_WIDTH),
            re_s.reshape((1, bs) + st_shape), im_s.reshape((1, bs) + st_shape))
```

```python
import functools
import math

import jax
import jax.numpy as jnp
from jax import lax
from jax.experimental import pallas as pl
from jax.experimental.pallas import tpu as pltpu

F32 = jnp.float32
BF16 = jnp.bfloat16

D_MODEL = 2048
PAST_LEN = 16384
POOL_WIDTH = D_MODEL // 2
POOL_WINDOWS = (2, 4, 8, 16)
POOL_GROUP_DIM = POOL_WIDTH // len(POOL_WINDOWS)
POOL_BUF = max(POOL_WINDOWS) - 1
SSM_WIDTH = D_MODEL // 2
SSM_GROUP_CH = 16
SSM_GROUPS = SSM_WIDTH // SSM_GROUP_CH
SSM_STATE = 64
SSM_LANES = SSM_GROUPS * SSM_STATE
XA_HEADS = 4
XA_HEAD_DIM = D_MODEL // 8
XA_WIDTH = XA_HEADS * XA_HEAD_DIM
XA_SCALE = XA_HEAD_DIM ** -0.5
N_MEM = 256
OFF_GATE = POOL_WIDTH + SSM_WIDTH + XA_WIDTH
IN_WIDTH = OFF_GATE + 3 * D_MODEL
RMS_EPS = 1e-6

SUBLANES = 8
LANES = 128
MXU_DIM = 256
VMEM_LIMIT_BYTES = 56 * 1024 * 1024

SSM_CHUNKS = SSM_WIDTH // MXU_DIM
SSM_CHUNK_LANES = SSM_LANES // SSM_CHUNKS
SSM_TIME_TILE = 256
SSM_SEG = SSM_TIME_TILE // SUBLANES
SCAN_LANES = 4 * LANES


def _params(*sem):
    return pltpu.CompilerParams(dimension_semantics=sem, vmem_limit_bytes=VMEM_LIMIT_BYTES)


def _rms(x, g):
    return (x * lax.rsqrt(jnp.mean(x * x, axis=-1, keepdims=True) + RMS_EPS)) * g


def _dot(a, b):
    return jnp.dot(a, b, preferred_element_type=F32)


def _cmul(a_re, a_im, b_re, b_im):
    return a_re * b_re - a_im * b_im, a_re * b_im + a_im * b_re


def _ffn_kernel(x_ref, gpre_ref, wg_ref, wu_ref, wd_ref, gpost_ref, o_ref, xn_ref, acc_ref):
    j = pl.program_id(1)

    @pl.when(j == 0)
    def _():
        xn_ref[...] = _rms(x_ref[...], gpre_ref[...]).astype(BF16)
        acc_ref[...] = jnp.zeros_like(acc_ref)

    xn = xn_ref[...]
    g = _dot(xn, wg_ref[...])
    u = _dot(xn, wu_ref[...])
    hid = (g * jax.nn.sigmoid(g)) * u
    acc_ref[...] += _dot(hid.astype(BF16), wd_ref[...])

    @pl.when(j == pl.num_programs(1) - 1)
    def _():
        o_ref[...] = x_ref[...] + 0.5 * _rms(acc_ref[...], gpost_ref[...])


def _ffn(x, g_pre, w_gate, w_up, w_down, g_post, tm, tf):
    m, d = x.shape
    f = w_gate.shape[1]
    return pl.pallas_call(
        _ffn_kernel,
        grid=(m // tm, f // tf),
        in_specs=[
            pl.BlockSpec((tm, d), lambda i, j: (i, 0)),
            pl.BlockSpec((1, d), lambda i, j: (0, 0)),
            pl.BlockSpec((d, tf), lambda i, j: (0, j)),
            pl.BlockSpec((d, tf), lambda i, j: (0, j)),
            pl.BlockSpec((tf, d), lambda i, j: (j, 0)),
            pl.BlockSpec((1, d), lambda i, j: (0, 0)),
        ],
        out_specs=pl.BlockSpec((tm, d), lambda i, j: (i, 0)),
        out_shape=jax.ShapeDtypeStruct((m, d), F32),
        scratch_shapes=[pltpu.VMEM((tm, d), BF16), pltpu.VMEM((tm, d), F32)],
        compiler_params=_params("parallel", "arbitrary"),
        name="ffn",
    )(x, g_pre, w_gate, w_up, w_down, g_post)


def _normproj_kernel(x_ref, g_ref, w_ref, o_ref, xn_ref):
    @pl.when(pl.program_id(1) == 0)
    def _():
        xn_ref[...] = _rms(x_ref[...], g_ref[...]).astype(BF16)

    o_ref[...] = _dot(xn_ref[...], w_ref[...])


def _normproj(x, g, w, tm, tn):
    m, d = x.shape
    n = w.shape[1]
    return pl.pallas_call(
        _normproj_kernel,
        grid=(m // tm, n // tn),
        in_specs=[
            pl.BlockSpec((tm, d), lambda i, j: (i, 0)),
            pl.BlockSpec((1, d), lambda i, j: (0, 0)),
            pl.BlockSpec((d, tn), lambda i, j: (0, j)),
        ],
        out_specs=pl.BlockSpec((tm, tn), lambda i, j: (i, j)),
        out_shape=jax.ShapeDtypeStruct((m, n), F32),
        scratch_shapes=[pltpu.VMEM((tm, d), BF16)],
        compiler_params=_params("parallel", "arbitrary"),
        name="normproj",
    )(x, g, w)


def _pool_group_out(k, diff, wgrp_ref, scale_ref, z_ref):
    ch = slice(k * POOL_GROUP_DIM, (k + 1) * POOL_GROUP_DIM)
    zk = _dot(diff.astype(BF16), wgrp_ref[k]) * scale_ref[:, ch]
    z_ref[:, ch] = zk.astype(BF16)


def _pool_seq_kernel(u_ref, wgrp_ref, scale_ref, z_ref, ext_ref, *, halo):
    t = pl.program_id(1)
    tt = u_ref.shape[0]

    @pl.when(t == 0)
    def _():
        ext_ref[0:halo, :] = jnp.zeros((halo, POOL_WIDTH), F32)

    u = u_ref[...]
    ext_ref[halo:halo + tt, :] = u
    pos = t * tt + lax.broadcasted_iota(jnp.int32, (tt, 1), 0)
    for k, w in enumerate(POOL_WINDOWS):
        ch = slice(k * POOL_GROUP_DIM, (k + 1) * POOL_GROUP_DIM)
        a = ext_ref[:, ch]
        d = 1
        while d < w:
            a = a + pltpu.roll(a, d, axis=0)
            d *= 2
        count = jnp.minimum(pos + 1, w).astype(F32)
        diff = a[halo:, :] / count - u[:, ch]
        _pool_group_out(k, diff, wgrp_ref, scale_ref, z_ref)
    ext_ref[0:halo, :] = ext_ref[tt:tt + halo, :]


def _pool_seq(proj, w_grp, scale, batch, seq, tt):
    halo = 2 * SUBLANES
    nt = seq // tt
    return pl.pallas_call(
        functools.partial(_pool_seq_kernel, halo=halo),
        grid=(batch, nt),
        in_specs=[
            pl.BlockSpec((tt, POOL_WIDTH), lambda b, t: (b * nt + t, 0)),
            pl.BlockSpec(w_grp.shape, lambda b, t: (0, 0, 0)),
            pl.BlockSpec((1, POOL_WIDTH), lambda b, t: (0, 0)),
        ],
        out_specs=pl.BlockSpec((tt, POOL_WIDTH), lambda b, t: (b * nt + t, 0)),
        out_shape=jax.ShapeDtypeStruct((batch * seq, POOL_WIDTH), BF16),
        scratch_shapes=[pltpu.VMEM((tt + halo, POOL_WIDTH), F32)],
        compiler_params=_params("parallel", "arbitrary"),
        name="pool_seq",
    )(proj, w_grp, scale)


def _pool_step_kernel(u_ref, buf_ref, wgrp_ref, scale_ref, z_ref, nbuf_ref):
    u = u_ref[...]
    for k, w in enumerate(POOL_WINDOWS):
        ch = slice(k * POOL_GROUP_DIM, (k + 1) * POOL_GROUP_DIM)
        s = u[:, ch]
        for j in range(1, w):
            row = POOL_BUF - j
            s = s + buf_ref[:, row * POOL_WIDTH + k * POOL_GROUP_DIM:
                            row * POOL_WIDTH + (k + 1) * POOL_GROUP_DIM]
        diff = s / float(min(PAST_LEN + 1, w)) - u[:, ch]
        _pool_group_out(k, diff, wgrp_ref, scale_ref, z_ref)
    keep = (POOL_BUF - 1) * POOL_WIDTH
    nbuf_ref[:, 0:keep] = buf_ref[:, POOL_WIDTH:]
    nbuf_ref[:, keep:] = u


def _pool_step(proj, buf, w_grp, scale, tb):
    m = proj.shape[0]
    return pl.pallas_call(
        _pool_step_kernel,
        grid=(m // tb,),
        in_specs=[
            pl.BlockSpec((tb, POOL_WIDTH), lambda i: (i, 0)),
            pl.BlockSpec((tb, POOL_BUF * POOL_WIDTH), lambda i: (i, 0)),
            pl.BlockSpec(w_grp.shape, lambda i: (0, 0, 0)),
            pl.BlockSpec((1, POOL_WIDTH), lambda i: (0, 0)),
        ],
        out_specs=[
            pl.BlockSpec((tb, POOL_WIDTH), lambda i: (i, 0)),
            pl.BlockSpec((tb, POOL_BUF * POOL_WIDTH), lambda i: (i, 0)),
        ],
        out_shape=[
            jax.ShapeDtypeStruct((m, POOL_WIDTH), BF16),
            jax.ShapeDtypeStruct((m, POOL_BUF * POOL_WIDTH), F32),
        ],
        compiler_params=_params("parallel"),
        name="pool_step",
    )(proj, buf, w_grp, scale)


def _zoh_lambda(a_re, a_im, log_step):
    dt = jnp.exp(log_step)
    mag = jnp.exp(a_re * dt)
    ang = a_im * dt
    return mag * jnp.cos(ang), mag * jnp.sin(ang)


def _ssm_disc_kernel(a_re_ref, a_im_ref, ls_ref, a_re_col_ref, a_im_col_ref, ls_col_ref, b_re_ref, b_im_ref,
                     bb_re_ref, bb_im_ref, pw_re_ref, pw_im_ref, ps_re_ref, ps_im_ref):
    a_re = a_re_col_ref[...]
    a_im = a_im_col_ref[...]
    lb_re, lb_im = _zoh_lambda(a_re, a_im, ls_col_ref[...])
    den = a_re * a_re + a_im * a_im
    n_re = lb_re - 1.0
    f_re = (n_re * a_re + lb_im * a_im) / den
    f_im = (lb_im * a_re - n_re * a_im) / den
    b_re = b_re_ref[...]
    b_im = b_im_ref[...]
    bb_re_ref[...] = f_re * b_re - f_im * b_im
    bb_im_ref[...] = f_re * b_im + f_im * b_re

    lb_re, lb_im = _zoh_lambda(a_re_ref[...], a_im_ref[...], ls_ref[...])

    def powers(base_re, base_im, out_re_ref, out_im_ref):
        p_re, p_im = base_re, base_im
        out_re_ref[0] = p_re
        out_im_ref[0] = p_im
        for r in range(1, SUBLANES):
            p_re, p_im = _cmul(p_re, p_im, base_re, base_im)
            out_re_ref[r] = p_re
            out_im_ref[r] = p_im
        return p_re, p_im

    s_re, s_im = powers(lb_re, lb_im, pw_re_ref, pw_im_ref)
    for _ in range(int(math.log2(SSM_SEG // SUBLANES))):
        s_re, s_im = _cmul(s_re, s_im, s_re, s_im)
    powers(s_re, s_im, ps_re_ref, ps_im_ref)


def _ssm_disc(a_re, a_im, log_step, b_re, b_im):
    g, n, h = b_re.shape
    col = (g * n, 1)
    pw = jax.ShapeDtypeStruct((SUBLANES, g, n), F32)
    bb = jax.ShapeDtypeStruct((g * n, h), F32)
    bb_re, bb_im, pw_re, pw_im, ps_re, ps_im = pl.pallas_call(
        _ssm_disc_kernel,
        out_shape=[bb, bb, pw, pw, pw, pw],
        name="ssm_disc",
    )(a_re, a_im, log_step[:, None],
      a_re.reshape(col), a_im.reshape(col), jnp.broadcast_to(log_step[:, None], (g, n)).reshape(col),
      b_re.reshape(g * n, h), b_im.reshape(g * n, h))
    return bb_re.reshape(g, n, h), bb_im.reshape(g, n, h), pw_re, pw_im, ps_re, ps_im


def _ssm_in(ub, bd_ref, x_ref):
    for c in range(SSM_CHUNKS):
        x_ref[:, c * SSM_CHUNK_LANES:(c + 1) * SSM_CHUNK_LANES] = _dot(
            ub[:, c * MXU_DIM:(c + 1) * MXU_DIM], bd_ref[c])


def _ssm_out(c, u, x_re_ref, x_im_ref, cdr_ref, cdi_ref, d_ref):
    ch = slice(c * MXU_DIM, (c + 1) * MXU_DIM)
    st = slice(c * SSM_CHUNK_LANES, (c + 1) * SSM_CHUNK_LANES)
    y = _dot(x_re_ref[:, st].astype(BF16), cdr_ref[c]) - _dot(x_im_ref[:, st].astype(BF16), cdi_ref[c])
    return jax.nn.gelu(y + d_ref[:, ch] * u[:, ch])


def _ssm_seq_kernel(u_ref, h_re_ref, h_im_ref, tab_ref, bdr_ref, bdi_ref, cdr_ref, cdi_ref, d_ref,
                    g_ref, last_re_ref, last_im_ref, xr_ref, xi_ref, cr_ref, ci_ref):
    t = pl.program_id(1)
    tt = u_ref.shape[0]
    seg = tt // SUBLANES

    @pl.when(t == 0)
    def _():
        cr_ref[...] = jnp.broadcast_to(h_re_ref[...], cr_ref.shape)
        ci_ref[...] = jnp.broadcast_to(h_im_ref[...], ci_ref.shape)

    def regroup(x, rows_major, rows_minor):
        w = x.shape[1]
        return jnp.swapaxes(x.reshape(rows_major, rows_minor, w), 0, 1).reshape(tt, w)

    u = regroup(u_ref[...], SUBLANES, seg)
    ub = u.astype(BF16)
    _ssm_in(ub, bdr_ref, xr_ref)
    _ssm_in(ub, bdi_ref, xi_ref)

    first_row = lax.broadcasted_iota(jnp.int32, (SUBLANES, SCAN_LANES), 0) == 0
    for lc in range(SSM_LANES // SCAN_LANES):
        sl = slice(lc * SCAN_LANES, (lc + 1) * SCAN_LANES)
        l_re = tab_ref[0, :, sl]
        l_im = tab_ref[1, :, sl]

        def rows(j):
            return pl.ds(pl.multiple_of(j * SUBLANES, SUBLANES), SUBLANES)

        def advance(j, x, sl=sl, l_re=l_re, l_im=l_im):
            x_re, x_im = _cmul(l_re, l_im, *x)
            x_re = x_re + xr_ref[rows(j), sl]
            x_im = x_im + xi_ref[rows(j), sl]
            xr_ref[rows(j), sl] = x_re
            xi_ref[rows(j), sl] = x_im
            return x_re, x_im

        zero = jnp.zeros((SUBLANES, SCAN_LANES), F32)
        f_re, f_im = lax.fori_loop(0, seg, advance, (zero, zero), unroll=4)

        c_re = cr_ref[:, sl]
        c_im = ci_ref[:, sl]
        for i in range(3):
            s_re = pltpu.roll(f_re, 1 << i, axis=0)
            s_im = pltpu.roll(f_im, 1 << i, axis=0)
            m_re, m_im = _cmul(tab_ref[2 + 2 * i, :, sl], tab_ref[3 + 2 * i, :, sl], s_re, s_im)
            f_re, f_im = f_re + m_re, f_im + m_im
        m_re, m_im = _cmul(tab_ref[8, :, sl], tab_ref[9, :, sl], c_re, c_im)
        f_re, f_im = f_re + m_re, f_im + m_im
        e_re = jnp.where(first_row, c_re, pltpu.roll(f_re, 1, axis=0))
        e_im = jnp.where(first_row, c_im, pltpu.roll(f_im, 1, axis=0))
        cr_ref[:, sl] = jnp.broadcast_to(f_re[SUBLANES - 1:SUBLANES, :], (SUBLANES, SCAN_LANES))
        ci_ref[:, sl] = jnp.broadcast_to(f_im[SUBLANES - 1:SUBLANES, :], (SUBLANES, SCAN_LANES))

        def fix(j, w, sl=sl, l_re=l_re, l_im=l_im):
            w_re, w_im = _cmul(l_re, l_im, *w)
            xr_ref[rows(j), sl] += w_re
            xi_ref[rows(j), sl] += w_im
            return w_re, w_im

        lax.fori_loop(0, seg, fix, (e_re, e_im), unroll=4)

    for c in range(SSM_CHUNKS):
        ch = slice(c * MXU_DIM, (c + 1) * MXU_DIM)
        g = _ssm_out(c, u, xr_ref, xi_ref, cdr_ref, cdi_ref, d_ref)
        g_ref[:, ch] = regroup(g, seg, SUBLANES).astype(BF16)

    @pl.when(t == pl.num_programs(1) - 1)
    def _():
        last_re_ref[...] = cr_ref[0:1, :]
        last_im_ref[...] = ci_ref[0:1, :]


def _const_spec(shape):
    return pl.BlockSpec(shape, lambda *_: (0,) * len(shape))


def _ssm_seq(proj, h_re, h_im, tab, bdr, bdi, cdr, cdi, d_skip, batch, seq):
    tt = SSM_TIME_TILE
    nt = seq // tt
    col_block = POOL_WIDTH // SSM_WIDTH
    state_spec = pl.BlockSpec((None, 1, SSM_LANES), lambda b, t: (b, 0, 0))
    state_shape = jax.ShapeDtypeStruct((batch, 1, SSM_LANES), F32)
    return pl.pallas_call(
        _ssm_seq_kernel,
        grid=(batch, nt),
        in_specs=[
            pl.BlockSpec((tt, SSM_WIDTH), lambda b, t: (b * nt + t, col_block)),
            state_spec, state_spec,
            _const_spec(tab.shape),
            _const_spec(bdr.shape), _const_spec(bdi.shape),
            _const_spec(cdr.shape), _const_spec(cdi.shape),
            _const_spec(d_skip.shape),
        ],
        out_specs=[pl.BlockSpec((tt, SSM_WIDTH), lambda b, t: (b * nt + t, 0)), state_spec, state_spec],
        out_shape=[jax.ShapeDtypeStruct((batch * seq, SSM_WIDTH), BF16), state_shape, state_shape],
        scratch_shapes=[
            pltpu.VMEM((tt, SSM_LANES), F32), pltpu.VMEM((tt, SSM_LANES), F32),
            pltpu.VMEM((SUBLANES, SSM_LANES), F32), pltpu.VMEM((SUBLANES, SSM_LANES), F32),
        ],
        compiler_params=_params("parallel", "arbitrary"),
        name="ssm_seq",
    )(proj, h_re, h_im, tab, bdr, bdi, cdr, cdi, d_skip)


def _ssm_step_kernel(u_ref, h_re_ref, h_im_ref, lb_ref, bdr_ref, bdi_ref, cdr_ref, cdi_ref, d_ref,
                     g_ref, x_re_ref, x_im_ref):
    u = u_ref[...]
    ub = u.astype(BF16)
    _ssm_in(ub, bdr_ref, x_re_ref)
    _ssm_in(ub, bdi_ref, x_im_ref)
    m_re, m_im = _cmul(lb_ref[0:1, :], lb_ref[1:2, :], h_re_ref[...], h_im_ref[...])
    x_re_ref[...] += m_re
    x_im_ref[...] += m_im
    for c in range(SSM_CHUNKS):
        ch = slice(c * MXU_DIM, (c + 1) * MXU_DIM)
        g_ref[:, ch] = _ssm_out(c, u, x_re_ref, x_im_ref, cdr_ref, cdi_ref, d_ref).astype(BF16)


def _ssm_step(proj, h_re, h_im, lb, bdr, bdi, cdr, cdi, d_skip):
    m = proj.shape[0]
    state_spec = pl.BlockSpec((m, SSM_LANES), lambda i: (0, 0))
    state_shape = jax.ShapeDtypeStruct((m, SSM_LANES), F32)
    return pl.pallas_call(
        _ssm_step_kernel,
        grid=(1,),
        in_specs=[
            pl.BlockSpec((m, SSM_WIDTH), lambda i: (0, POOL_WIDTH // SSM_WIDTH)),
            state_spec, state_spec,
            _const_spec(lb.shape),
            _const_spec(bdr.shape), _const_spec(bdi.shape),
            _const_spec(cdr.shape), _const_spec(cdi.shape),
            _const_spec(d_skip.shape),
        ],
        out_specs=[pl.BlockSpec((m, SSM_WIDTH), lambda i: (0, 0)), state_spec, state_spec],
        out_shape=[jax.ShapeDtypeStruct((m, SSM_WIDTH), BF16), state_shape, state_shape],
        compiler_params=_params("arbitrary"),
        name="ssm_step",
    )(proj, h_re, h_im, lb, bdr, bdi, cdr, cdi, d_skip)


def _xa_seq_kernel(q_ref, k_ref, v_ref, o_ref):
    for h in range(XA_HEADS):
        ch = slice(h * XA_HEAD_DIM, (h + 1) * XA_HEAD_DIM)
        q = q_ref[:, ch].astype(BF16)
        k = k_ref[:, ch].astype(BF16)
        v = v_ref[:, ch].astype(BF16)
        s = lax.dot_general(q, k, (((1,), (1,)), ((), ())), preferred_element_type=F32) * XA_SCALE
        e = jnp.exp(s - jnp.max(s, axis=-1, keepdims=True))
        p = e / jnp.sum(e, axis=-1, keepdims=True)
        o_ref[:, ch] = _dot(p.astype(BF16), v).astype(BF16)


def _xa_seq(proj, mem_k, mem_v, batch, seq, tq):
    nt = seq // tq
    q_block = (POOL_WIDTH + SSM_WIDTH) // XA_WIDTH
    kv_spec = pl.BlockSpec((N_MEM, XA_WIDTH), lambda b, t: (b, 0))
    return pl.pallas_call(
        _xa_seq_kernel,
        grid=(batch, nt),
        in_specs=[pl.BlockSpec((tq, XA_WIDTH), lambda b, t: (b * nt + t, q_block)), kv_spec, kv_spec],
        out_specs=pl.BlockSpec((tq, XA_WIDTH), lambda b, t: (b * nt + t, 0)),
        out_shape=jax.ShapeDtypeStruct((batch * seq, XA_WIDTH), BF16),
        compiler_params=_params("parallel", "parallel"),
        name="xa_seq",
    )(proj, mem_k, mem_v)


def _xa_step_kernel(q_ref, k_ref, v_ref, o_ref):
    q = q_ref[...]
    s = jnp.sum(k_ref[...] * q, axis=-1, keepdims=True) * XA_SCALE
    e = jnp.exp(s - jnp.max(s, axis=1, keepdims=True))
    p = e / jnp.sum(e, axis=1, keepdims=True)
    o_ref[...] = jnp.sum(p * v_ref[...], axis=1, keepdims=True)


def _xa_step(q4, cache_k, cache_v, tb):
    m = q4.shape[0]
    kv_spec = pl.BlockSpec((tb, N_MEM, XA_HEADS, XA_HEAD_DIM), lambda i: (i, 0, 0, 0))
    q_spec = pl.BlockSpec((tb, 1, XA_HEADS, XA_HEAD_DIM), lambda i: (i, 0, 0, 0))
    return pl.pallas_call(
        _xa_step_kernel,
        grid=(m // tb,),
        in_specs=[q_spec, kv_spec, kv_spec],
        out_specs=q_spec,
        out_shape=jax.ShapeDtypeStruct(q4.shape, F32),
        compiler_params=_params("parallel"),
        name="xa_step",
    )(q4, cache_k, cache_v)


def _merge_kernel(z_ref, g_ref, o_ref, ga_ref, gb_ref, gc_ref, wpo_ref, wv_ref, wg_ref, wxo_ref, wout_ref,
                  h_ref, gpost_ref, out_ref, acc_ref):
    n = pl.program_id(1)

    @pl.when(n == 0)
    def _():
        acc_ref[...] = jnp.zeros_like(acc_ref)

    g = g_ref[...]
    o_pool = _dot(z_ref[...], wpo_ref[...])
    o_ssm = _dot(g, wv_ref[...]) * jax.nn.sigmoid(_dot(g, wg_ref[...]))
    o_xa = _dot(o_ref[...], wxo_ref[...])
    merged = (jax.nn.sigmoid(ga_ref[...]) * o_pool + jax.nn.sigmoid(gb_ref[...]) * o_ssm
              + jax.nn.sigmoid(gc_ref[...]) * o_xa)
    acc_ref[...] += _dot(merged.astype(BF16), wout_ref[...])

    @pl.when(n == pl.num_programs(1) - 1)
    def _():
        out_ref[...] = h_ref[...] + _rms(acc_ref[...], gpost_ref[...])


def _merge(z, g, o, proj, w_pool_out, w_glu_val, w_glu_gate, w_xa_out, w_out, h, g_post, tm, tn):
    m, d = h.shape
    half = z.shape[1]
    gate0 = OFF_GATE // tn
    per_gate = d // tn
    branch_spec = pl.BlockSpec((tm, half), lambda i, n: (i, 0))
    w_spec = pl.BlockSpec((half, tn), lambda i, n: (0, n))

    def gate_spec(k):
        return pl.BlockSpec((tm, tn), lambda i, n: (i, gate0 + k * per_gate + n))

    return pl.pallas_call(
        _merge_kernel,
        grid=(m // tm, d // tn),
        in_specs=[
            branch_spec, branch_spec, branch_spec,
            gate_spec(0), gate_spec(1), gate_spec(2),
            w_spec, w_spec, w_spec, w_spec,
            pl.BlockSpec((tn, d), lambda i, n: (n, 0)),
            pl.BlockSpec((tm, d), lambda i, n: (i, 0)),
            pl.BlockSpec((1, d), lambda i, n: (0, 0)),
        ],
        out_specs=pl.BlockSpec((tm, d), lambda i, n: (i, 0)),
        out_shape=jax.ShapeDtypeStruct((m, d), F32),
        scratch_shapes=[pltpu.VMEM((tm, d), F32)],
        compiler_params=_params("parallel", "arbitrary"),
        name="merge",
    )(z, g, o, proj, proj, proj, w_pool_out, w_glu_val, w_glu_gate, w_xa_out, w_out, h, g_post)


def _block_diag_in(bb):
    gpc = SSM_GROUPS // SSM_CHUNKS
    x = bb.reshape(SSM_CHUNKS, gpc, SSM_STATE, SSM_GROUP_CH).transpose(0, 1, 3, 2)
    eye = jnp.eye(gpc, dtype=bb.dtype)
    y = x[:, :, :, None, :] * eye[None, :, None, :, None]
    return y.reshape(SSM_CHUNKS, gpc * SSM_GROUP_CH, gpc * SSM_STATE)


def _block_diag_out(c):
    gpc = SSM_GROUPS // SSM_CHUNKS
    x = c.reshape(SSM_CHUNKS, gpc, SSM_GROUP_CH, SSM_STATE).transpose(0, 1, 3, 2)
    eye = jnp.eye(gpc, dtype=c.dtype)
    y = x[:, :, :, None, :] * eye[None, :, None, :, None]
    return y.reshape(SSM_CHUNKS, gpc * SSM_STATE, gpc * SSM_GROUP_CH)


def _scan_table(pw_re, pw_im, ps_re, ps_im):
    flat = lambda p: p.reshape(SUBLANES, SSM_LANES)
    pw_re, pw_im, ps_re, ps_im = flat(pw_re), flat(pw_im), flat(ps_re), flat(ps_im)
    row = jnp.arange(SUBLANES)[:, None]
    tiles = [jnp.broadcast_to(pw_re[0], (SUBLANES, SSM_LANES)), jnp.broadcast_to(pw_im[0], (SUBLANES, SSM_LANES))]
    for d in (1, 2, 4):
        keep = row >= d
        tiles.append(jnp.where(keep, ps_re[d - 1][None, :], 0.0))
        tiles.append(jnp.where(keep, ps_im[d - 1][None, :], 0.0))
    tiles += [ps_re, ps_im]
    return jnp.stack(tiles)


def _row_tile(m, target):
    return target if m % target == 0 else m


def _layer(x, w, mixers):
    m = x.shape[0]
    tm = _row_tile(m, 512)
    h = _ffn(x, w['g_ff1_pre'], w['w_ff1_gate'], w['w_ff1_up'], w['w_ff1_down'], w['g_ff1_post'], tm, 512)
    proj = _normproj(h, w['g_mix_pre'], w['w_in'], _row_tile(m, 1024), 512)
    z, g, o, states = mixers(proj)
    h = _merge(z, g, o, proj, w['w_pool_out'], w['w_glu_val'], w['w_glu_gate'], w['w_xa_out'], w['w_out'],
               h, w['g_mix_post'], tm, 512)
    h = _ffn(h, w['g_ff2_pre'], w['w_ff2_gate'], w['w_ff2_up'], w['w_ff2_down'], w['g_ff2_post'], tm, 512)
    return h, states


def kernel(x_prompt, x_sample, mem_prompt, cache_mem_k, cache_mem_v, state_pool, state_ssm_re, state_ssm_im, g_ff1_pre, w_ff1_gate, w_ff1_up, w_ff1_down, g_ff1_post, g_mix_pre, w_in, w_pool_grp, pool_scale, w_pool_out, ssm_a_re, ssm_a_im, ssm_log_step, ssm_b_re, ssm_b_im, ssm_c_re, ssm_c_im, ssm_d, w_glu_val, w_glu_gate, g_mem, w_mem_k, w_mem_v, w_xa_out, w_out, g_mix_post, g_ff2_pre, w_ff2_gate, w_ff2_up, w_ff2_down, g_ff2_post):
    bp, lp, d = x_prompt.shape
    bs, ls, _ = x_sample.shape
    depth = w_in.shape[0]
    assert depth == 1 and ls == 1 and lp >= POOL_BUF and lp % SSM_TIME_TILE == 0

    mats = dict(w_ff1_gate=w_ff1_gate, w_ff1_up=w_ff1_up, w_ff1_down=w_ff1_down, w_in=w_in,
                w_pool_out=w_pool_out, w_glu_val=w_glu_val, w_glu_gate=w_glu_gate, w_xa_out=w_xa_out,
                w_out=w_out, w_ff2_gate=w_ff2_gate, w_ff2_up=w_ff2_up, w_ff2_down=w_ff2_down,
                w_mem_k=w_mem_k, w_mem_v=w_mem_v)
    gains = dict(g_ff1_pre=g_ff1_pre, g_ff1_post=g_ff1_post, g_mix_pre=g_mix_pre, g_mix_post=g_mix_post,
                 g_ff2_pre=g_ff2_pre, g_ff2_post=g_ff2_post, g_mem=g_mem)
    w = {k: v[0].astype(BF16) for k, v in mats.items()}
    w.update({k: v for k, v in gains.items()})
    w_grp = w_pool_grp[0].astype(BF16)
    scale = pool_scale
    d_skip = ssm_d

    bb_re, bb_im, pw_re, pw_im, ps_re, ps_im = _ssm_disc(
        ssm_a_re[0], ssm_a_im[0], ssm_log_step[0], ssm_b_re[0], ssm_b_im[0])
    bdr = _block_diag_in(bb_re.astype(BF16))
    bdi = _block_diag_in(bb_im.astype(BF16))
    cdr = _block_diag_out(ssm_c_re[0].astype(BF16))
    cdi = _block_diag_out(ssm_c_im[0].astype(BF16))
    tab = _scan_table(pw_re, pw_im, ps_re, ps_im)
    lb = jnp.stack([pw_re[0].reshape(SSM_LANES), pw_im[0].reshape(SSM_LANES)])

    mem = mem_prompt.reshape(bp * N_MEM, d)
    mem_k = _normproj(mem, w['g_mem'], w['w_mem_k'], 512, 1024)
    mem_v = _normproj(mem, w['g_mem'], w['w_mem_v'], 512, 1024)

    def prompt_mixers(proj):
        z = _pool_seq(proj, w_grp, scale, bp, lp, 512)
        zeros = jnp.zeros((bp, 1, SSM_LANES), F32)
        g, last_re, last_im = _ssm_seq(proj, zeros, zeros, tab, bdr, bdi, cdr, cdi, d_skip, bp, lp)
        o = _xa_seq(proj, mem_k, mem_v, bp, lp, 512)
        pool_rows = proj.reshape(bp, lp, IN_WIDTH)[:, lp - POOL_BUF:, :POOL_WIDTH]
        return z, g, o, (pool_rows, last_re, last_im)

    y_prompt, (pool_p, re_p, im_p) = _layer(x_prompt.reshape(bp * lp, d), w, prompt_mixers)

    def sample_mixers(proj):
        z, new_buf = _pool_step(proj, state_pool[0].reshape(bs, POOL_BUF * POOL_WIDTH), w_grp, scale, 32)
        g, x_re, x_im = _ssm_step(proj, state_ssm_re[0].reshape(bs, SSM_LANES),
                                  state_ssm_im[0].reshape(bs, SSM_LANES), lb, bdr, bdi, cdr, cdi, d_skip)
        q4 = proj[:, POOL_WIDTH + SSM_WIDTH:OFF_GATE].reshape(bs, 1, XA_HEADS, XA_HEAD_DIM)
        o = _xa_step(q4, cache_mem_k[0], cache_mem_v[0], 4)
        return z, g, o.reshape(bs, XA_WIDTH).astype(BF16), (new_buf, x_re, x_im)

    y_sample, (pool_s, re_s, im_s) = _layer(x_sample.reshape(bs * ls, d), w, sample_mixers)

    kv_shape = (1, bp, N_MEM, XA_HEADS, XA_HEAD_DIM)
    st_shape = (SSM_GROUPS, SSM_STATE)
    return (y_prompt.reshape(bp, lp, d), y_sample.reshape(bs, ls, d),
            mem_k.reshape(kv_shape), mem_v.reshape(kv_shape),
            pool_p[None],
            re_p.reshape((1, bp) + st_shape), im_p.reshape((1, bp) + st_shape),
            pool_s.reshape(1, bs, POOL_BUF, POOL_WIDTH),
            re_s.reshape((1, bs) + st_shape), im_s.reshape((1, bs) + st_shape))
```

```python
import functools
import math

import jax
import jax.numpy as jnp
from jax import lax
from jax.experimental import pallas as pl
from jax.experimental.pallas import tpu as pltpu

F32 = jnp.float32
BF16 = jnp.bfloat16

D_MODEL = 2048
PAST_LEN = 16384
POOL_WIDTH = D_MODEL // 2
POOL_WINDOWS = (2, 4, 8, 16)
POOL_GROUP_DIM = POOL_WIDTH // len(POOL_WINDOWS)
POOL_BUF = max(POOL_WINDOWS) - 1
SSM_WIDTH = D_MODEL // 2
SSM_GROUP_CH = 16
SSM_GROUPS = SSM_WIDTH // SSM_GROUP_CH
SSM_STATE = 64
SSM_LANES = SSM_GROUPS * SSM_STATE
XA_HEADS = 4
XA_HEAD_DIM = D_MODEL // 8
XA_WIDTH = XA_HEADS * XA_HEAD_DIM
XA_SCALE = XA_HEAD_DIM ** -0.5
N_MEM = 256
OFF_GATE = POOL_WIDTH + SSM_WIDTH + XA_WIDTH
IN_WIDTH = OFF_GATE + 3 * D_MODEL
RMS_EPS = 1e-6

SUBLANES = 8
LANES = 128
MXU_DIM = 256
VMEM_LIMIT_BYTES = 60 * 1024 * 1024
DOWN_CHUNKS = 4

SSM_CHUNKS = SSM_WIDTH // MXU_DIM
SSM_CHUNK_LANES = SSM_LANES // SSM_CHUNKS
SSM_TIME_TILE = 256
SSM_SEG = SSM_TIME_TILE // SUBLANES
SCAN_LANES = 4 * LANES


def _params(*sem):
    return pltpu.CompilerParams(dimension_semantics=sem, vmem_limit_bytes=VMEM_LIMIT_BYTES)


def _rms(x, g):
    return (x * lax.rsqrt(jnp.mean(x * x, axis=-1, keepdims=True) + RMS_EPS)) * g


def _dot(a, b):
    return jnp.dot(a, b, preferred_element_type=F32)


def _cmul(a_re, a_im, b_re, b_im):
    return a_re * b_re - a_im * b_im, a_re * b_im + a_im * b_re


def _ffn_kernel(x_ref, gpre_ref, wg_ref, wu_ref, wd_ref, gpost_ref, *rest, emit_next):
    if emit_next:
        gnext_ref, o_ref, xnext_ref, xn_ref = rest
    else:
        o_ref, xn_ref = rest
    j = pl.program_id(1)

    @pl.when(j == 0)
    def _():
        xn_ref[...] = _rms(x_ref[...], gpre_ref[...]).astype(BF16)
        o_ref[...] = jnp.zeros_like(o_ref)

    xn = xn_ref[...]
    g = _dot(xn, wg_ref[...])
    u = _dot(xn, wu_ref[...])
    hid = ((g * jax.nn.sigmoid(g)) * u).astype(BF16)
    cols = wd_ref.shape[1] // DOWN_CHUNKS
    for n in range(DOWN_CHUNKS):
        sl = slice(n * cols, (n + 1) * cols)
        o_ref[:, sl] += _dot(hid, wd_ref[:, sl])

    @pl.when(j == pl.num_programs(1) - 1)
    def _():
        out = x_ref[...] + 0.5 * _rms(o_ref[...], gpost_ref[...])
        o_ref[...] = out
        if emit_next:
            xnext_ref[...] = _rms(out, gnext_ref[...]).astype(BF16)


def _ffn(x, g_pre, w_gate, w_up, w_down, g_post, tm, tf, g_next=None):
    m, d = x.shape
    f = w_gate.shape[1]
    emit_next = g_next is not None
    row_spec = pl.BlockSpec((tm, d), lambda i, j: (i, 0))
    gain_spec = pl.BlockSpec((1, d), lambda i, j: (0, 0))
    out_shape = jax.ShapeDtypeStruct((m, d), F32)
    return pl.pallas_call(
        functools.partial(_ffn_kernel, emit_next=emit_next),
        grid=(m // tm, f // tf),
        in_specs=[
            row_spec, gain_spec,
            pl.BlockSpec((d, tf), lambda i, j: (0, j)),
            pl.BlockSpec((d, tf), lambda i, j: (0, j)),
            pl.BlockSpec((tf, d), lambda i, j: (j, 0)),
            gain_spec,
        ] + ([gain_spec] if emit_next else []),
        out_specs=[row_spec, row_spec] if emit_next else row_spec,
        out_shape=[out_shape, jax.ShapeDtypeStruct((m, d), BF16)] if emit_next else out_shape,
        scratch_shapes=[pltpu.VMEM((tm, d), BF16)],
        compiler_params=_params("parallel", "arbitrary"),
        name="ffn",
    )(*((x, g_pre, w_gate, w_up, w_down, g_post) + ((g_next,) if emit_next else ())))


def _proj_kernel(xn_ref, w_ref, o_ref):
    o_ref[...] = _dot(xn_ref[...], w_ref[...])


def _proj(xn, w, tm, tn):
    m, d = xn.shape
    n = w.shape[1]
    return pl.pallas_call(
        _proj_kernel,
        grid=(m // tm, n // tn),
        in_specs=[pl.BlockSpec((tm, d), lambda i, j: (i, 0)), pl.BlockSpec((d, tn), lambda i, j: (0, j))],
        out_specs=pl.BlockSpec((tm, tn), lambda i, j: (i, j)),
        out_shape=jax.ShapeDtypeStruct((m, n), F32),
        compiler_params=_params("parallel", "arbitrary"),
        name="proj",
    )(xn, w)


def _normproj_kernel(x_ref, g_ref, w_ref, o_ref, xn_ref):
    @pl.when(pl.program_id(1) == 0)
    def _():
        xn_ref[...] = _rms(x_ref[...], g_ref[...]).astype(BF16)

    o_ref[...] = _dot(xn_ref[...], w_ref[...])


def _normproj(x, g, w, tm, tn):
    m, d = x.shape
    n = w.shape[1]
    return pl.pallas_call(
        _normproj_kernel,
        grid=(m // tm, n // tn),
        in_specs=[
            pl.BlockSpec((tm, d), lambda i, j: (i, 0)),
            pl.BlockSpec((1, d), lambda i, j: (0, 0)),
            pl.BlockSpec((d, tn), lambda i, j: (0, j)),
        ],
        out_specs=pl.BlockSpec((tm, tn), lambda i, j: (i, j)),
        out_shape=jax.ShapeDtypeStruct((m, n), F32),
        scratch_shapes=[pltpu.VMEM((tm, d), BF16)],
        compiler_params=_params("parallel", "arbitrary"),
        name="normproj",
    )(x, g, w)


def _pool_group_out(k, diff, wgrp_ref, scale_ref, z_ref):
    ch = slice(k * POOL_GROUP_DIM, (k + 1) * POOL_GROUP_DIM)
    zk = _dot(diff.astype(BF16), wgrp_ref[k]) * scale_ref[:, ch]
    z_ref[:, ch] = zk.astype(BF16)


def _pool_seq_kernel(u_ref, wgrp_ref, scale_ref, z_ref, ext_ref, *, halo):
    t = pl.program_id(1)
    tt = u_ref.shape[0]

    @pl.when(t == 0)
    def _():
        ext_ref[0:halo, :] = jnp.zeros((halo, POOL_WIDTH), F32)

    u = u_ref[...]
    ext_ref[halo:halo + tt, :] = u
    pos = t * tt + lax.broadcasted_iota(jnp.int32, (tt, 1), 0)
    for k, w in enumerate(POOL_WINDOWS):
        ch = slice(k * POOL_GROUP_DIM, (k + 1) * POOL_GROUP_DIM)
        a = ext_ref[:, ch]
        d = 1
        while d < w:
            a = a + pltpu.roll(a, d, axis=0)
            d *= 2
        count = jnp.minimum(pos + 1, w).astype(F32)
        diff = a[halo:, :] / count - u[:, ch]
        _pool_group_out(k, diff, wgrp_ref, scale_ref, z_ref)
    ext_ref[0:halo, :] = ext_ref[tt:tt + halo, :]


def _pool_seq(proj, w_grp, scale, batch, seq, tt):
    halo = 2 * SUBLANES
    nt = seq // tt
    return pl.pallas_call(
        functools.partial(_pool_seq_kernel, halo=halo),
        grid=(batch, nt),
        in_specs=[
            pl.BlockSpec((tt, POOL_WIDTH), lambda b, t: (b * nt + t, 0)),
            pl.BlockSpec(w_grp.shape, lambda b, t: (0, 0, 0)),
            pl.BlockSpec((1, POOL_WIDTH), lambda b, t: (0, 0)),
        ],
        out_specs=pl.BlockSpec((tt, POOL_WIDTH), lambda b, t: (b * nt + t, 0)),
        out_shape=jax.ShapeDtypeStruct((batch * seq, POOL_WIDTH), BF16),
        scratch_shapes=[pltpu.VMEM((tt + halo, POOL_WIDTH), F32)],
        compiler_params=_params("parallel", "arbitrary"),
        name="pool_seq",
    )(proj, w_grp, scale)


def _pool_step_kernel(u_ref, buf_ref, wgrp_ref, scale_ref, z_ref, nbuf_ref):
    u = u_ref[...]
    for k, w in enumerate(POOL_WINDOWS):
        ch = slice(k * POOL_GROUP_DIM, (k + 1) * POOL_GROUP_DIM)
        s = u[:, ch]
        for j in range(1, w):
            row = POOL_BUF - j
            s = s + buf_ref[:, row * POOL_WIDTH + k * POOL_GROUP_DIM:
                            row * POOL_WIDTH + (k + 1) * POOL_GROUP_DIM]
        diff = s / float(min(PAST_LEN + 1, w)) - u[:, ch]
        _pool_group_out(k, diff, wgrp_ref, scale_ref, z_ref)
    keep = (POOL_BUF - 1) * POOL_WIDTH
    nbuf_ref[:, 0:keep] = buf_ref[:, POOL_WIDTH:]
    nbuf_ref[:, keep:] = u


def _pool_step(proj, buf, w_grp, scale, tb):
    m = proj.shape[0]
    return pl.pallas_call(
        _pool_step_kernel,
        grid=(m // tb,),
        in_specs=[
            pl.BlockSpec((tb, POOL_WIDTH), lambda i: (i, 0)),
            pl.BlockSpec((tb, POOL_BUF * POOL_WIDTH), lambda i: (i, 0)),
            pl.BlockSpec(w_grp.shape, lambda i: (0, 0, 0)),
            pl.BlockSpec((1, POOL_WIDTH), lambda i: (0, 0)),
        ],
        out_specs=[
            pl.BlockSpec((tb, POOL_WIDTH), lambda i: (i, 0)),
            pl.BlockSpec((tb, POOL_BUF * POOL_WIDTH), lambda i: (i, 0)),
        ],
        out_shape=[
            jax.ShapeDtypeStruct((m, POOL_WIDTH), BF16),
            jax.ShapeDtypeStruct((m, POOL_BUF * POOL_WIDTH), F32),
        ],
        compiler_params=_params("parallel"),
        name="pool_step",
    )(proj, buf, w_grp, scale)


def _zoh_lambda(a_re, a_im, log_step):
    dt = jnp.exp(log_step)
    mag = jnp.exp(a_re * dt)
    ang = a_im * dt
    return mag * jnp.cos(ang), mag * jnp.sin(ang)


def _ssm_disc_kernel(a_re_ref, a_im_ref, ls_ref, a_re_col_ref, a_im_col_ref, ls_col_ref, b_re_ref, b_im_ref,
                     bb_re_ref, bb_im_ref, pw_re_ref, pw_im_ref, ps_re_ref, ps_im_ref):
    a_re = a_re_col_ref[...]
    a_im = a_im_col_ref[...]
    lb_re, lb_im = _zoh_lambda(a_re, a_im, ls_col_ref[...])
    den = a_re * a_re + a_im * a_im
    n_re = lb_re - 1.0
    f_re = (n_re * a_re + lb_im * a_im) / den
    f_im = (lb_im * a_re - n_re * a_im) / den
    b_re = b_re_ref[...]
    b_im = b_im_ref[...]
    bb_re_ref[...] = f_re * b_re - f_im * b_im
    bb_im_ref[...] = f_re * b_im + f_im * b_re

    lb_re, lb_im = _zoh_lambda(a_re_ref[...], a_im_ref[...], ls_ref[...])

    def powers(base_re, base_im, out_re_ref, out_im_ref):
        p_re, p_im = base_re, base_im
        out_re_ref[0] = p_re
        out_im_ref[0] = p_im
        for r in range(1, SUBLANES):
            p_re, p_im = _cmul(p_re, p_im, base_re, base_im)
            out_re_ref[r] = p_re
            out_im_ref[r] = p_im
        return p_re, p_im

    s_re, s_im = powers(lb_re, lb_im, pw_re_ref, pw_im_ref)
    for _ in range(int(math.log2(SSM_SEG // SUBLANES))):
        s_re, s_im = _cmul(s_re, s_im, s_re, s_im)
    powers(s_re, s_im, ps_re_ref, ps_im_ref)


def _ssm_disc(a_re, a_im, log_step, b_re, b_im):
    g, n, h = b_re.shape
    col = (g * n, 1)
    pw = jax.ShapeDtypeStruct((SUBLANES, g, n), F32)
    bb = jax.ShapeDtypeStruct((g * n, h), F32)
    bb_re, bb_im, pw_re, pw_im, ps_re, ps_im = pl.pallas_call(
        _ssm_disc_kernel,
        out_shape=[bb, bb, pw, pw, pw, pw],
        name="ssm_disc",
    )(a_re, a_im, log_step[:, None],
      a_re.reshape(col), a_im.reshape(col), jnp.broadcast_to(log_step[:, None], (g, n)).reshape(col),
      b_re.reshape(g * n, h), b_im.reshape(g * n, h))
    return bb_re.reshape(g, n, h), bb_im.reshape(g, n, h), pw_re, pw_im, ps_re, ps_im


def _ssm_in(ub, bd_ref, x_ref):
    for c in range(SSM_CHUNKS):
        x_ref[:, c * SSM_CHUNK_LANES:(c + 1) * SSM_CHUNK_LANES] = _dot(
            ub[:, c * MXU_DIM:(c + 1) * MXU_DIM], bd_ref[c])


def _ssm_out(c, u, x_re_ref, x_im_ref, cdr_ref, cdi_ref, d_ref):
    ch = slice(c * MXU_DIM, (c + 1) * MXU_DIM)
    st = slice(c * SSM_CHUNK_LANES, (c + 1) * SSM_CHUNK_LANES)
    y = _dot(x_re_ref[:, st].astype(BF16), cdr_ref[c]) - _dot(x_im_ref[:, st].astype(BF16), cdi_ref[c])
    return jax.nn.gelu(y + d_ref[:, ch] * u[:, ch])


def _ssm_seq_kernel(u_ref, h_re_ref, h_im_ref, tab_ref, bdr_ref, bdi_ref, cdr_ref, cdi_ref, d_ref,
                    g_ref, last_re_ref, last_im_ref, xr_ref, xi_ref, cr_ref, ci_ref):
    t = pl.program_id(1)
    tt = u_ref.shape[0]
    seg = tt // SUBLANES

    @pl.when(t == 0)
    def _():
        cr_ref[...] = jnp.broadcast_to(h_re_ref[...], cr_ref.shape)
        ci_ref[...] = jnp.broadcast_to(h_im_ref[...], ci_ref.shape)

    def regroup(x, rows_major, rows_minor):
        w = x.shape[1]
        return jnp.swapaxes(x.reshape(rows_major, rows_minor, w), 0, 1).reshape(tt, w)

    u = regroup(u_ref[...], SUBLANES, seg)
    ub = u.astype(BF16)
    _ssm_in(ub, bdr_ref, xr_ref)
    _ssm_in(ub, bdi_ref, xi_ref)

    first_row = lax.broadcasted_iota(jnp.int32, (SUBLANES, SCAN_LANES), 0) == 0
    for lc in range(SSM_LANES // SCAN_LANES):
        sl = slice(lc * SCAN_LANES, (lc + 1) * SCAN_LANES)
        l_re = tab_ref[0, :, sl]
        l_im = tab_ref[1, :, sl]

        def rows(j):
            return pl.ds(pl.multiple_of(j * SUBLANES, SUBLANES), SUBLANES)

        def advance(j, x, sl=sl, l_re=l_re, l_im=l_im):
            x_re, x_im = _cmul(l_re, l_im, *x)
            x_re = x_re + xr_ref[rows(j), sl]
            x_im = x_im + xi_ref[rows(j), sl]
            xr_ref[rows(j), sl] = x_re
            xi_ref[rows(j), sl] = x_im
            return x_re, x_im

        zero = jnp.zeros((SUBLANES, SCAN_LANES), F32)
        f_re, f_im = lax.fori_loop(0, seg, advance, (zero, zero), unroll=4)

        c_re = cr_ref[:, sl]
        c_im = ci_ref[:, sl]
        for i in range(3):
            s_re = pltpu.roll(f_re, 1 << i, axis=0)
            s_im = pltpu.roll(f_im, 1 << i, axis=0)
            m_re, m_im = _cmul(tab_ref[2 + 2 * i, :, sl], tab_ref[3 + 2 * i, :, sl], s_re, s_im)
            f_re, f_im = f_re + m_re, f_im + m_im
        m_re, m_im = _cmul(tab_ref[8, :, sl], tab_ref[9, :, sl], c_re, c_im)
        f_re, f_im = f_re + m_re, f_im + m_im
        e_re = jnp.where(first_row, c_re, pltpu.roll(f_re, 1, axis=0))
        e_im = jnp.where(first_row, c_im, pltpu.roll(f_im, 1, axis=0))
        cr_ref[:, sl] = jnp.broadcast_to(f_re[SUBLANES - 1:SUBLANES, :], (SUBLANES, SCAN_LANES))
        ci_ref[:, sl] = jnp.broadcast_to(f_im[SUBLANES - 1:SUBLANES, :], (SUBLANES, SCAN_LANES))

        def fix(j, w, sl=sl, l_re=l_re, l_im=l_im):
            w_re, w_im = _cmul(l_re, l_im, *w)
            xr_ref[rows(j), sl] += w_re
            xi_ref[rows(j), sl] += w_im
            return w_re, w_im

        lax.fori_loop(0, seg, fix, (e_re, e_im), unroll=4)

    for c in range(SSM_CHUNKS):
        ch = slice(c * MXU_DIM, (c + 1) * MXU_DIM)
        g = _ssm_out(c, u, xr_ref, xi_ref, cdr_ref, cdi_ref, d_ref)
        g_ref[:, ch] = regroup(g, seg, SUBLANES).astype(BF16)

    @pl.when(t == pl.num_programs(1) - 1)
    def _():
        last_re_ref[...] = cr_ref[0:1, :]
        last_im_ref[...] = ci_ref[0:1, :]


def _const_spec(shape):
    return pl.BlockSpec(shape, lambda *_: (0,) * len(shape))


def _ssm_seq(proj, h_re, h_im, tab, bdr, bdi, cdr, cdi, d_skip, batch, seq):
    tt = SSM_TIME_TILE
    nt = seq // tt
    col_block = POOL_WIDTH // SSM_WIDTH
    state_spec = pl.BlockSpec((None, 1, SSM_LANES), lambda b, t: (b, 0, 0))
    state_shape = jax.ShapeDtypeStruct((batch, 1, SSM_LANES), F32)
    return pl.pallas_call(
        _ssm_seq_kernel,
        grid=(batch, nt),
        in_specs=[
            pl.BlockSpec((tt, SSM_WIDTH), lambda b, t: (b * nt + t, col_block)),
            state_spec, state_spec,
            _const_spec(tab.shape),
            _const_spec(bdr.shape), _const_spec(bdi.shape),
            _const_spec(cdr.shape), _const_spec(cdi.shape),
            _const_spec(d_skip.shape),
        ],
        out_specs=[pl.BlockSpec((tt, SSM_WIDTH), lambda b, t: (b * nt + t, 0)), state_spec, state_spec],
        out_shape=[jax.ShapeDtypeStruct((batch * seq, SSM_WIDTH), BF16), state_shape, state_shape],
        scratch_shapes=[
            pltpu.VMEM((tt, SSM_LANES), F32), pltpu.VMEM((tt, SSM_LANES), F32),
            pltpu.VMEM((SUBLANES, SSM_LANES), F32), pltpu.VMEM((SUBLANES, SSM_LANES), F32),
        ],
        compiler_params=_params("parallel", "arbitrary"),
        name="ssm_seq",
    )(proj, h_re, h_im, tab, bdr, bdi, cdr, cdi, d_skip)


def _ssm_step_kernel(u_ref, h_re_ref, h_im_ref, lb_ref, bdr_ref, bdi_ref, cdr_ref, cdi_ref, d_ref,
                     g_ref, x_re_ref, x_im_ref):
    u = u_ref[...]
    ub = u.astype(BF16)
    _ssm_in(ub, bdr_ref, x_re_ref)
    _ssm_in(ub, bdi_ref, x_im_ref)
    m_re, m_im = _cmul(lb_ref[0:1, :], lb_ref[1:2, :], h_re_ref[...], h_im_ref[...])
    x_re_ref[...] += m_re
    x_im_ref[...] += m_im
    for c in range(SSM_CHUNKS):
        ch = slice(c * MXU_DIM, (c + 1) * MXU_DIM)
        g_ref[:, ch] = _ssm_out(c, u, x_re_ref, x_im_ref, cdr_ref, cdi_ref, d_ref).astype(BF16)


def _ssm_step(proj, h_re, h_im, lb, bdr, bdi, cdr, cdi, d_skip):
    m = proj.shape[0]
    state_spec = pl.BlockSpec((m, SSM_LANES), lambda i: (0, 0))
    state_shape = jax.ShapeDtypeStruct((m, SSM_LANES), F32)
    return pl.pallas_call(
        _ssm_step_kernel,
        grid=(1,),
        in_specs=[
            pl.BlockSpec((m, SSM_WIDTH), lambda i: (0, POOL_WIDTH // SSM_WIDTH)),
            state_spec, state_spec,
            _const_spec(lb.shape),
            _const_spec(bdr.shape), _const_spec(bdi.shape),
            _const_spec(cdr.shape), _const_spec(cdi.shape),
            _const_spec(d_skip.shape),
        ],
        out_specs=[pl.BlockSpec((m, SSM_WIDTH), lambda i: (0, 0)), state_spec, state_spec],
        out_shape=[jax.ShapeDtypeStruct((m, SSM_WIDTH), BF16), state_shape, state_shape],
        compiler_params=_params("arbitrary"),
        name="ssm_step",
    )(proj, h_re, h_im, lb, bdr, bdi, cdr, cdi, d_skip)


def _xa_seq_kernel(q_ref, k_ref, v_ref, o_ref):
    for h in range(XA_HEADS):
        ch = slice(h * XA_HEAD_DIM, (h + 1) * XA_HEAD_DIM)
        q = q_ref[:, ch].astype(BF16)
        k = k_ref[:, ch].astype(BF16)
        v = v_ref[:, ch].astype(BF16)
        s = lax.dot_general(q, k, (((1,), (1,)), ((), ())), preferred_element_type=F32) * XA_SCALE
        e = jnp.exp(s - jnp.max(s, axis=-1, keepdims=True))
        p = e / jnp.sum(e, axis=-1, keepdims=True)
        o_ref[:, ch] = _dot(p.astype(BF16), v).astype(BF16)


def _xa_seq(proj, mem_k, mem_v, batch, seq, tq):
    nt = seq // tq
    q_block = (POOL_WIDTH + SSM_WIDTH) // XA_WIDTH
    kv_spec = pl.BlockSpec((N_MEM, XA_WIDTH), lambda b, t: (b, 0))
    return pl.pallas_call(
        _xa_seq_kernel,
        grid=(batch, nt),
        in_specs=[pl.BlockSpec((tq, XA_WIDTH), lambda b, t: (b * nt + t, q_block)), kv_spec, kv_spec],
        out_specs=pl.BlockSpec((tq, XA_WIDTH), lambda b, t: (b * nt + t, 0)),
        out_shape=jax.ShapeDtypeStruct((batch * seq, XA_WIDTH), BF16),
        compiler_params=_params("parallel", "parallel"),
        name="xa_seq",
    )(proj, mem_k, mem_v)


def _xa_step_kernel(q_ref, k_ref, v_ref, o_ref):
    q = q_ref[...]
    s = jnp.sum(k_ref[...] * q, axis=-1, keepdims=True) * XA_SCALE
    e = jnp.exp(s - jnp.max(s, axis=1, keepdims=True))
    p = e / jnp.sum(e, axis=1, keepdims=True)
    o_ref[...] = jnp.sum(p * v_ref[...], axis=1, keepdims=True)


def _xa_step(q4, cache_k, cache_v, tb):
    m = q4.shape[0]
    kv_spec = pl.BlockSpec((tb, N_MEM, XA_HEADS, XA_HEAD_DIM), lambda i: (i, 0, 0, 0))
    q_spec = pl.BlockSpec((tb, 1, XA_HEADS, XA_HEAD_DIM), lambda i: (i, 0, 0, 0))
    return pl.pallas_call(
        _xa_step_kernel,
        grid=(m // tb,),
        in_specs=[q_spec, kv_spec, kv_spec],
        out_specs=q_spec,
        out_shape=jax.ShapeDtypeStruct(q4.shape, F32),
        compiler_params=_params("parallel"),
        name="xa_step",
    )(q4, cache_k, cache_v)


def _merge_kernel(z_ref, g_ref, o_ref, ga_ref, gb_ref, gc_ref, wpo_ref, wv_ref, wg_ref, wxo_ref, wout_ref,
                  h_ref, gpost_ref, out_ref, acc_ref):
    n = pl.program_id(1)

    @pl.when(n == 0)
    def _():
        acc_ref[...] = jnp.zeros_like(acc_ref)

    g = g_ref[...]
    o_pool = _dot(z_ref[...], wpo_ref[...])
    o_ssm = _dot(g, wv_ref[...]) * jax.nn.sigmoid(_dot(g, wg_ref[...]))
    o_xa = _dot(o_ref[...], wxo_ref[...])
    merged = (jax.nn.sigmoid(ga_ref[...]) * o_pool + jax.nn.sigmoid(gb_ref[...]) * o_ssm
              + jax.nn.sigmoid(gc_ref[...]) * o_xa)
    acc_ref[...] += _dot(merged.astype(BF16), wout_ref[...])

    @pl.when(n == pl.num_programs(1) - 1)
    def _():
        out_ref[...] = h_ref[...] + _rms(acc_ref[...], gpost_ref[...])


def _merge(z, g, o, proj, w_pool_out, w_glu_val, w_glu_gate, w_xa_out, w_out, h, g_post, tm, tn):
    m, d = h.shape
    half = z.shape[1]
    gate0 = OFF_GATE // tn
    per_gate = d // tn
    branch_spec = pl.BlockSpec((tm, half), lambda i, n: (i, 0))
    w_spec = pl.BlockSpec((half, tn), lambda i, n: (0, n))

    def gate_spec(k):
        return pl.BlockSpec((tm, tn), lambda i, n: (i, gate0 + k * per_gate + n))

    return pl.pallas_call(
        _merge_kernel,
        grid=(m // tm, d // tn),
        in_specs=[
            branch_spec, branch_spec, branch_spec,
            gate_spec(0), gate_spec(1), gate_spec(2),
            w_spec, w_spec, w_spec, w_spec,
            pl.BlockSpec((tn, d), lambda i, n: (n, 0)),
            pl.BlockSpec((tm, d), lambda i, n: (i, 0)),
            pl.BlockSpec((1, d), lambda i, n: (0, 0)),
        ],
        out_specs=pl.BlockSpec((tm, d), lambda i, n: (i, 0)),
        out_shape=jax.ShapeDtypeStruct((m, d), F32),
        scratch_shapes=[pltpu.VMEM((tm, d), F32)],
        compiler_params=_params("parallel", "arbitrary"),
        name="merge",
    )(z, g, o, proj, proj, proj, w_pool_out, w_glu_val, w_glu_gate, w_xa_out, w_out, h, g_post)


def _block_diag_in(bb):
    gpc = SSM_GROUPS // SSM_CHUNKS
    x = bb.reshape(SSM_CHUNKS, gpc, SSM_STATE, SSM_GROUP_CH).transpose(0, 1, 3, 2)
    eye = jnp.eye(gpc, dtype=bb.dtype)
    y = x[:, :, :, None, :] * eye[None, :, None, :, None]
    return y.reshape(SSM_CHUNKS, gpc * SSM_GROUP_CH, gpc * SSM_STATE)


def _block_diag_out(c):
    gpc = SSM_GROUPS // SSM_CHUNKS
    x = c.reshape(SSM_CHUNKS, gpc, SSM_GROUP_CH, SSM_STATE).transpose(0, 1, 3, 2)
    eye = jnp.eye(gpc, dtype=c.dtype)
    y = x[:, :, :, None, :] * eye[None, :, None, :, None]
    return y.reshape(SSM_CHUNKS, gpc * SSM_STATE, gpc * SSM_GROUP_CH)


def _scan_table(pw_re, pw_im, ps_re, ps_im):
    flat = lambda p: p.reshape(SUBLANES, SSM_LANES)
    pw_re, pw_im, ps_re, ps_im = flat(pw_re), flat(pw_im), flat(ps_re), flat(ps_im)
    row = jnp.arange(SUBLANES)[:, None]
    tiles = [jnp.broadcast_to(pw_re[0], (SUBLANES, SSM_LANES)), jnp.broadcast_to(pw_im[0], (SUBLANES, SSM_LANES))]
    for d in (1, 2, 4):
        keep = row >= d
        tiles.append(jnp.where(keep, ps_re[d - 1][None, :], 0.0))
        tiles.append(jnp.where(keep, ps_im[d - 1][None, :], 0.0))
    tiles += [ps_re, ps_im]
    return jnp.stack(tiles)


def _row_tile(m, target):
    return target if m % target == 0 else m


def _layer(x, w, mixers):
    m = x.shape[0]
    tm_ffn = _row_tile(m, 1024)
    h, xn = _ffn(x, w['g_ff1_pre'], w['w_ff1_gate'], w['w_ff1_up'], w['w_ff1_down'], w['g_ff1_post'],
                 _row_tile(m, 512), 512, g_next=w['g_mix_pre'])
    proj = _proj(xn, w['w_in'], _row_tile(m, 2048), 512)
    z, g, o, states = mixers(proj)
    h = _merge(z, g, o, proj, w['w_pool_out'], w['w_glu_val'], w['w_glu_gate'], w['w_xa_out'], w['w_out'],
               h, w['g_mix_post'], _row_tile(m, 512), 512)
    h = _ffn(h, w['g_ff2_pre'], w['w_ff2_gate'], w['w_ff2_up'], w['w_ff2_down'], w['g_ff2_post'], tm_ffn, 256)
    return h, states


def kernel(x_prompt, x_sample, mem_prompt, cache_mem_k, cache_mem_v, state_pool, state_ssm_re, state_ssm_im, g_ff1_pre, w_ff1_gate, w_ff1_up, w_ff1_down, g_ff1_post, g_mix_pre, w_in, w_pool_grp, pool_scale, w_pool_out, ssm_a_re, ssm_a_im, ssm_log_step, ssm_b_re, ssm_b_im, ssm_c_re, ssm_c_im, ssm_d, w_glu_val, w_glu_gate, g_mem, w_mem_k, w_mem_v, w_xa_out, w_out, g_mix_post, g_ff2_pre, w_ff2_gate, w_ff2_up, w_ff2_down, g_ff2_post):
    bp, lp, d = x_prompt.shape
    bs, ls, _ = x_sample.shape
    depth = w_in.shape[0]
    assert depth == 1 and ls == 1 and lp >= POOL_BUF and lp % SSM_TIME_TILE == 0

    mats = dict(w_ff1_gate=w_ff1_gate, w_ff1_up=w_ff1_up, w_ff1_down=w_ff1_down, w_in=w_in,
                w_pool_out=w_pool_out, w_glu_val=w_glu_val, w_glu_gate=w_glu_gate, w_xa_out=w_xa_out,
                w_out=w_out, w_ff2_gate=w_ff2_gate, w_ff2_up=w_ff2_up, w_ff2_down=w_ff2_down,
                w_mem_k=w_mem_k, w_mem_v=w_mem_v)
    gains = dict(g_ff1_pre=g_ff1_pre, g_ff1_post=g_ff1_post, g_mix_pre=g_mix_pre, g_mix_post=g_mix_post,
                 g_ff2_pre=g_ff2_pre, g_ff2_post=g_ff2_post, g_mem=g_mem)
    w = {k: v[0].astype(BF16) for k, v in mats.items()}
    w.update({k: v for k, v in gains.items()})
    w_grp = w_pool_grp[0].astype(BF16)
    scale = pool_scale
    d_skip = ssm_d

    bb_re, bb_im, pw_re, pw_im, ps_re, ps_im = _ssm_disc(
        ssm_a_re[0], ssm_a_im[0], ssm_log_step[0], ssm_b_re[0], ssm_b_im[0])
    bdr = _block_diag_in(bb_re.astype(BF16))
    bdi = _block_diag_in(bb_im.astype(BF16))
    cdr = _block_diag_out(ssm_c_re[0].astype(BF16))
    cdi = _block_diag_out(ssm_c_im[0].astype(BF16))
    tab = _scan_table(pw_re, pw_im, ps_re, ps_im)
    lb = jnp.stack([pw_re[0].reshape(SSM_LANES), pw_im[0].reshape(SSM_LANES)])

    mem = mem_prompt.reshape(bp * N_MEM, d)
    mem_k = _normproj(mem, w['g_mem'], w['w_mem_k'], 512, 1024)
    mem_v = _normproj(mem, w['g_mem'], w['w_mem_v'], 512, 1024)

    def prompt_mixers(proj):
        z = _pool_seq(proj, w_grp, scale, bp, lp, 512)
        zeros = jnp.zeros((bp, 1, SSM_LANES), F32)
        g, last_re, last_im = _ssm_seq(proj, zeros, zeros, tab, bdr, bdi, cdr, cdi, d_skip, bp, lp)
        o = _xa_seq(proj, mem_k, mem_v, bp, lp, 512)
        pool_rows = proj.reshape(bp, lp, IN_WIDTH)[:, lp - POOL_BUF:, :POOL_WIDTH]
        return z, g, o, (pool_rows, last_re, last_im)

    y_prompt, (pool_p, re_p, im_p) = _layer(x_prompt.reshape(bp * lp, d), w, prompt_mixers)

    def sample_mixers(proj):
        z, new_buf = _pool_step(proj, state_pool[0].reshape(bs, POOL_BUF * POOL_WIDTH), w_grp, scale, 32)
        g, x_re, x_im = _ssm_step(proj, state_ssm_re[0].reshape(bs, SSM_LANES),
                                  state_ssm_im[0].reshape(bs, SSM_LANES), lb, bdr, bdi, cdr, cdi, d_skip)
        q4 = proj[:, POOL_WIDTH + SSM_WIDTH:OFF_GATE].reshape(bs, 1, XA_HEADS, XA_HEAD_DIM)
        o = _xa_step(q4, cache_mem_k[0], cache_mem_v[0], 4)
        return z, g, o.reshape(bs, XA_WIDTH).astype(BF16), (new_buf, x_re, x_im)

    y_sample, (pool_s, re_s, im_s) = _layer(x_sample.reshape(bs * ls, d), w, sample_mixers)

    kv_shape = (1, bp, N_MEM, XA_HEADS, XA_HEAD_DIM)
    st_shape = (SSM_GROUPS, SSM_STATE)
    return (y_prompt.reshape(bp, lp, d), y_sample.reshape(bs, ls, d),
            mem_k.reshape(kv_shape), mem_v.reshape(kv_shape),
            pool_p[None],
            re_p.reshape((1, bp) + st_shape), im_p.reshape((1, bp) + st_shape),
            pool_s.reshape(1, bs, POOL_BUF, POOL_WIDTH),
            re_s.reshape((1, bs) + st_shape), im_s.reshape((1, bs) + st_shape))
```

```python
import functools
import math

import jax
import jax.numpy as jnp
from jax import lax
from jax.experimental import pallas as pl
from jax.experimental.pallas import tpu as pltpu

F32 = jnp.float32
BF16 = jnp.bfloat16

D_MODEL = 2048
PAST_LEN = 16384
POOL_WIDTH = D_MODEL // 2
POOL_WINDOWS = (2, 4, 8, 16)
POOL_GROUP_DIM = POOL_WIDTH // len(POOL_WINDOWS)
POOL_BUF = max(POOL_WINDOWS) - 1
SSM_WIDTH = D_MODEL // 2
SSM_GROUP_CH = 16
SSM_GROUPS = SSM_WIDTH // SSM_GROUP_CH
SSM_STATE = 64
SSM_LANES = SSM_GROUPS * SSM_STATE
XA_HEADS = 4
XA_HEAD_DIM = D_MODEL // 8
XA_WIDTH = XA_HEADS * XA_HEAD_DIM
XA_SCALE = XA_HEAD_DIM ** -0.5
N_MEM = 256
OFF_GATE = POOL_WIDTH + SSM_WIDTH + XA_WIDTH
IN_WIDTH = OFF_GATE + 3 * D_MODEL
RMS_EPS = 1e-6

SUBLANES = 8
LANES = 128
MXU_DIM = 256
VMEM_LIMIT_BYTES = 60 * 1024 * 1024
DOWN_CHUNKS = 4

SSM_CHUNKS = SSM_WIDTH // MXU_DIM
SSM_CHUNK_LANES = SSM_LANES // SSM_CHUNKS
SSM_TIME_TILE = 256
SSM_SEG = SSM_TIME_TILE // SUBLANES
SCAN_LANES = 4 * LANES


def _params(*sem):
    return pltpu.CompilerParams(dimension_semantics=sem, vmem_limit_bytes=VMEM_LIMIT_BYTES)


def _rms(x, g):
    return (x * lax.rsqrt(jnp.mean(x * x, axis=-1, keepdims=True) + RMS_EPS)) * g


def _dot(a, b):
    return jnp.dot(a, b, preferred_element_type=F32)


def _cmul(a_re, a_im, b_re, b_im):
    return a_re * b_re - a_im * b_im, a_re * b_im + a_im * b_re


def _weight(w_ref, copy_ref=None):
    w = w_ref[...]
    if w.dtype != BF16:
        w = w.astype(BF16)
    if copy_ref is not None:
        copy_ref[...] = w
    return w


def _with_copies(specs, weights, emit_w):
    if not emit_w:
        return [], []
    return list(specs), [jax.ShapeDtypeStruct(w.shape, BF16) for w in weights]


def _ffn_kernel(*refs, first, emit_w):
    refs = list(refs)
    x_ref, pre_ref, wg_ref, wu_ref, wd_ref, gpost_ref = refs[:6]
    del refs[:6]
    gnext_ref = refs.pop(0) if first else None
    o_ref = refs.pop(0)
    xnext_ref = refs.pop(0) if first else None
    wgb_ref, wub_ref, wdb_ref = (refs.pop(0), refs.pop(0), refs.pop(0)) if emit_w else (None, None, None)
    xn_ref = refs.pop(0) if first else pre_ref
    j = pl.program_id(1)

    @pl.when(j == 0)
    def _():
        if first:
            xn_ref[...] = _rms(x_ref[...], pre_ref[...]).astype(BF16)
        o_ref[...] = jnp.zeros_like(o_ref)

    xn = xn_ref[...]
    g = _dot(xn, _weight(wg_ref, wgb_ref))
    u = _dot(xn, _weight(wu_ref, wub_ref))
    hid = ((g * jax.nn.sigmoid(g)) * u).astype(BF16)
    wd = _weight(wd_ref, wdb_ref)
    cols = wd.shape[1] // DOWN_CHUNKS
    for n in range(DOWN_CHUNKS):
        sl = slice(n * cols, (n + 1) * cols)
        o_ref[:, sl] += _dot(hid, wd[:, sl])

    @pl.when(j == pl.num_programs(1) - 1)
    def _():
        out = x_ref[...] + 0.5 * _rms(o_ref[...], gpost_ref[...])
        o_ref[...] = out
        if first:
            xnext_ref[...] = _rms(out, gnext_ref[...]).astype(BF16)


def _ffn(x, pre, weights, g_post, tm, tf, g_next=None):
    m, d = x.shape
    w_gate, w_up, w_down = weights
    f = w_gate.shape[1]
    first = g_next is not None
    emit_w = w_gate.dtype != BF16
    assert not emit_w or m == tm, "weight copies need every weight block visited exactly once"
    row_spec = pl.BlockSpec((tm, d), lambda i, j: (i, 0))
    gain_spec = pl.BlockSpec((1, d), lambda i, j: (0, 0))
    w_specs = [pl.BlockSpec((d, tf), lambda i, j: (0, j)), pl.BlockSpec((d, tf), lambda i, j: (0, j)),
               pl.BlockSpec((tf, d), lambda i, j: (j, 0))]
    copy_specs, copy_shapes = _with_copies(w_specs, weights, emit_w)
    outs = pl.pallas_call(
        functools.partial(_ffn_kernel, first=first, emit_w=emit_w),
        grid=(m // tm, f // tf),
        in_specs=[row_spec, gain_spec if first else row_spec] + w_specs + [gain_spec]
        + ([gain_spec] if first else []),
        out_specs=[row_spec] + ([row_spec] if first else []) + copy_specs,
        out_shape=[jax.ShapeDtypeStruct((m, d), F32)] + ([jax.ShapeDtypeStruct((m, d), BF16)] if first else [])
        + copy_shapes,
        scratch_shapes=[pltpu.VMEM((tm, d), BF16)] if first else [],
        compiler_params=_params("parallel", "arbitrary"),
        name="ffn",
    )(*((x, pre, w_gate, w_up, w_down, g_post) + ((g_next,) if first else ())))
    n_out = 2 if first else 1
    return outs[:n_out], (tuple(outs[n_out:]) if emit_w else tuple(weights))


def _proj_kernel(xn_ref, w_ref, o_ref, *copy_ref):
    o_ref[...] = _dot(xn_ref[...], _weight(w_ref, *copy_ref))


def _proj(xn, w, tm, tn):
    m, d = xn.shape
    n = w.shape[1]
    emit_w = w.dtype != BF16
    assert not emit_w or m == tm
    w_spec = pl.BlockSpec((d, tn), lambda i, j: (0, j))
    copy_specs, copy_shapes = _with_copies([w_spec], [w], emit_w)
    outs = pl.pallas_call(
        _proj_kernel,
        grid=(m // tm, n // tn),
        in_specs=[pl.BlockSpec((tm, d), lambda i, j: (i, 0)), w_spec],
        out_specs=[pl.BlockSpec((tm, tn), lambda i, j: (i, j))] + copy_specs,
        out_shape=[jax.ShapeDtypeStruct((m, n), F32)] + copy_shapes,
        compiler_params=_params("parallel", "arbitrary"),
        name="proj",
    )(xn, w)
    return outs[0], (outs[1] if emit_w else w)


def _normproj_kernel(x_ref, g_ref, w_ref, o_ref, xn_ref):
    @pl.when(pl.program_id(1) == 0)
    def _():
        xn_ref[...] = _rms(x_ref[...], g_ref[...]).astype(BF16)

    o_ref[...] = _dot(xn_ref[...], _weight(w_ref))


def _normproj(x, g, w, tm, tn):
    m, d = x.shape
    n = w.shape[1]
    return pl.pallas_call(
        _normproj_kernel,
        grid=(m // tm, n // tn),
        in_specs=[
            pl.BlockSpec((tm, d), lambda i, j: (i, 0)),
            pl.BlockSpec((1, d), lambda i, j: (0, 0)),
            pl.BlockSpec((d, tn), lambda i, j: (0, j)),
        ],
        out_specs=pl.BlockSpec((tm, tn), lambda i, j: (i, j)),
        out_shape=jax.ShapeDtypeStruct((m, n), F32),
        scratch_shapes=[pltpu.VMEM((tm, d), BF16)],
        compiler_params=_params("parallel", "arbitrary"),
        name="normproj",
    )(x, g, w)


def _pool_group_out(k, diff, wgrp_ref, scale_ref, z_ref):
    ch = slice(k * POOL_GROUP_DIM, (k + 1) * POOL_GROUP_DIM)
    zk = _dot(diff.astype(BF16), wgrp_ref[k]) * scale_ref[:, ch]
    z_ref[:, ch] = zk.astype(BF16)


def _pool_seq_kernel(u_ref, wgrp_ref, scale_ref, z_ref, ext_ref, *, halo):
    t = pl.program_id(1)
    tt = u_ref.shape[0]

    @pl.when(t == 0)
    def _():
        ext_ref[0:halo, :] = jnp.zeros((halo, POOL_WIDTH), F32)

    u = u_ref[...]
    ext_ref[halo:halo + tt, :] = u
    pos = t * tt + lax.broadcasted_iota(jnp.int32, (tt, 1), 0)
    for k, w in enumerate(POOL_WINDOWS):
        ch = slice(k * POOL_GROUP_DIM, (k + 1) * POOL_GROUP_DIM)
        a = ext_ref[:, ch]
        d = 1
        while d < w:
            a = a + pltpu.roll(a, d, axis=0)
            d *= 2
        count = jnp.minimum(pos + 1, w).astype(F32)
        diff = a[halo:, :] / count - u[:, ch]
        _pool_group_out(k, diff, wgrp_ref, scale_ref, z_ref)
    ext_ref[0:halo, :] = ext_ref[tt:tt + halo, :]


def _pool_seq(proj, w_grp, scale, batch, seq, tt):
    halo = 2 * SUBLANES
    nt = seq // tt
    return pl.pallas_call(
        functools.partial(_pool_seq_kernel, halo=halo),
        grid=(batch, nt),
        in_specs=[
            pl.BlockSpec((tt, POOL_WIDTH), lambda b, t: (b * nt + t, 0)),
            pl.BlockSpec(w_grp.shape, lambda b, t: (0, 0, 0)),
            pl.BlockSpec((1, POOL_WIDTH), lambda b, t: (0, 0)),
        ],
        out_specs=pl.BlockSpec((tt, POOL_WIDTH), lambda b, t: (b * nt + t, 0)),
        out_shape=jax.ShapeDtypeStruct((batch * seq, POOL_WIDTH), BF16),
        scratch_shapes=[pltpu.VMEM((tt + halo, POOL_WIDTH), F32)],
        compiler_params=_params("parallel", "arbitrary"),
        name="pool_seq",
    )(proj, w_grp, scale)


def _pool_step_kernel(u_ref, buf_ref, wgrp_ref, scale_ref, z_ref, nbuf_ref):
    u = u_ref[...]
    for k, w in enumerate(POOL_WINDOWS):
        ch = slice(k * POOL_GROUP_DIM, (k + 1) * POOL_GROUP_DIM)
        s = u[:, ch]
        for j in range(1, w):
            row = POOL_BUF - j
            s = s + buf_ref[:, row * POOL_WIDTH + k * POOL_GROUP_DIM:
                            row * POOL_WIDTH + (k + 1) * POOL_GROUP_DIM]
        diff = s / float(min(PAST_LEN + 1, w)) - u[:, ch]
        _pool_group_out(k, diff, wgrp_ref, scale_ref, z_ref)
    keep = (POOL_BUF - 1) * POOL_WIDTH
    nbuf_ref[:, 0:keep] = buf_ref[:, POOL_WIDTH:]
    nbuf_ref[:, keep:] = u


def _pool_step(proj, buf, w_grp, scale, tb):
    m = proj.shape[0]
    return pl.pallas_call(
        _pool_step_kernel,
        grid=(m // tb,),
        in_specs=[
            pl.BlockSpec((tb, POOL_WIDTH), lambda i: (i, 0)),
            pl.BlockSpec((tb, POOL_BUF * POOL_WIDTH), lambda i: (i, 0)),
            pl.BlockSpec(w_grp.shape, lambda i: (0, 0, 0)),
            pl.BlockSpec((1, POOL_WIDTH), lambda i: (0, 0)),
        ],
        out_specs=[
            pl.BlockSpec((tb, POOL_WIDTH), lambda i: (i, 0)),
            pl.BlockSpec((tb, POOL_BUF * POOL_WIDTH), lambda i: (i, 0)),
        ],
        out_shape=[
            jax.ShapeDtypeStruct((m, POOL_WIDTH), BF16),
            jax.ShapeDtypeStruct((m, POOL_BUF * POOL_WIDTH), F32),
        ],
        compiler_params=_params("parallel"),
        name="pool_step",
    )(proj, buf, w_grp, scale)


def _zoh_lambda(a_re, a_im, log_step):
    dt = jnp.exp(log_step)
    mag = jnp.exp(a_re * dt)
    ang = a_im * dt
    return mag * jnp.cos(ang), mag * jnp.sin(ang)


def _ssm_disc_kernel(a_re_ref, a_im_ref, ls_ref, a_re_col_ref, a_im_col_ref, ls_col_ref, b_re_ref, b_im_ref,
                     bb_re_ref, bb_im_ref, pw_re_ref, pw_im_ref, ps_re_ref, ps_im_ref):
    a_re = a_re_col_ref[...]
    a_im = a_im_col_ref[...]
    lb_re, lb_im = _zoh_lambda(a_re, a_im, ls_col_ref[...])
    den = a_re * a_re + a_im * a_im
    n_re = lb_re - 1.0
    f_re = (n_re * a_re + lb_im * a_im) / den
    f_im = (lb_im * a_re - n_re * a_im) / den
    b_re = b_re_ref[...]
    b_im = b_im_ref[...]
    bb_re_ref[...] = f_re * b_re - f_im * b_im
    bb_im_ref[...] = f_re * b_im + f_im * b_re

    lb_re, lb_im = _zoh_lambda(a_re_ref[...], a_im_ref[...], ls_ref[...])

    def powers(base_re, base_im, out_re_ref, out_im_ref):
        p_re, p_im = base_re, base_im
        out_re_ref[0] = p_re
        out_im_ref[0] = p_im
        for r in range(1, SUBLANES):
            p_re, p_im = _cmul(p_re, p_im, base_re, base_im)
            out_re_ref[r] = p_re
            out_im_ref[r] = p_im
        return p_re, p_im

    s_re, s_im = powers(lb_re, lb_im, pw_re_ref, pw_im_ref)
    for _ in range(int(math.log2(SSM_SEG // SUBLANES))):
        s_re, s_im = _cmul(s_re, s_im, s_re, s_im)
    powers(s_re, s_im, ps_re_ref, ps_im_ref)


def _ssm_disc(a_re, a_im, log_step, b_re, b_im):
    g, n, h = b_re.shape
    col = (g * n, 1)
    pw = jax.ShapeDtypeStruct((SUBLANES, g, n), F32)
    bb = jax.ShapeDtypeStruct((g * n, h), F32)
    bb_re, bb_im, pw_re, pw_im, ps_re, ps_im = pl.pallas_call(
        _ssm_disc_kernel,
        out_shape=[bb, bb, pw, pw, pw, pw],
        name="ssm_disc",
    )(a_re, a_im, log_step[:, None],
      a_re.reshape(col), a_im.reshape(col), jnp.broadcast_to(log_step[:, None], (g, n)).reshape(col),
      b_re.reshape(g * n, h), b_im.reshape(g * n, h))
    return bb_re.reshape(g, n, h), bb_im.reshape(g, n, h), pw_re, pw_im, ps_re, ps_im


def _ssm_in(ub, bd_ref, x_ref):
    for c in range(SSM_CHUNKS):
        x_ref[:, c * SSM_CHUNK_LANES:(c + 1) * SSM_CHUNK_LANES] = _dot(
            ub[:, c * MXU_DIM:(c + 1) * MXU_DIM], bd_ref[c])


def _ssm_out(c, u, x_re_ref, x_im_ref, cdr_ref, cdi_ref, d_ref):
    ch = slice(c * MXU_DIM, (c + 1) * MXU_DIM)
    st = slice(c * SSM_CHUNK_LANES, (c + 1) * SSM_CHUNK_LANES)
    y = _dot(x_re_ref[:, st].astype(BF16), cdr_ref[c]) - _dot(x_im_ref[:, st].astype(BF16), cdi_ref[c])
    return jax.nn.gelu(y + d_ref[:, ch] * u[:, ch])


def _ssm_seq_kernel(u_ref, h_re_ref, h_im_ref, tab_ref, bdr_ref, bdi_ref, cdr_ref, cdi_ref, d_ref,
                    g_ref, last_re_ref, last_im_ref, xr_ref, xi_ref, cr_ref, ci_ref):
    t = pl.program_id(1)
    tt = u_ref.shape[0]
    seg = tt // SUBLANES

    @pl.when(t == 0)
    def _():
        cr_ref[...] = jnp.broadcast_to(h_re_ref[...], cr_ref.shape)
        ci_ref[...] = jnp.broadcast_to(h_im_ref[...], ci_ref.shape)

    def regroup(x, rows_major, rows_minor):
        w = x.shape[1]
        return jnp.swapaxes(x.reshape(rows_major, rows_minor, w), 0, 1).reshape(tt, w)

    u = regroup(u_ref[...], SUBLANES, seg)
    ub = u.astype(BF16)
    _ssm_in(ub, bdr_ref, xr_ref)
    _ssm_in(ub, bdi_ref, xi_ref)

    first_row = lax.broadcasted_iota(jnp.int32, (SUBLANES, SCAN_LANES), 0) == 0
    for lc in range(SSM_LANES // SCAN_LANES):
        sl = slice(lc * SCAN_LANES, (lc + 1) * SCAN_LANES)
        l_re = tab_ref[0, :, sl]
        l_im = tab_ref[1, :, sl]

        def rows(j):
            return pl.ds(pl.multiple_of(j * SUBLANES, SUBLANES), SUBLANES)

        def advance(j, x, sl=sl, l_re=l_re, l_im=l_im):
            x_re, x_im = _cmul(l_re, l_im, *x)
            x_re = x_re + xr_ref[rows(j), sl]
            x_im = x_im + xi_ref[rows(j), sl]
            xr_ref[rows(j), sl] = x_re
            xi_ref[rows(j), sl] = x_im
            return x_re, x_im

        zero = jnp.zeros((SUBLANES, SCAN_LANES), F32)
        f_re, f_im = lax.fori_loop(0, seg, advance, (zero, zero), unroll=4)

        c_re = cr_ref[:, sl]
        c_im = ci_ref[:, sl]
        for i in range(3):
            s_re = pltpu.roll(f_re, 1 << i, axis=0)
            s_im = pltpu.roll(f_im, 1 << i, axis=0)
            m_re, m_im = _cmul(tab_ref[2 + 2 * i, :, sl], tab_ref[3 + 2 * i, :, sl], s_re, s_im)
            f_re, f_im = f_re + m_re, f_im + m_im
        m_re, m_im = _cmul(tab_ref[8, :, sl], tab_ref[9, :, sl], c_re, c_im)
        f_re, f_im = f_re + m_re, f_im + m_im
        e_re = jnp.where(first_row, c_re, pltpu.roll(f_re, 1, axis=0))
        e_im = jnp.where(first_row, c_im, pltpu.roll(f_im, 1, axis=0))
        cr_ref[:, sl] = jnp.broadcast_to(f_re[SUBLANES - 1:SUBLANES, :], (SUBLANES, SCAN_LANES))
        ci_ref[:, sl] = jnp.broadcast_to(f_im[SUBLANES - 1:SUBLANES, :], (SUBLANES, SCAN_LANES))

        def fix(j, w, sl=sl, l_re=l_re, l_im=l_im):
            w_re, w_im = _cmul(l_re, l_im, *w)
            xr_ref[rows(j), sl] += w_re
            xi_ref[rows(j), sl] += w_im
            return w_re, w_im

        lax.fori_loop(0, seg, fix, (e_re, e_im), unroll=4)

    for c in range(SSM_CHUNKS):
        ch = slice(c * MXU_DIM, (c + 1) * MXU_DIM)
        g = _ssm_out(c, u, xr_ref, xi_ref, cdr_ref, cdi_ref, d_ref)
        g_ref[:, ch] = regroup(g, seg, SUBLANES).astype(BF16)

    @pl.when(t == pl.num_programs(1) - 1)
    def _():
        last_re_ref[...] = cr_ref[0:1, :]
        last_im_ref[...] = ci_ref[0:1, :]


def _const_spec(shape):
    return pl.BlockSpec(shape, lambda *_: (0,) * len(shape))


def _ssm_seq(proj, h_re, h_im, tab, bdr, bdi, cdr, cdi, d_skip, batch, seq):
    tt = SSM_TIME_TILE
    nt = seq // tt
    col_block = POOL_WIDTH // SSM_WIDTH
    state_spec = pl.BlockSpec((None, 1, SSM_LANES), lambda b, t: (b, 0, 0))
    state_shape = jax.ShapeDtypeStruct((batch, 1, SSM_LANES), F32)
    return pl.pallas_call(
        _ssm_seq_kernel,
        grid=(batch, nt),
        in_specs=[
            pl.BlockSpec((tt, SSM_WIDTH), lambda b, t: (b * nt + t, col_block)),
            state_spec, state_spec,
            _const_spec(tab.shape),
            _const_spec(bdr.shape), _const_spec(bdi.shape),
            _const_spec(cdr.shape), _const_spec(cdi.shape),
            _const_spec(d_skip.shape),
        ],
        out_specs=[pl.BlockSpec((tt, SSM_WIDTH), lambda b, t: (b * nt + t, 0)), state_spec, state_spec],
        out_shape=[jax.ShapeDtypeStruct((batch * seq, SSM_WIDTH), BF16), state_shape, state_shape],
        scratch_shapes=[
            pltpu.VMEM((tt, SSM_LANES), F32), pltpu.VMEM((tt, SSM_LANES), F32),
            pltpu.VMEM((SUBLANES, SSM_LANES), F32), pltpu.VMEM((SUBLANES, SSM_LANES), F32),
        ],
        compiler_params=_params("parallel", "arbitrary"),
        name="ssm_seq",
    )(proj, h_re, h_im, tab, bdr, bdi, cdr, cdi, d_skip)


def _ssm_step_kernel(u_ref, h_re_ref, h_im_ref, lb_ref, bdr_ref, bdi_ref, cdr_ref, cdi_ref, d_ref,
                     g_ref, x_re_ref, x_im_ref):
    u = u_ref[...]
    ub = u.astype(BF16)
    _ssm_in(ub, bdr_ref, x_re_ref)
    _ssm_in(ub, bdi_ref, x_im_ref)
    m_re, m_im = _cmul(lb_ref[0:1, :], lb_ref[1:2, :], h_re_ref[...], h_im_ref[...])
    x_re_ref[...] += m_re
    x_im_ref[...] += m_im
    for c in range(SSM_CHUNKS):
        ch = slice(c * MXU_DIM, (c + 1) * MXU_DIM)
        g_ref[:, ch] = _ssm_out(c, u, x_re_ref, x_im_ref, cdr_ref, cdi_ref, d_ref).astype(BF16)


def _ssm_step(proj, h_re, h_im, lb, bdr, bdi, cdr, cdi, d_skip):
    m = proj.shape[0]
    state_spec = pl.BlockSpec((m, SSM_LANES), lambda i: (0, 0))
    state_shape = jax.ShapeDtypeStruct((m, SSM_LANES), F32)
    return pl.pallas_call(
        _ssm_step_kernel,
        grid=(1,),
        in_specs=[
            pl.BlockSpec((m, SSM_WIDTH), lambda i: (0, POOL_WIDTH // SSM_WIDTH)),
            state_spec, state_spec,
            _const_spec(lb.shape),
            _const_spec(bdr.shape), _const_spec(bdi.shape),
            _const_spec(cdr.shape), _const_spec(cdi.shape),
            _const_spec(d_skip.shape),
        ],
        out_specs=[pl.BlockSpec((m, SSM_WIDTH), lambda i: (0, 0)), state_spec, state_spec],
        out_shape=[jax.ShapeDtypeStruct((m, SSM_WIDTH), BF16), state_shape, state_shape],
        compiler_params=_params("arbitrary"),
        name="ssm_step",
    )(proj, h_re, h_im, lb, bdr, bdi, cdr, cdi, d_skip)


def _xa_seq_kernel(q_ref, k_ref, v_ref, o_ref):
    for h in range(XA_HEADS):
        ch = slice(h * XA_HEAD_DIM, (h + 1) * XA_HEAD_DIM)
        q = q_ref[:, ch].astype(BF16)
        k = k_ref[:, ch].astype(BF16)
        v = v_ref[:, ch].astype(BF16)
        s = lax.dot_general(q, k, (((1,), (1,)), ((), ())), preferred_element_type=F32) * XA_SCALE
        e = jnp.exp(s - jnp.max(s, axis=-1, keepdims=True))
        p = e / jnp.sum(e, axis=-1, keepdims=True)
        o_ref[:, ch] = _dot(p.astype(BF16), v).astype(BF16)


def _xa_seq(proj, mem_k, mem_v, batch, seq, tq):
    nt = seq // tq
    q_block = (POOL_WIDTH + SSM_WIDTH) // XA_WIDTH
    kv_spec = pl.BlockSpec((N_MEM, XA_WIDTH), lambda b, t: (b, 0))
    return pl.pallas_call(
        _xa_seq_kernel,
        grid=(batch, nt),
        in_specs=[pl.BlockSpec((tq, XA_WIDTH), lambda b, t: (b * nt + t, q_block)), kv_spec, kv_spec],
        out_specs=pl.BlockSpec((tq, XA_WIDTH), lambda b, t: (b * nt + t, 0)),
        out_shape=jax.ShapeDtypeStruct((batch * seq, XA_WIDTH), BF16),
        compiler_params=_params("parallel", "parallel"),
        name="xa_seq",
    )(proj, mem_k, mem_v)


def _xa_step_kernel(q_ref, k_ref, v_ref, o_ref):
    q = q_ref[...]
    s = jnp.sum(k_ref[...] * q, axis=-1, keepdims=True) * XA_SCALE
    e = jnp.exp(s - jnp.max(s, axis=1, keepdims=True))
    p = e / jnp.sum(e, axis=1, keepdims=True)
    o_ref[...] = jnp.sum(p * v_ref[...], axis=1, keepdims=True)


def _xa_step(q4, cache_k, cache_v, tb):
    m = q4.shape[0]
    kv_spec = pl.BlockSpec((tb, N_MEM, XA_HEADS, XA_HEAD_DIM), lambda i: (i, 0, 0, 0))
    q_spec = pl.BlockSpec((tb, 1, XA_HEADS, XA_HEAD_DIM), lambda i: (i, 0, 0, 0))
    return pl.pallas_call(
        _xa_step_kernel,
        grid=(m // tb,),
        in_specs=[q_spec, kv_spec, kv_spec],
        out_specs=q_spec,
        out_shape=jax.ShapeDtypeStruct(q4.shape, F32),
        compiler_params=_params("parallel"),
        name="xa_step",
    )(q4, cache_k, cache_v)


def _gate_kernel(z_ref, g_ref, o_ref, ga_ref, gb_ref, gc_ref, wpo_ref, wv_ref, wg_ref, wxo_ref, m_ref, *copy_refs):
    copies = copy_refs if copy_refs else (None,) * 4
    g = g_ref[...]
    o_pool = _dot(z_ref[...], _weight(wpo_ref, copies[0]))
    o_ssm = _dot(g, _weight(wv_ref, copies[1])) * jax.nn.sigmoid(_dot(g, _weight(wg_ref, copies[2])))
    o_xa = _dot(o_ref[...], _weight(wxo_ref, copies[3]))
    merged = (jax.nn.sigmoid(ga_ref[...]) * o_pool + jax.nn.sigmoid(gb_ref[...]) * o_ssm
              + jax.nn.sigmoid(gc_ref[...]) * o_xa)
    m_ref[...] = merged.astype(BF16)


def _gate(z, g, o, proj, weights, tm, tn):
    m, half = z.shape
    d = weights[0].shape[1]
    emit_w = weights[0].dtype != BF16
    assert not emit_w or m == tm
    gate0 = OFF_GATE // tn
    per_gate = d // tn
    branch_spec = pl.BlockSpec((tm, half), lambda i, n: (i, 0))
    w_specs = [pl.BlockSpec((half, tn), lambda i, n: (0, n))] * 4
    copy_specs, copy_shapes = _with_copies(w_specs, weights, emit_w)

    def gate_spec(k):
        return pl.BlockSpec((tm, tn), lambda i, n: (i, gate0 + k * per_gate + n))

    outs = pl.pallas_call(
        _gate_kernel,
        grid=(m // tm, d // tn),
        in_specs=[branch_spec] * 3 + [gate_spec(0), gate_spec(1), gate_spec(2)] + w_specs,
        out_specs=[pl.BlockSpec((tm, tn), lambda i, n: (i, n))] + copy_specs,
        out_shape=[jax.ShapeDtypeStruct((m, d), BF16)] + copy_shapes,
        compiler_params=_params("parallel", "arbitrary"),
        name="gate",
    )(z, g, o, proj, proj, proj, *weights)
    return outs[0], (tuple(outs[1:]) if emit_w else tuple(weights))


def _mixout_kernel(m_ref, wout_ref, h_ref, gpost_ref, gnext_ref, h2_ref, xn_ref, *copy_ref, sub):
    wout = _weight(wout_ref, *copy_ref)
    for r in range(m_ref.shape[0] // sub):
        rows = slice(r * sub, (r + 1) * sub)
        h2 = h_ref[rows, :] + _rms(_dot(m_ref[rows, :], wout), gpost_ref[...])
        h2_ref[rows, :] = h2
        xn_ref[rows, :] = _rms(h2, gnext_ref[...]).astype(BF16)


def _mixout(merged, w_out, h, g_post, g_next, tm):
    m, d = h.shape
    emit_w = w_out.dtype != BF16
    assert not emit_w or m == tm
    row_spec = pl.BlockSpec((tm, d), lambda i: (i, 0))
    gain_spec = pl.BlockSpec((1, d), lambda i: (0, 0))
    w_spec = pl.BlockSpec((d, d), lambda i: (0, 0))
    copy_specs, copy_shapes = _with_copies([w_spec], [w_out], emit_w)
    outs = pl.pallas_call(
        functools.partial(_mixout_kernel, sub=min(tm, MXU_DIM)),
        grid=(m // tm,),
        in_specs=[row_spec, w_spec, row_spec, gain_spec, gain_spec],
        out_specs=[row_spec, row_spec] + copy_specs,
        out_shape=[jax.ShapeDtypeStruct((m, d), F32), jax.ShapeDtypeStruct((m, d), BF16)] + copy_shapes,
        compiler_params=_params("parallel"),
        name="mixout",
    )(merged, w_out, h, g_post, g_next)
    return outs[:2], (outs[2] if emit_w else w_out)


def _block_diag_in(bb):
    gpc = SSM_GROUPS // SSM_CHUNKS
    x = bb.reshape(SSM_CHUNKS, gpc, SSM_STATE, SSM_GROUP_CH).transpose(0, 1, 3, 2)
    eye = jnp.eye(gpc, dtype=bb.dtype)
    y = x[:, :, :, None, :] * eye[None, :, None, :, None]
    return y.reshape(SSM_CHUNKS, gpc * SSM_GROUP_CH, gpc * SSM_STATE)


def _block_diag_out(c):
    gpc = SSM_GROUPS // SSM_CHUNKS
    x = c.reshape(SSM_CHUNKS, gpc, SSM_GROUP_CH, SSM_STATE).transpose(0, 1, 3, 2)
    eye = jnp.eye(gpc, dtype=c.dtype)
    y = x[:, :, :, None, :] * eye[None, :, None, :, None]
    return y.reshape(SSM_CHUNKS, gpc * SSM_STATE, gpc * SSM_GROUP_CH)


def _scan_table(pw_re, pw_im, ps_re, ps_im):
    flat = lambda p: p.reshape(SUBLANES, SSM_LANES)
    pw_re, pw_im, ps_re, ps_im = flat(pw_re), flat(pw_im), flat(ps_re), flat(ps_im)
    row = jnp.arange(SUBLANES)[:, None]
    tiles = [jnp.broadcast_to(pw_re[0], (SUBLANES, SSM_LANES)), jnp.broadcast_to(pw_im[0], (SUBLANES, SSM_LANES))]
    for d in (1, 2, 4):
        keep = row >= d
        tiles.append(jnp.where(keep, ps_re[d - 1][None, :], 0.0))
        tiles.append(jnp.where(keep, ps_im[d - 1][None, :], 0.0))
    tiles += [ps_re, ps_im]
    return jnp.stack(tiles)


def _row_tile(m, target):
    return target if m % target == 0 else m


FFN1_W = ('w_ff1_gate', 'w_ff1_up', 'w_ff1_down')
FFN2_W = ('w_ff2_gate', 'w_ff2_up', 'w_ff2_down')
GATE_W = ('w_pool_out', 'w_glu_val', 'w_glu_gate', 'w_xa_out')
TILES = dict(tm=512, tf=512, tm_proj=2048, tm_gate=1024, tn=512)


def _layer(x, w, mixers):
    m = x.shape[0]
    tm = _row_tile(m, TILES['tm'])
    wb = dict(w)
    (h, xn), wb_ffn1 = _ffn(x, w['g_ff1_pre'], [w[k] for k in FFN1_W], w['g_ff1_post'], tm, TILES['tf'],
                            g_next=w['g_mix_pre'])
    proj, wb['w_in'] = _proj(xn, w['w_in'], _row_tile(m, TILES['tm_proj']), TILES['tn'])
    z, g, o, states = mixers(proj)
    merged, wb_gate = _gate(z, g, o, proj, [w[k] for k in GATE_W], _row_tile(m, TILES['tm_gate']), TILES['tn'])
    (h, xn), wb['w_out'] = _mixout(merged, w['w_out'], h, w['g_mix_post'], w['g_ff2_pre'], tm)
    (y,), wb_ffn2 = _ffn(h, xn, [w[k] for k in FFN2_W], w['g_ff2_post'], tm, TILES['tf'])
    wb.update(zip(FFN1_W + GATE_W + FFN2_W, wb_ffn1 + wb_gate + wb_ffn2))
    return y, states, wb


def kernel(x_prompt, x_sample, mem_prompt, cache_mem_k, cache_mem_v, state_pool, state_ssm_re, state_ssm_im, g_ff1_pre, w_ff1_gate, w_ff1_up, w_ff1_down, g_ff1_post, g_mix_pre, w_in, w_pool_grp, pool_scale, w_pool_out, ssm_a_re, ssm_a_im, ssm_log_step, ssm_b_re, ssm_b_im, ssm_c_re, ssm_c_im, ssm_d, w_glu_val, w_glu_gate, g_mem, w_mem_k, w_mem_v, w_xa_out, w_out, g_mix_post, g_ff2_pre, w_ff2_gate, w_ff2_up, w_ff2_down, g_ff2_post):
    bp, lp, d = x_prompt.shape
    bs, ls, _ = x_sample.shape
    depth = w_in.shape[0]
    assert depth == 1 and ls == 1 and lp >= POOL_BUF and lp % SSM_TIME_TILE == 0

    mats = dict(w_ff1_gate=w_ff1_gate, w_ff1_up=w_ff1_up, w_ff1_down=w_ff1_down, w_in=w_in,
                w_pool_out=w_pool_out, w_glu_val=w_glu_val, w_glu_gate=w_glu_gate, w_xa_out=w_xa_out,
                w_out=w_out, w_ff2_gate=w_ff2_gate, w_ff2_up=w_ff2_up, w_ff2_down=w_ff2_down)
    w = {k: v[0] for k, v in mats.items()}
    w.update(g_ff1_pre=g_ff1_pre, g_ff1_post=g_ff1_post, g_mix_pre=g_mix_pre, g_mix_post=g_mix_post,
             g_ff2_pre=g_ff2_pre, g_ff2_post=g_ff2_post)
    w_grp = w_pool_grp[0].astype(BF16)
    scale = pool_scale
    d_skip = ssm_d

    bb_re, bb_im, pw_re, pw_im, ps_re, ps_im = _ssm_disc(
        ssm_a_re[0], ssm_a_im[0], ssm_log_step[0], ssm_b_re[0], ssm_b_im[0])
    bdr = _block_diag_in(bb_re.astype(BF16))
    bdi = _block_diag_in(bb_im.astype(BF16))
    cdr = _block_diag_out(ssm_c_re[0].astype(BF16))
    cdi = _block_diag_out(ssm_c_im[0].astype(BF16))
    tab = _scan_table(pw_re, pw_im, ps_re, ps_im)
    lb = jnp.stack([pw_re[0].reshape(SSM_LANES), pw_im[0].reshape(SSM_LANES)])

    def sample_mixers(proj):
        z, new_buf = _pool_step(proj, state_pool[0].reshape(bs, POOL_BUF * POOL_WIDTH), w_grp, scale, 32)
        g, x_re, x_im = _ssm_step(proj, state_ssm_re[0].reshape(bs, SSM_LANES),
                                  state_ssm_im[0].reshape(bs, SSM_LANES), lb, bdr, bdi, cdr, cdi, d_skip)
        q4 = proj[:, POOL_WIDTH + SSM_WIDTH:OFF_GATE].reshape(bs, 1, XA_HEADS, XA_HEAD_DIM)
        o = _xa_step(q4, cache_mem_k[0], cache_mem_v[0], 4)
        return z, g, o.reshape(bs, XA_WIDTH).astype(BF16), (new_buf, x_re, x_im)

    y_sample, (pool_s, re_s, im_s), wb = _layer(x_sample.reshape(bs * ls, d), w, sample_mixers)

    mem = mem_prompt.reshape(bp * N_MEM, d)
    mem_k = _normproj(mem, g_mem, w_mem_k[0], bp * N_MEM, 512)
    mem_v = _normproj(mem, g_mem, w_mem_v[0], bp * N_MEM, 512)

    def prompt_mixers(proj):
        z = _pool_seq(proj, w_grp, scale, bp, lp, 512)
        zeros = jnp.zeros((bp, 1, SSM_LANES), F32)
        g, last_re, last_im = _ssm_seq(proj, zeros, zeros, tab, bdr, bdi, cdr, cdi, d_skip, bp, lp)
        o = _xa_seq(proj, mem_k, mem_v, bp, lp, 512)
        pool_rows = proj.reshape(bp, lp, IN_WIDTH)[:, lp - POOL_BUF:, :POOL_WIDTH]
        return z, g, o, (pool_rows, last_re, last_im)

    y_prompt, (pool_p, re_p, im_p), _ = _layer(x_prompt.reshape(bp * lp, d), wb, prompt_mixers)

    kv_shape = (1, bp, N_MEM, XA_HEADS, XA_HEAD_DIM)
    st_shape = (SSM_GROUPS, SSM_STATE)
    return (y_prompt.reshape(bp, lp, d), y_sample.reshape(bs, ls, d),
            mem_k.reshape(kv_shape), mem_v.reshape(kv_shape),
            pool_p[None],
            re_p.reshape((1, bp) + st_shape), im_p.reshape((1, bp) + st_shape),
            pool_s.reshape(1, bs, POOL_BUF, POOL_WIDTH),
            re_s.reshape((1, bs) + st_shape), im_s.reshape((1, bs) + st_shape))
```

```python
import functools
import math

import jax
import jax.numpy as jnp
from jax import lax
from jax.experimental import pallas as pl
from jax.experimental.pallas import tpu as pltpu

F32 = jnp.float32
BF16 = jnp.bfloat16

D_MODEL = 2048
PAST_LEN = 16384
POOL_WIDTH = D_MODEL // 2
POOL_WINDOWS = (2, 4, 8, 16)
POOL_GROUP_DIM = POOL_WIDTH // len(POOL_WINDOWS)
POOL_BUF = max(POOL_WINDOWS) - 1
SSM_WIDTH = D_MODEL // 2
SSM_GROUP_CH = 16
SSM_GROUPS = SSM_WIDTH // SSM_GROUP_CH
SSM_STATE = 64
SSM_LANES = SSM_GROUPS * SSM_STATE
XA_HEADS = 4
XA_HEAD_DIM = D_MODEL // 8
XA_WIDTH = XA_HEADS * XA_HEAD_DIM
XA_SCALE = XA_HEAD_DIM ** -0.5
N_MEM = 256
OFF_GATE = POOL_WIDTH + SSM_WIDTH + XA_WIDTH
IN_WIDTH = OFF_GATE + 3 * D_MODEL
RMS_EPS = 1e-6

SUBLANES = 8
LANES = 128
MXU_DIM = 256
VMEM_LIMIT_BYTES = 60 * 1024 * 1024
DOWN_CHUNKS = 4

SSM_CHUNKS = SSM_WIDTH // MXU_DIM
SSM_CHUNK_LANES = SSM_LANES // SSM_CHUNKS
SSM_TIME_TILE = 256
SSM_SEG = SSM_TIME_TILE // SUBLANES
SCAN_LANES = 4 * LANES


def _params(*sem):
    return pltpu.CompilerParams(dimension_semantics=sem, vmem_limit_bytes=VMEM_LIMIT_BYTES)


def _rms(x, g):
    return (x * lax.rsqrt(jnp.mean(x * x, axis=-1, keepdims=True) + RMS_EPS)) * g


def _dot(a, b):
    return jnp.dot(a, b, preferred_element_type=F32)


def _cmul(a_re, a_im, b_re, b_im):
    return a_re * b_re - a_im * b_im, a_re * b_im + a_im * b_re


def _weight(w_ref, copy_ref=None):
    w = w_ref[...]
    if w.dtype != BF16:
        w = w.astype(BF16)
    if copy_ref is not None:
        copy_ref[...] = w
    return w


def _with_copies(specs, weights, emit_w):
    if not emit_w:
        return [], []
    return list(specs), [jax.ShapeDtypeStruct(w.shape, BF16) for w in weights]


def _col_tiled(w, tn):
    tiled = lambda k: pl.BlockSpec((None, k, tn), lambda i, j: (j, 0, 0))
    if w.dtype == BF16:
        assert w.ndim == 3 and w.shape[2] == tn, "bf16 weights come tile-major from an earlier call"
        return tiled(w.shape[1]), None, None
    k, n = w.shape
    return pl.BlockSpec((k, tn), lambda i, j: (0, j)), tiled(k), jax.ShapeDtypeStruct((n // tn, k, tn), BF16)


def _ffn_kernel(*refs, first, emit_w):
    refs = list(refs)
    x_ref, pre_ref, wg_ref, wu_ref, wd_ref, gpost_ref = refs[:6]
    del refs[:6]
    gnext_ref = refs.pop(0) if first else None
    o_ref = refs.pop(0)
    xnext_ref = refs.pop(0) if first else None
    wgb_ref, wub_ref, wdb_ref = (refs.pop(0), refs.pop(0), refs.pop(0)) if emit_w else (None, None, None)
    xn_ref = refs.pop(0) if first else pre_ref
    j = pl.program_id(1)

    @pl.when(j == 0)
    def _():
        if first:
            xn_ref[...] = _rms(x_ref[...], pre_ref[...]).astype(BF16)
        o_ref[...] = jnp.zeros_like(o_ref)

    xn = xn_ref[...]
    g = _dot(xn, _weight(wg_ref, wgb_ref))
    u = _dot(xn, _weight(wu_ref, wub_ref))
    hid = ((g * jax.nn.sigmoid(g)) * u).astype(BF16)
    wd = _weight(wd_ref, wdb_ref)
    cols = wd.shape[1] // DOWN_CHUNKS
    for n in range(DOWN_CHUNKS):
        sl = slice(n * cols, (n + 1) * cols)
        o_ref[:, sl] += _dot(hid, wd[:, sl])

    @pl.when(j == pl.num_programs(1) - 1)
    def _():
        out = x_ref[...] + 0.5 * _rms(o_ref[...], gpost_ref[...])
        o_ref[...] = out
        if first:
            xnext_ref[...] = _rms(out, gnext_ref[...]).astype(BF16)


def _ffn(x, pre, weights, g_post, tm, tf, g_next=None):
    m, d = x.shape
    w_gate, w_up, w_down = weights
    f = w_down.shape[0]
    first = g_next is not None
    emit_w = w_gate.dtype != BF16
    assert not emit_w or m == tm, "weight copies need every weight block visited exactly once"
    row_spec = pl.BlockSpec((tm, d), lambda i, j: (i, 0))
    gain_spec = pl.BlockSpec((1, d), lambda i, j: (0, 0))
    wg_spec, wg_copy, wg_shape = _col_tiled(w_gate, tf)
    wu_spec, wu_copy, wu_shape = _col_tiled(w_up, tf)
    wd_spec = pl.BlockSpec((tf, d), lambda i, j: (j, 0))
    w_specs = [wg_spec, wu_spec, wd_spec]
    copy_specs = [wg_copy, wu_copy, wd_spec] if emit_w else []
    copy_shapes = [wg_shape, wu_shape, jax.ShapeDtypeStruct(w_down.shape, BF16)] if emit_w else []
    outs = pl.pallas_call(
        functools.partial(_ffn_kernel, first=first, emit_w=emit_w),
        grid=(m // tm, f // tf),
        in_specs=[row_spec, gain_spec if first else row_spec] + w_specs + [gain_spec]
        + ([gain_spec] if first else []),
        out_specs=[row_spec] + ([row_spec] if first else []) + copy_specs,
        out_shape=[jax.ShapeDtypeStruct((m, d), F32)] + ([jax.ShapeDtypeStruct((m, d), BF16)] if first else [])
        + copy_shapes,
        scratch_shapes=[pltpu.VMEM((tm, d), BF16)] if first else [],
        compiler_params=_params("parallel", "arbitrary"),
        name="ffn",
    )(*((x, pre, w_gate, w_up, w_down, g_post) + ((g_next,) if first else ())))
    n_out = 2 if first else 1
    return outs[:n_out], (tuple(outs[n_out:]) if emit_w else tuple(weights))


def _proj_kernel(xn_ref, w_ref, o_ref, *copy_ref):
    o_ref[...] = _dot(xn_ref[...], _weight(w_ref, *copy_ref))


def _proj(xn, w, tm, tn):
    m, d = xn.shape
    emit_w = w.dtype != BF16
    assert not emit_w or m == tm
    n = w.shape[1] if emit_w else w.shape[0] * tn
    w_spec, copy_spec, copy_shape = _col_tiled(w, tn)
    copy_specs, copy_shapes = ([copy_spec], [copy_shape]) if emit_w else ([], [])
    outs = pl.pallas_call(
        _proj_kernel,
        grid=(m // tm, n // tn),
        in_specs=[pl.BlockSpec((tm, d), lambda i, j: (i, 0)), w_spec],
        out_specs=[pl.BlockSpec((tm, tn), lambda i, j: (i, j))] + copy_specs,
        out_shape=[jax.ShapeDtypeStruct((m, n), F32)] + copy_shapes,
        compiler_params=_params("parallel", "arbitrary"),
        name="proj",
    )(xn, w)
    return outs[0], (outs[1] if emit_w else w)


def _normproj_kernel(x_ref, g_ref, w_ref, o_ref, xn_ref):
    @pl.when(pl.program_id(1) == 0)
    def _():
        xn_ref[...] = _rms(x_ref[...], g_ref[...]).astype(BF16)

    o_ref[...] = _dot(xn_ref[...], _weight(w_ref))


def _normproj(x, g, w, tm, tn):
    m, d = x.shape
    n = w.shape[1]
    return pl.pallas_call(
        _normproj_kernel,
        grid=(m // tm, n // tn),
        in_specs=[
            pl.BlockSpec((tm, d), lambda i, j: (i, 0)),
            pl.BlockSpec((1, d), lambda i, j: (0, 0)),
            pl.BlockSpec((d, tn), lambda i, j: (0, j)),
        ],
        out_specs=pl.BlockSpec((tm, tn), lambda i, j: (i, j)),
        out_shape=jax.ShapeDtypeStruct((m, n), F32),
        scratch_shapes=[pltpu.VMEM((tm, d), BF16)],
        compiler_params=_params("parallel", "arbitrary"),
        name="normproj",
    )(x, g, w)


def _pool_group_out(k, diff, wgrp_ref, scale_ref, z_ref):
    ch = slice(k * POOL_GROUP_DIM, (k + 1) * POOL_GROUP_DIM)
    zk = _dot(diff.astype(BF16), wgrp_ref[k]) * scale_ref[:, ch]
    z_ref[:, ch] = zk.astype(BF16)


def _pool_seq_kernel(u_ref, wgrp_ref, scale_ref, z_ref, ext_ref, *, halo):
    t = pl.program_id(1)
    tt = u_ref.shape[0]

    @pl.when(t == 0)
    def _():
        ext_ref[0:halo, :] = jnp.zeros((halo, POOL_WIDTH), F32)

    u = u_ref[...]
    ext_ref[halo:halo + tt, :] = u
    pos = t * tt + lax.broadcasted_iota(jnp.int32, (tt, 1), 0)
    for k, w in enumerate(POOL_WINDOWS):
        ch = slice(k * POOL_GROUP_DIM, (k + 1) * POOL_GROUP_DIM)
        a = ext_ref[:, ch]
        d = 1
        while d < w:
            a = a + pltpu.roll(a, d, axis=0)
            d *= 2
        count = jnp.minimum(pos + 1, w).astype(F32)
        diff = a[halo:, :] / count - u[:, ch]
        _pool_group_out(k, diff, wgrp_ref, scale_ref, z_ref)
    ext_ref[0:halo, :] = ext_ref[tt:tt + halo, :]


def _pool_seq(proj, w_grp, scale, batch, seq, tt):
    halo = 2 * SUBLANES
    nt = seq // tt
    return pl.pallas_call(
        functools.partial(_pool_seq_kernel, halo=halo),
        grid=(batch, nt),
        in_specs=[
            pl.BlockSpec((tt, POOL_WIDTH), lambda b, t: (b * nt + t, 0)),
            pl.BlockSpec(w_grp.shape, lambda b, t: (0, 0, 0)),
            pl.BlockSpec((1, POOL_WIDTH), lambda b, t: (0, 0)),
        ],
        out_specs=pl.BlockSpec((tt, POOL_WIDTH), lambda b, t: (b * nt + t, 0)),
        out_shape=jax.ShapeDtypeStruct((batch * seq, POOL_WIDTH), BF16),
        scratch_shapes=[pltpu.VMEM((tt + halo, POOL_WIDTH), F32)],
        compiler_params=_params("parallel", "arbitrary"),
        name="pool_seq",
    )(proj, w_grp, scale)


def _pool_step_kernel(u_ref, buf_ref, wgrp_ref, scale_ref, z_ref, nbuf_ref):
    u = u_ref[...]
    for k, w in enumerate(POOL_WINDOWS):
        ch = slice(k * POOL_GROUP_DIM, (k + 1) * POOL_GROUP_DIM)
        s = u[:, ch]
        for j in range(1, w):
            row = POOL_BUF - j
            s = s + buf_ref[:, row * POOL_WIDTH + k * POOL_GROUP_DIM:
                            row * POOL_WIDTH + (k + 1) * POOL_GROUP_DIM]
        diff = s / float(min(PAST_LEN + 1, w)) - u[:, ch]
        _pool_group_out(k, diff, wgrp_ref, scale_ref, z_ref)
    keep = (POOL_BUF - 1) * POOL_WIDTH
    nbuf_ref[:, 0:keep] = buf_ref[:, POOL_WIDTH:]
    nbuf_ref[:, keep:] = u


def _pool_step(proj, buf, w_grp, scale, tb):
    m = proj.shape[0]
    return pl.pallas_call(
        _pool_step_kernel,
        grid=(m // tb,),
        in_specs=[
            pl.BlockSpec((tb, POOL_WIDTH), lambda i: (i, 0)),
            pl.BlockSpec((tb, POOL_BUF * POOL_WIDTH), lambda i: (i, 0)),
            pl.BlockSpec(w_grp.shape, lambda i: (0, 0, 0)),
            pl.BlockSpec((1, POOL_WIDTH), lambda i: (0, 0)),
        ],
        out_specs=[
            pl.BlockSpec((tb, POOL_WIDTH), lambda i: (i, 0)),
            pl.BlockSpec((tb, POOL_BUF * POOL_WIDTH), lambda i: (i, 0)),
        ],
        out_shape=[
            jax.ShapeDtypeStruct((m, POOL_WIDTH), BF16),
            jax.ShapeDtypeStruct((m, POOL_BUF * POOL_WIDTH), F32),
        ],
        compiler_params=_params("parallel"),
        name="pool_step",
    )(proj, buf, w_grp, scale)


def _zoh_lambda(a_re, a_im, log_step):
    dt = jnp.exp(log_step)
    mag = jnp.exp(a_re * dt)
    ang = a_im * dt
    return mag * jnp.cos(ang), mag * jnp.sin(ang)


def _ssm_disc_kernel(a_re_ref, a_im_ref, ls_ref, a_re_col_ref, a_im_col_ref, ls_col_ref, b_re_ref, b_im_ref,
                     bb_re_ref, bb_im_ref, pw_re_ref, pw_im_ref, ps_re_ref, ps_im_ref):
    a_re = a_re_col_ref[...]
    a_im = a_im_col_ref[...]
    lb_re, lb_im = _zoh_lambda(a_re, a_im, ls_col_ref[...])
    den = a_re * a_re + a_im * a_im
    n_re = lb_re - 1.0
    f_re = (n_re * a_re + lb_im * a_im) / den
    f_im = (lb_im * a_re - n_re * a_im) / den
    b_re = b_re_ref[...]
    b_im = b_im_ref[...]
    bb_re_ref[...] = f_re * b_re - f_im * b_im
    bb_im_ref[...] = f_re * b_im + f_im * b_re

    lb_re, lb_im = _zoh_lambda(a_re_ref[...], a_im_ref[...], ls_ref[...])

    def powers(base_re, base_im, out_re_ref, out_im_ref):
        p_re, p_im = base_re, base_im
        out_re_ref[0] = p_re
        out_im_ref[0] = p_im
        for r in range(1, SUBLANES):
            p_re, p_im = _cmul(p_re, p_im, base_re, base_im)
            out_re_ref[r] = p_re
            out_im_ref[r] = p_im
        return p_re, p_im

    s_re, s_im = powers(lb_re, lb_im, pw_re_ref, pw_im_ref)
    for _ in range(int(math.log2(SSM_SEG // SUBLANES))):
        s_re, s_im = _cmul(s_re, s_im, s_re, s_im)
    powers(s_re, s_im, ps_re_ref, ps_im_ref)


def _ssm_disc(a_re, a_im, log_step, b_re, b_im):
    g, n, h = b_re.shape
    col = (g * n, 1)
    pw = jax.ShapeDtypeStruct((SUBLANES, g, n), F32)
    bb = jax.ShapeDtypeStruct((g * n, h), F32)
    bb_re, bb_im, pw_re, pw_im, ps_re, ps_im = pl.pallas_call(
        _ssm_disc_kernel,
        out_shape=[bb, bb, pw, pw, pw, pw],
        name="ssm_disc",
    )(a_re, a_im, log_step[:, None],
      a_re.reshape(col), a_im.reshape(col), jnp.broadcast_to(log_step[:, None], (g, n)).reshape(col),
      b_re.reshape(g * n, h), b_im.reshape(g * n, h))
    return bb_re.reshape(g, n, h), bb_im.reshape(g, n, h), pw_re, pw_im, ps_re, ps_im


def _ssm_in(ub, bd_ref, x_ref):
    for c in range(SSM_CHUNKS):
        x_ref[:, c * SSM_CHUNK_LANES:(c + 1) * SSM_CHUNK_LANES] = _dot(
            ub[:, c * MXU_DIM:(c + 1) * MXU_DIM], bd_ref[c])


def _ssm_out(c, u, x_re_ref, x_im_ref, cdr_ref, cdi_ref, d_ref):
    ch = slice(c * MXU_DIM, (c + 1) * MXU_DIM)
    st = slice(c * SSM_CHUNK_LANES, (c + 1) * SSM_CHUNK_LANES)
    y = _dot(x_re_ref[:, st].astype(BF16), cdr_ref[c]) - _dot(x_im_ref[:, st].astype(BF16), cdi_ref[c])
    return jax.nn.gelu(y + d_ref[:, ch] * u[:, ch])


def _ssm_seq_kernel(u_ref, h_re_ref, h_im_ref, tab_ref, bdr_ref, bdi_ref, cdr_ref, cdi_ref, d_ref,
                    g_ref, last_re_ref, last_im_ref, xr_ref, xi_ref, cr_ref, ci_ref):
    t = pl.program_id(1)
    tt = u_ref.shape[0]
    seg = tt // SUBLANES

    @pl.when(t == 0)
    def _():
        cr_ref[...] = jnp.broadcast_to(h_re_ref[...], cr_ref.shape)
        ci_ref[...] = jnp.broadcast_to(h_im_ref[...], ci_ref.shape)

    def regroup(x, rows_major, rows_minor):
        w = x.shape[1]
        return jnp.swapaxes(x.reshape(rows_major, rows_minor, w), 0, 1).reshape(tt, w)

    u = regroup(u_ref[...], SUBLANES, seg)
    ub = u.astype(BF16)
    _ssm_in(ub, bdr_ref, xr_ref)
    _ssm_in(ub, bdi_ref, xi_ref)

    first_row = lax.broadcasted_iota(jnp.int32, (SUBLANES, SCAN_LANES), 0) == 0
    for lc in range(SSM_LANES // SCAN_LANES):
        sl = slice(lc * SCAN_LANES, (lc + 1) * SCAN_LANES)
        l_re = tab_ref[0, :, sl]
        l_im = tab_ref[1, :, sl]

        def rows(j):
            return pl.ds(j * SUBLANES, SUBLANES)

        def advance(j, x, sl=sl, l_re=l_re, l_im=l_im):
            x_re, x_im = _cmul(l_re, l_im, *x)
            x_re = x_re + xr_ref[rows(j), sl]
            x_im = x_im + xi_ref[rows(j), sl]
            xr_ref[rows(j), sl] = x_re
            xi_ref[rows(j), sl] = x_im
            return x_re, x_im

        zero = jnp.zeros((SUBLANES, SCAN_LANES), F32)
        f_re, f_im = zero, zero
        for j in range(seg):
            f_re, f_im = advance(j, (f_re, f_im))

        c_re = cr_ref[:, sl]
        c_im = ci_ref[:, sl]
        for i in range(3):
            s_re = pltpu.roll(f_re, 1 << i, axis=0)
            s_im = pltpu.roll(f_im, 1 << i, axis=0)
            m_re, m_im = _cmul(tab_ref[2 + 2 * i, :, sl], tab_ref[3 + 2 * i, :, sl], s_re, s_im)
            f_re, f_im = f_re + m_re, f_im + m_im
        m_re, m_im = _cmul(tab_ref[8, :, sl], tab_ref[9, :, sl], c_re, c_im)
        f_re, f_im = f_re + m_re, f_im + m_im
        e_re = jnp.where(first_row, c_re, pltpu.roll(f_re, 1, axis=0))
        e_im = jnp.where(first_row, c_im, pltpu.roll(f_im, 1, axis=0))
        cr_ref[:, sl] = jnp.broadcast_to(f_re[SUBLANES - 1:SUBLANES, :], (SUBLANES, SCAN_LANES))
        ci_ref[:, sl] = jnp.broadcast_to(f_im[SUBLANES - 1:SUBLANES, :], (SUBLANES, SCAN_LANES))

        def fix(j, w, sl=sl, l_re=l_re, l_im=l_im):
            w_re, w_im = _cmul(l_re, l_im, *w)
            xr_ref[rows(j), sl] += w_re
            xi_ref[rows(j), sl] += w_im
            return w_re, w_im

        w = (e_re, e_im)
        for j in range(seg):
            w = fix(j, w)

    for c in range(SSM_CHUNKS):
        ch = slice(c * MXU_DIM, (c + 1) * MXU_DIM)
        g = _ssm_out(c, u, xr_ref, xi_ref, cdr_ref, cdi_ref, d_ref)
        g_ref[:, ch] = regroup(g, seg, SUBLANES).astype(BF16)

    @pl.when(t == pl.num_programs(1) - 1)
    def _():
        last_re_ref[...] = cr_ref[0:1, :]
        last_im_ref[...] = ci_ref[0:1, :]


def _const_spec(shape):
    return pl.BlockSpec(shape, lambda *_: (0,) * len(shape))


def _ssm_seq(proj, h_re, h_im, tab, bdr, bdi, cdr, cdi, d_skip, batch, seq):
    tt = SSM_TIME_TILE
    nt = seq // tt
    col_block = POOL_WIDTH // SSM_WIDTH
    state_spec = pl.BlockSpec((None, 1, SSM_LANES), lambda b, t: (b, 0, 0))
    state_shape = jax.ShapeDtypeStruct((batch, 1, SSM_LANES), F32)
    return pl.pallas_call(
        _ssm_seq_kernel,
        grid=(batch, nt),
        in_specs=[
            pl.BlockSpec((tt, SSM_WIDTH), lambda b, t: (b * nt + t, col_block)),
            state_spec, state_spec,
            _const_spec(tab.shape),
            _const_spec(bdr.shape), _const_spec(bdi.shape),
            _const_spec(cdr.shape), _const_spec(cdi.shape),
            _const_spec(d_skip.shape),
        ],
        out_specs=[pl.BlockSpec((tt, SSM_WIDTH), lambda b, t: (b * nt + t, 0)), state_spec, state_spec],
        out_shape=[jax.ShapeDtypeStruct((batch * seq, SSM_WIDTH), BF16), state_shape, state_shape],
        scratch_shapes=[
            pltpu.VMEM((tt, SSM_LANES), F32), pltpu.VMEM((tt, SSM_LANES), F32),
            pltpu.VMEM((SUBLANES, SSM_LANES), F32), pltpu.VMEM((SUBLANES, SSM_LANES), F32),
        ],
        compiler_params=_params("parallel", "arbitrary"),
        name="ssm_seq",
    )(proj, h_re, h_im, tab, bdr, bdi, cdr, cdi, d_skip)


def _ssm_step_kernel(u_ref, h_re_ref, h_im_ref, lb_ref, bdr_ref, bdi_ref, cdr_ref, cdi_ref, d_ref,
                     g_ref, x_re_ref, x_im_ref):
    u = u_ref[...]
    ub = u.astype(BF16)
    _ssm_in(ub, bdr_ref, x_re_ref)
    _ssm_in(ub, bdi_ref, x_im_ref)
    m_re, m_im = _cmul(lb_ref[0:1, :], lb_ref[1:2, :], h_re_ref[...], h_im_ref[...])
    x_re_ref[...] += m_re
    x_im_ref[...] += m_im
    for c in range(SSM_CHUNKS):
        ch = slice(c * MXU_DIM, (c + 1) * MXU_DIM)
        g_ref[:, ch] = _ssm_out(c, u, x_re_ref, x_im_ref, cdr_ref, cdi_ref, d_ref).astype(BF16)


def _ssm_step(proj, h_re, h_im, lb, bdr, bdi, cdr, cdi, d_skip):
    m = proj.shape[0]
    state_spec = pl.BlockSpec((m, SSM_LANES), lambda i: (0, 0))
    state_shape = jax.ShapeDtypeStruct((m, SSM_LANES), F32)
    return pl.pallas_call(
        _ssm_step_kernel,
        grid=(1,),
        in_specs=[
            pl.BlockSpec((m, SSM_WIDTH), lambda i: (0, POOL_WIDTH // SSM_WIDTH)),
            state_spec, state_spec,
            _const_spec(lb.shape),
            _const_spec(bdr.shape), _const_spec(bdi.shape),
            _const_spec(cdr.shape), _const_spec(cdi.shape),
            _const_spec(d_skip.shape),
        ],
        out_specs=[pl.BlockSpec((m, SSM_WIDTH), lambda i: (0, 0)), state_spec, state_spec],
        out_shape=[jax.ShapeDtypeStruct((m, SSM_WIDTH), BF16), state_shape, state_shape],
        compiler_params=_params("arbitrary"),
        name="ssm_step",
    )(proj, h_re, h_im, lb, bdr, bdi, cdr, cdi, d_skip)


def _xa_seq_kernel(q_ref, k_ref, v_ref, o_ref):
    for h in range(XA_HEADS):
        ch = slice(h * XA_HEAD_DIM, (h + 1) * XA_HEAD_DIM)
        q = q_ref[:, ch].astype(BF16)
        k = k_ref[:, ch].astype(BF16)
        v = v_ref[:, ch].astype(BF16)
        s = lax.dot_general(q, k, (((1,), (1,)), ((), ())), preferred_element_type=F32) * XA_SCALE
        e = jnp.exp(s - jnp.max(s, axis=-1, keepdims=True))
        p = e / jnp.sum(e, axis=-1, keepdims=True)
        o_ref[:, ch] = _dot(p.astype(BF16), v).astype(BF16)


def _xa_seq(proj, mem_k, mem_v, batch, seq, tq):
    nt = seq // tq
    q_block = (POOL_WIDTH + SSM_WIDTH) // XA_WIDTH
    kv_spec = pl.BlockSpec((N_MEM, XA_WIDTH), lambda b, t: (b, 0))
    return pl.pallas_call(
        _xa_seq_kernel,
        grid=(batch, nt),
        in_specs=[pl.BlockSpec((tq, XA_WIDTH), lambda b, t: (b * nt + t, q_block)), kv_spec, kv_spec],
        out_specs=pl.BlockSpec((tq, XA_WIDTH), lambda b, t: (b * nt + t, 0)),
        out_shape=jax.ShapeDtypeStruct((batch * seq, XA_WIDTH), BF16),
        compiler_params=_params("parallel", "parallel"),
        name="xa_seq",
    )(proj, mem_k, mem_v)


def _xa_step_kernel(q_ref, k_ref, v_ref, o_ref):
    q = q_ref[...]
    s = jnp.sum(k_ref[...] * q, axis=-1, keepdims=True) * XA_SCALE
    e = jnp.exp(s - jnp.max(s, axis=1, keepdims=True))
    p = e / jnp.sum(e, axis=1, keepdims=True)
    o_ref[...] = jnp.sum(p * v_ref[...], axis=1, keepdims=True)


def _xa_step(q4, cache_k, cache_v, tb):
    m = q4.shape[0]
    kv_spec = pl.BlockSpec((tb, N_MEM, XA_HEADS, XA_HEAD_DIM), lambda i: (i, 0, 0, 0))
    q_spec = pl.BlockSpec((tb, 1, XA_HEADS, XA_HEAD_DIM), lambda i: (i, 0, 0, 0))
    return pl.pallas_call(
        _xa_step_kernel,
        grid=(m // tb,),
        in_specs=[q_spec, kv_spec, kv_spec],
        out_specs=q_spec,
        out_shape=jax.ShapeDtypeStruct(q4.shape, F32),
        compiler_params=_params("parallel"),
        name="xa_step",
    )(q4, cache_k, cache_v)


def _gate_kernel(z_ref, g_ref, o_ref, ga_ref, gb_ref, gc_ref, wpo_ref, wv_ref, wg_ref, wxo_ref, m_ref, *copy_refs):
    copies = copy_refs if copy_refs else (None,) * 4
    g = g_ref[...]
    o_pool = _dot(z_ref[...], _weight(wpo_ref, copies[0]))
    o_ssm = _dot(g, _weight(wv_ref, copies[1])) * jax.nn.sigmoid(_dot(g, _weight(wg_ref, copies[2])))
    o_xa = _dot(o_ref[...], _weight(wxo_ref, copies[3]))
    merged = (jax.nn.sigmoid(ga_ref[...]) * o_pool + jax.nn.sigmoid(gb_ref[...]) * o_ssm
              + jax.nn.sigmoid(gc_ref[...]) * o_xa)
    m_ref[...] = merged.astype(BF16)


def _gate(z, g, o, proj, weights, tm, tn):
    m, half = z.shape
    emit_w = weights[0].dtype != BF16
    assert not emit_w or m == tm
    d = weights[0].shape[1] if emit_w else weights[0].shape[0] * tn
    gate0 = OFF_GATE // tn
    per_gate = d // tn
    branch_spec = pl.BlockSpec((tm, half), lambda i, n: (i, 0))
    w_specs, copy_specs, copy_shapes = map(list, zip(*[_col_tiled(w, tn) for w in weights]))
    if not emit_w:
        copy_specs, copy_shapes = [], []

    def gate_spec(k):
        return pl.BlockSpec((tm, tn), lambda i, n: (i, gate0 + k * per_gate + n))

    outs = pl.pallas_call(
        _gate_kernel,
        grid=(m // tm, d // tn),
        in_specs=[branch_spec] * 3 + [gate_spec(0), gate_spec(1), gate_spec(2)] + w_specs,
        out_specs=[pl.BlockSpec((tm, tn), lambda i, n: (i, n))] + copy_specs,
        out_shape=[jax.ShapeDtypeStruct((m, d), BF16)] + copy_shapes,
        compiler_params=_params("parallel", "arbitrary"),
        name="gate",
    )(z, g, o, proj, proj, proj, *weights)
    return outs[0], (tuple(outs[1:]) if emit_w else tuple(weights))


def _mixout_kernel(m_ref, wout_ref, h_ref, gpost_ref, gnext_ref, h2_ref, xn_ref, *copy_ref, sub):
    wout = _weight(wout_ref, *copy_ref)
    for r in range(m_ref.shape[0] // sub):
        rows = slice(r * sub, (r + 1) * sub)
        h2 = h_ref[rows, :] + _rms(_dot(m_ref[rows, :], wout), gpost_ref[...])
        h2_ref[rows, :] = h2
        xn_ref[rows, :] = _rms(h2, gnext_ref[...]).astype(BF16)


def _mixout(merged, w_out, h, g_post, g_next, tm):
    m, d = h.shape
    emit_w = w_out.dtype != BF16
    assert not emit_w or m == tm
    row_spec = pl.BlockSpec((tm, d), lambda i: (i, 0))
    gain_spec = pl.BlockSpec((1, d), lambda i: (0, 0))
    w_spec = pl.BlockSpec((d, d), lambda i: (0, 0))
    copy_specs, copy_shapes = _with_copies([w_spec], [w_out], emit_w)
    outs = pl.pallas_call(
        functools.partial(_mixout_kernel, sub=min(tm, MXU_DIM)),
        grid=(m // tm,),
        in_specs=[row_spec, w_spec, row_spec, gain_spec, gain_spec],
        out_specs=[row_spec, row_spec] + copy_specs,
        out_shape=[jax.ShapeDtypeStruct((m, d), F32), jax.ShapeDtypeStruct((m, d), BF16)] + copy_shapes,
        compiler_params=_params("parallel"),
        name="mixout",
    )(merged, w_out, h, g_post, g_next)
    return outs[:2], (outs[2] if emit_w else w_out)


def _block_diag_in(bb):
    gpc = SSM_GROUPS // SSM_CHUNKS
    x = bb.reshape(SSM_CHUNKS, gpc, SSM_STATE, SSM_GROUP_CH).transpose(0, 1, 3, 2)
    eye = jnp.eye(gpc, dtype=bb.dtype)
    y = x[:, :, :, None, :] * eye[None, :, None, :, None]
    return y.reshape(SSM_CHUNKS, gpc * SSM_GROUP_CH, gpc * SSM_STATE)


def _block_diag_out(c):
    gpc = SSM_GROUPS // SSM_CHUNKS
    x = c.reshape(SSM_CHUNKS, gpc, SSM_GROUP_CH, SSM_STATE).transpose(0, 1, 3, 2)
    eye = jnp.eye(gpc, dtype=c.dtype)
    y = x[:, :, :, None, :] * eye[None, :, None, :, None]
    return y.reshape(SSM_CHUNKS, gpc * SSM_STATE, gpc * SSM_GROUP_CH)


def _scan_table(pw_re, pw_im, ps_re, ps_im):
    flat = lambda p: p.reshape(SUBLANES, SSM_LANES)
    pw_re, pw_im, ps_re, ps_im = flat(pw_re), flat(pw_im), flat(ps_re), flat(ps_im)
    row = jnp.arange(SUBLANES)[:, None]
    tiles = [jnp.broadcast_to(pw_re[0], (SUBLANES, SSM_LANES)), jnp.broadcast_to(pw_im[0], (SUBLANES, SSM_LANES))]
    for d in (1, 2, 4):
        keep = row >= d
        tiles.append(jnp.where(keep, ps_re[d - 1][None, :], 0.0))
        tiles.append(jnp.where(keep, ps_im[d - 1][None, :], 0.0))
    tiles += [ps_re, ps_im]
    return jnp.stack(tiles)


def _row_tile(m, target):
    return target if m % target == 0 else m


FFN1_W = ('w_ff1_gate', 'w_ff1_up', 'w_ff1_down')
FFN2_W = ('w_ff2_gate', 'w_ff2_up', 'w_ff2_down')
GATE_W = ('w_pool_out', 'w_glu_val', 'w_glu_gate', 'w_xa_out')
TILES = dict(tm=512, tf=512, tm_proj=2048, tm_gate=1024, tn=512)


def _layer(x, w, mixers):
    m = x.shape[0]
    tm = _row_tile(m, TILES['tm'])
    wb = dict(w)
    (h, xn), wb_ffn1 = _ffn(x, w['g_ff1_pre'], [w[k] for k in FFN1_W], w['g_ff1_post'], tm, TILES['tf'],
                            g_next=w['g_mix_pre'])
    proj, wb['w_in'] = _proj(xn, w['w_in'], _row_tile(m, TILES['tm_proj']), TILES['tn'])
    z, g, o, states = mixers(proj)
    merged, wb_gate = _gate(z, g, o, proj, [w[k] for k in GATE_W], _row_tile(m, TILES['tm_gate']), TILES['tn'])
    (h, xn), wb['w_out'] = _mixout(merged, w['w_out'], h, w['g_mix_post'], w['g_ff2_pre'], tm)
    (y,), wb_ffn2 = _ffn(h, xn, [w[k] for k in FFN2_W], w['g_ff2_post'], tm, TILES['tf'])
    wb.update(zip(FFN1_W + GATE_W + FFN2_W, wb_ffn1 + wb_gate + wb_ffn2))
    return y, states, wb


def kernel(x_prompt, x_sample, mem_prompt, cache_mem_k, cache_mem_v, state_pool, state_ssm_re, state_ssm_im, g_ff1_pre, w_ff1_gate, w_ff1_up, w_ff1_down, g_ff1_post, g_mix_pre, w_in, w_pool_grp, pool_scale, w_pool_out, ssm_a_re, ssm_a_im, ssm_log_step, ssm_b_re, ssm_b_im, ssm_c_re, ssm_c_im, ssm_d, w_glu_val, w_glu_gate, g_mem, w_mem_k, w_mem_v, w_xa_out, w_out, g_mix_post, g_ff2_pre, w_ff2_gate, w_ff2_up, w_ff2_down, g_ff2_post):
    bp, lp, d = x_prompt.shape
    bs, ls, _ = x_sample.shape
    depth = w_in.shape[0]
    assert depth == 1 and ls == 1 and lp >= POOL_BUF and lp % SSM_TIME_TILE == 0

    mats = dict(w_ff1_gate=w_ff1_gate, w_ff1_up=w_ff1_up, w_ff1_down=w_ff1_down, w_in=w_in,
                w_pool_out=w_pool_out, w_glu_val=w_glu_val, w_glu_gate=w_glu_gate, w_xa_out=w_xa_out,
                w_out=w_out, w_ff2_gate=w_ff2_gate, w_ff2_up=w_ff2_up, w_ff2_down=w_ff2_down)
    w = {k: v[0] for k, v in mats.items()}
    w.update(g_ff1_pre=g_ff1_pre, g_ff1_post=g_ff1_post, g_mix_pre=g_mix_pre, g_mix_post=g_mix_post,
             g_ff2_pre=g_ff2_pre, g_ff2_post=g_ff2_post)
    w_grp = w_pool_grp[0].astype(BF16)
    scale = pool_scale
    d_skip = ssm_d

    bb_re, bb_im, pw_re, pw_im, ps_re, ps_im = _ssm_disc(
        ssm_a_re[0], ssm_a_im[0], ssm_log_step[0], ssm_b_re[0], ssm_b_im[0])
    bdr = _block_diag_in(bb_re.astype(BF16))
    bdi = _block_diag_in(bb_im.astype(BF16))
    cdr = _block_diag_out(ssm_c_re[0].astype(BF16))
    cdi = _block_diag_out(ssm_c_im[0].astype(BF16))
    tab = _scan_table(pw_re, pw_im, ps_re, ps_im)
    lb = jnp.stack([pw_re[0].reshape(SSM_LANES), pw_im[0].reshape(SSM_LANES)])

    def sample_mixers(proj):
        z, new_buf = _pool_step(proj, state_pool[0].reshape(bs, POOL_BUF * POOL_WIDTH), w_grp, scale, 32)
        g, x_re, x_im = _ssm_step(proj, state_ssm_re[0].reshape(bs, SSM_LANES),
                                  state_ssm_im[0].reshape(bs, SSM_LANES), lb, bdr, bdi, cdr, cdi, d_skip)
        q4 = proj[:, POOL_WIDTH + SSM_WIDTH:OFF_GATE].reshape(bs, 1, XA_HEADS, XA_HEAD_DIM)
        o = _xa_step(q4, cache_mem_k[0], cache_mem_v[0], 4)
        return z, g, o.reshape(bs, XA_WIDTH).astype(BF16), (new_buf, x_re, x_im)

    y_sample, (pool_s, re_s, im_s), wb = _layer(x_sample.reshape(bs * ls, d), w, sample_mixers)

    mem = mem_prompt.reshape(bp * N_MEM, d)
    mem_k = _normproj(mem, g_mem, w_mem_k[0], bp * N_MEM, 512)
    mem_v = _normproj(mem, g_mem, w_mem_v[0], bp * N_MEM, 512)

    def prompt_mixers(proj):
        z = _pool_seq(proj, w_grp, scale, bp, lp, 512)
        zeros = jnp.zeros((bp, 1, SSM_LANES), F32)
        g, last_re, last_im = _ssm_seq(proj, zeros, zeros, tab, bdr, bdi, cdr, cdi, d_skip, bp, lp)
        o = _xa_seq(proj, mem_k, mem_v, bp, lp, 512)
        pool_rows = proj.reshape(bp, lp, IN_WIDTH)[:, lp - POOL_BUF:, :POOL_WIDTH]
        return z, g, o, (pool_rows, last_re, last_im)

    y_prompt, (pool_p, re_p, im_p), _ = _layer(x_prompt.reshape(bp * lp, d), wb, prompt_mixers)

    kv_shape = (1, bp, N_MEM, XA_HEADS, XA_HEAD_DIM)
    st_shape = (SSM_GROUPS, SSM_STATE)
    return (y_prompt.reshape(bp, lp, d), y_sample.reshape(bs, ls, d),
            mem_k.reshape(kv_shape), mem_v.reshape(kv_shape),
            pool_p[None],
            re_p.reshape((1, bp) + st_shape), im_p.reshape((1, bp) + st_shape),
            pool_s.reshape(1, bs, POOL_BUF, POOL_WIDTH),
            re_s.reshape((1, bs) + st_shape), im_s.reshape((1, bs) + st_shape))
```

```python
import functools
import math

import jax
import jax.numpy as jnp
from jax import lax
from jax.experimental import pallas as pl
from jax.experimental.pallas import tpu as pltpu

F32 = jnp.float32
BF16 = jnp.bfloat16

D_MODEL = 2048
PAST_LEN = 16384
POOL_WIDTH = D_MODEL // 2
POOL_WINDOWS = (2, 4, 8, 16)
POOL_GROUP_DIM = POOL_WIDTH // len(POOL_WINDOWS)
POOL_BUF = max(POOL_WINDOWS) - 1
SSM_WIDTH = D_MODEL // 2
SSM_GROUP_CH = 16
SSM_GROUPS = SSM_WIDTH // SSM_GROUP_CH
SSM_STATE = 64
SSM_LANES = SSM_GROUPS * SSM_STATE
XA_HEADS = 4
XA_HEAD_DIM = D_MODEL // 8
XA_WIDTH = XA_HEADS * XA_HEAD_DIM
XA_SCALE = XA_HEAD_DIM ** -0.5
N_MEM = 256
OFF_GATE = POOL_WIDTH + SSM_WIDTH + XA_WIDTH
IN_WIDTH = OFF_GATE + 3 * D_MODEL
RMS_EPS = 1e-6

SUBLANES = 8
LANES = 128
MXU_DIM = 256
VMEM_LIMIT_BYTES = 60 * 1024 * 1024
DOWN_CHUNKS = 4
NORM_ROWS = 2 * SUBLANES

SSM_CHUNKS = SSM_WIDTH // MXU_DIM
SSM_CHUNK_LANES = SSM_LANES // SSM_CHUNKS
SSM_TIME_TILE = 256
SSM_SEG = SSM_TIME_TILE // SUBLANES
SCAN_LANES = 4 * LANES


def _params(*sem):
    return pltpu.CompilerParams(dimension_semantics=sem, vmem_limit_bytes=VMEM_LIMIT_BYTES)


def _rms(x, g, half=False):
    scale = lax.rsqrt(jnp.mean(x * x, axis=-1, keepdims=True) + RMS_EPS)
    return (x * (0.5 * scale if half else scale)) * g


def _dot(a, b):
    return jnp.dot(a, b, preferred_element_type=F32)


def _cmul(a_re, a_im, b_re, b_im):
    return a_re * b_re - a_im * b_im, a_re * b_im + a_im * b_re


def _weight(w_ref, copy_ref=None):
    w = w_ref[...]
    if w.dtype != BF16:
        w = w.astype(BF16)
    if copy_ref is not None:
        copy_ref[...] = w
    return w


def _with_copies(specs, weights, emit_w):
    if not emit_w:
        return [], []
    return list(specs), [jax.ShapeDtypeStruct(w.shape, BF16) for w in weights]


def _col_tiled(w, tn):
    tiled = lambda k: pl.BlockSpec((None, k, tn), lambda i, j: (j, 0, 0))
    if w.dtype == BF16:
        assert w.ndim == 3 and w.shape[2] == tn, "bf16 weights come tile-major from an earlier call"
        return tiled(w.shape[1]), None, None
    k, n = w.shape
    return pl.BlockSpec((k, tn), lambda i, j: (0, j)), tiled(k), jax.ShapeDtypeStruct((n // tn, k, tn), BF16)


def _ffn_kernel(*refs, first, emit_w):
    refs = list(refs)
    x_ref, pre_ref, wg_ref, wu_ref, wd_ref, gpost_ref = refs[:6]
    del refs[:6]
    gnext_ref = refs.pop(0) if first else None
    o_ref = refs.pop(0)
    xnext_ref = refs.pop(0) if first else None
    wgb_ref, wub_ref, wdb_ref = (refs.pop(0), refs.pop(0), refs.pop(0)) if emit_w else (None, None, None)
    xn_ref = refs.pop(0) if first else pre_ref
    j = pl.program_id(1)

    @pl.when(j == 0)
    def _():
        if first:
            xn_ref[...] = _rms(x_ref[...], pre_ref[...]).astype(BF16)
        o_ref[...] = jnp.zeros_like(o_ref)

    xn = xn_ref[...]
    g = _dot(xn, _weight(wg_ref, wgb_ref))
    u = _dot(xn, _weight(wu_ref, wub_ref))
    hid = ((g * jax.nn.sigmoid(g)) * u).astype(BF16)
    wd = _weight(wd_ref, wdb_ref)
    cols = wd.shape[1] // DOWN_CHUNKS
    for n in range(DOWN_CHUNKS):
        sl = slice(n * cols, (n + 1) * cols)
        o_ref[:, sl] += _dot(hid, wd[:, sl])

    @pl.when(j == pl.num_programs(1) - 1)
    def _():
        if first:
            out = x_ref[...] + _rms(o_ref[...], gpost_ref[...], half=True)
            o_ref[...] = out
            xnext_ref[...] = _rms(out, gnext_ref[...]).astype(BF16)
        else:
            for r in range(o_ref.shape[0] // NORM_ROWS):
                rs = slice(r * NORM_ROWS, (r + 1) * NORM_ROWS)
                o_ref[rs, :] = x_ref[rs, :] + _rms(o_ref[rs, :], gpost_ref[...], half=True)


def _ffn(x, pre, weights, g_post, tm, tf, g_next=None):
    m, d = x.shape
    w_gate, w_up, w_down = weights
    f = w_down.shape[0]
    first = g_next is not None
    emit_w = w_gate.dtype != BF16
    assert not emit_w or m == tm, "weight copies need every weight block visited exactly once"
    row_spec = pl.BlockSpec((tm, d), lambda i, j: (i, 0))
    gain_spec = pl.BlockSpec((1, d), lambda i, j: (0, 0))
    wg_spec, wg_copy, wg_shape = _col_tiled(w_gate, tf)
    wu_spec, wu_copy, wu_shape = _col_tiled(w_up, tf)
    wd_spec = pl.BlockSpec((tf, d), lambda i, j: (j, 0))
    w_specs = [wg_spec, wu_spec, wd_spec]
    copy_specs = [wg_copy, wu_copy, wd_spec] if emit_w else []
    copy_shapes = [wg_shape, wu_shape, jax.ShapeDtypeStruct(w_down.shape, BF16)] if emit_w else []
    outs = pl.pallas_call(
        functools.partial(_ffn_kernel, first=first, emit_w=emit_w),
        grid=(m // tm, f // tf),
        in_specs=[row_spec, gain_spec if first else row_spec] + w_specs + [gain_spec]
        + ([gain_spec] if first else []),
        out_specs=[row_spec] + ([row_spec] if first else []) + copy_specs,
        out_shape=[jax.ShapeDtypeStruct((m, d), F32)] + ([jax.ShapeDtypeStruct((m, d), BF16)] if first else [])
        + copy_shapes,
        scratch_shapes=[pltpu.VMEM((tm, d), BF16)] if first else [],
        compiler_params=_params("parallel", "arbitrary"),
        name="ffn",
    )(*((x, pre, w_gate, w_up, w_down, g_post) + ((g_next,) if first else ())))
    n_out = 2 if first else 1
    return outs[:n_out], (tuple(outs[n_out:]) if emit_w else tuple(weights))


def _proj_kernel(xn_ref, w_ref, o_ref, *copy_ref):
    o_ref[...] = _dot(xn_ref[...], _weight(w_ref, *copy_ref))


def _proj(xn, w, tm, tn):
    m, d = xn.shape
    emit_w = w.dtype != BF16
    assert not emit_w or m == tm
    n = w.shape[1] if emit_w else w.shape[0] * tn
    w_spec, copy_spec, copy_shape = _col_tiled(w, tn)
    copy_specs, copy_shapes = ([copy_spec], [copy_shape]) if emit_w else ([], [])
    outs = pl.pallas_call(
        _proj_kernel,
        grid=(m // tm, n // tn),
        in_specs=[pl.BlockSpec((tm, d), lambda i, j: (i, 0)), w_spec],
        out_specs=[pl.BlockSpec((tm, tn), lambda i, j: (i, j))] + copy_specs,
        out_shape=[jax.ShapeDtypeStruct((m, n), F32)] + copy_shapes,
        compiler_params=_params("parallel", "arbitrary"),
        name="proj",
    )(xn, w)
    return outs[0], (outs[1] if emit_w else w)


def _normproj_kernel(x_ref, g_ref, w_ref, o_ref, xn_ref):
    @pl.when(pl.program_id(1) == 0)
    def _():
        xn_ref[...] = _rms(x_ref[...], g_ref[...]).astype(BF16)

    o_ref[...] = _dot(xn_ref[...], _weight(w_ref))


def _normproj(x, g, w, tm, tn):
    m, d = x.shape
    n = w.shape[1]
    return pl.pallas_call(
        _normproj_kernel,
        grid=(m // tm, n // tn),
        in_specs=[
            pl.BlockSpec((tm, d), lambda i, j: (i, 0)),
            pl.BlockSpec((1, d), lambda i, j: (0, 0)),
            pl.BlockSpec((d, tn), lambda i, j: (0, j)),
        ],
        out_specs=pl.BlockSpec((tm, tn), lambda i, j: (i, j)),
        out_shape=jax.ShapeDtypeStruct((m, n), F32),
        scratch_shapes=[pltpu.VMEM((tm, d), BF16)],
        compiler_params=_params("parallel", "arbitrary"),
        name="normproj",
    )(x, g, w)


def _pool_group_out(k, diff, wgrp_ref, scale_ref, z_ref):
    ch = slice(k * POOL_GROUP_DIM, (k + 1) * POOL_GROUP_DIM)
    zk = _dot(diff.astype(BF16), wgrp_ref[k]) * scale_ref[:, ch]
    z_ref[:, ch] = zk.astype(BF16)


def _pool_seq_kernel(u_ref, wgrp_ref, scale_ref, z_ref, ext_ref, *, halo):
    t = pl.program_id(1)
    tt = u_ref.shape[0]

    @pl.when(t == 0)
    def _():
        ext_ref[0:halo, :] = jnp.zeros((halo, POOL_WIDTH), F32)

    u = u_ref[...]
    ext_ref[halo:halo + tt, :] = u
    pos = t * tt + lax.broadcasted_iota(jnp.int32, (tt, 1), 0)
    for k, w in enumerate(POOL_WINDOWS):
        ch = slice(k * POOL_GROUP_DIM, (k + 1) * POOL_GROUP_DIM)
        a = ext_ref[:, ch]
        d = 1
        while d < w:
            a = a + pltpu.roll(a, d, axis=0)
            d *= 2
        count = jnp.minimum(pos + 1, w).astype(F32)
        diff = a[halo:, :] / count - u[:, ch]
        _pool_group_out(k, diff, wgrp_ref, scale_ref, z_ref)
    ext_ref[0:halo, :] = ext_ref[tt:tt + halo, :]


def _pool_seq(proj, w_grp, scale, batch, seq, tt):
    halo = 2 * SUBLANES
    nt = seq // tt
    return pl.pallas_call(
        functools.partial(_pool_seq_kernel, halo=halo),
        grid=(batch, nt),
        in_specs=[
            pl.BlockSpec((tt, POOL_WIDTH), lambda b, t: (b * nt + t, 0)),
            pl.BlockSpec(w_grp.shape, lambda b, t: (0, 0, 0)),
            pl.BlockSpec((1, POOL_WIDTH), lambda b, t: (0, 0)),
        ],
        out_specs=pl.BlockSpec((tt, POOL_WIDTH), lambda b, t: (b * nt + t, 0)),
        out_shape=jax.ShapeDtypeStruct((batch * seq, POOL_WIDTH), BF16),
        scratch_shapes=[pltpu.VMEM((tt + halo, POOL_WIDTH), F32)],
        compiler_params=_params("parallel", "arbitrary"),
        name="pool_seq",
    )(proj, w_grp, scale)


def _pool_step_kernel(u_ref, buf_ref, wgrp_ref, scale_ref, z_ref, nbuf_ref):
    u = u_ref[...]
    for k, w in enumerate(POOL_WINDOWS):
        ch = slice(k * POOL_GROUP_DIM, (k + 1) * POOL_GROUP_DIM)
        s = u[:, ch]
        for j in range(1, w):
            row = POOL_BUF - j
            s = s + buf_ref[:, row * POOL_WIDTH + k * POOL_GROUP_DIM:
                            row * POOL_WIDTH + (k + 1) * POOL_GROUP_DIM]
        diff = s / float(min(PAST_LEN + 1, w)) - u[:, ch]
        _pool_group_out(k, diff, wgrp_ref, scale_ref, z_ref)
    keep = (POOL_BUF - 1) * POOL_WIDTH
    nbuf_ref[:, 0:keep] = buf_ref[:, POOL_WIDTH:]
    nbuf_ref[:, keep:] = u


def _pool_step(proj, buf, w_grp, scale, tb):
    m = proj.shape[0]
    return pl.pallas_call(
        _pool_step_kernel,
        grid=(m // tb,),
        in_specs=[
            pl.BlockSpec((tb, POOL_WIDTH), lambda i: (i, 0)),
            pl.BlockSpec((tb, POOL_BUF * POOL_WIDTH), lambda i: (i, 0)),
            pl.BlockSpec(w_grp.shape, lambda i: (0, 0, 0)),
            pl.BlockSpec((1, POOL_WIDTH), lambda i: (0, 0)),
        ],
        out_specs=[
            pl.BlockSpec((tb, POOL_WIDTH), lambda i: (i, 0)),
            pl.BlockSpec((tb, POOL_BUF * POOL_WIDTH), lambda i: (i, 0)),
        ],
        out_shape=[
            jax.ShapeDtypeStruct((m, POOL_WIDTH), BF16),
            jax.ShapeDtypeStruct((m, POOL_BUF * POOL_WIDTH), F32),
        ],
        compiler_params=_params("parallel"),
        name="pool_step",
    )(proj, buf, w_grp, scale)


def _zoh_lambda(a_re, a_im, log_step):
    dt = jnp.exp(log_step)
    mag = jnp.exp(a_re * dt)
    ang = a_im * dt
    return mag * jnp.cos(ang), mag * jnp.sin(ang)


def _ssm_disc_kernel(a_re_ref, a_im_ref, ls_ref, a_re_col_ref, a_im_col_ref, ls_col_ref, b_re_ref, b_im_ref,
                     bb_re_ref, bb_im_ref, pw_re_ref, pw_im_ref, ps_re_ref, ps_im_ref):
    a_re = a_re_col_ref[...]
    a_im = a_im_col_ref[...]
    lb_re, lb_im = _zoh_lambda(a_re, a_im, ls_col_ref[...])
    den = a_re * a_re + a_im * a_im
    n_re = lb_re - 1.0
    f_re = (n_re * a_re + lb_im * a_im) / den
    f_im = (lb_im * a_re - n_re * a_im) / den
    b_re = b_re_ref[...]
    b_im = b_im_ref[...]
    bb_re_ref[...] = f_re * b_re - f_im * b_im
    bb_im_ref[...] = f_re * b_im + f_im * b_re

    lb_re, lb_im = _zoh_lambda(a_re_ref[...], a_im_ref[...], ls_ref[...])

    def powers(base_re, base_im, out_re_ref, out_im_ref):
        p_re, p_im = base_re, base_im
        out_re_ref[0] = p_re
        out_im_ref[0] = p_im
        for r in range(1, SUBLANES):
            p_re, p_im = _cmul(p_re, p_im, base_re, base_im)
            out_re_ref[r] = p_re
            out_im_ref[r] = p_im
        return p_re, p_im

    s_re, s_im = powers(lb_re, lb_im, pw_re_ref, pw_im_ref)
    for _ in range(int(math.log2(SSM_SEG // SUBLANES))):
        s_re, s_im = _cmul(s_re, s_im, s_re, s_im)
    powers(s_re, s_im, ps_re_ref, ps_im_ref)


def _ssm_disc(a_re, a_im, log_step, b_re, b_im):
    g, n, h = b_re.shape
    col = (g * n, 1)
    pw = jax.ShapeDtypeStruct((SUBLANES, g, n), F32)
    bb = jax.ShapeDtypeStruct((g * n, h), F32)
    bb_re, bb_im, pw_re, pw_im, ps_re, ps_im = pl.pallas_call(
        _ssm_disc_kernel,
        out_shape=[bb, bb, pw, pw, pw, pw],
        name="ssm_disc",
    )(a_re, a_im, log_step[:, None],
      a_re.reshape(col), a_im.reshape(col), jnp.broadcast_to(log_step[:, None], (g, n)).reshape(col),
      b_re.reshape(g * n, h), b_im.reshape(g * n, h))
    return bb_re.reshape(g, n, h), bb_im.reshape(g, n, h), pw_re, pw_im, ps_re, ps_im


def _ssm_in(ub, bd_ref, x_ref):
    for c in range(SSM_CHUNKS):
        x_ref[:, c * SSM_CHUNK_LANES:(c + 1) * SSM_CHUNK_LANES] = _dot(
            ub[:, c * MXU_DIM:(c + 1) * MXU_DIM], bd_ref[c])


def _ssm_out(c, u, x_re_ref, x_im_ref, cdr_ref, cdi_ref, d_ref):
    ch = slice(c * MXU_DIM, (c + 1) * MXU_DIM)
    st = slice(c * SSM_CHUNK_LANES, (c + 1) * SSM_CHUNK_LANES)
    y = _dot(x_re_ref[:, st].astype(BF16), cdr_ref[c]) - _dot(x_im_ref[:, st].astype(BF16), cdi_ref[c])
    return jax.nn.gelu(y + d_ref[:, ch] * u[:, ch])


def _ssm_seq_kernel(u_ref, h_re_ref, h_im_ref, tab_ref, bdr_ref, bdi_ref, cdr_ref, cdi_ref, d_ref,
                    g_ref, last_re_ref, last_im_ref, xr_ref, xi_ref, cr_ref, ci_ref):
    t = pl.program_id(1)
    tt = u_ref.shape[0]
    seg = tt // SUBLANES

    @pl.when(t == 0)
    def _():
        cr_ref[...] = jnp.broadcast_to(h_re_ref[...], cr_ref.shape)
        ci_ref[...] = jnp.broadcast_to(h_im_ref[...], ci_ref.shape)

    def regroup(x, rows_major, rows_minor):
        w = x.shape[1]
        return jnp.swapaxes(x.reshape(rows_major, rows_minor, w), 0, 1).reshape(tt, w)

    u = regroup(u_ref[...], SUBLANES, seg)
    ub = u.astype(BF16)
    _ssm_in(ub, bdr_ref, xr_ref)
    _ssm_in(ub, bdi_ref, xi_ref)

    first_row = lax.broadcasted_iota(jnp.int32, (SUBLANES, SCAN_LANES), 0) == 0
    for lc in range(SSM_LANES // SCAN_LANES):
        sl = slice(lc * SCAN_LANES, (lc + 1) * SCAN_LANES)
        l_re = tab_ref[0, :, sl]
        l_im = tab_ref[1, :, sl]

        def rows(j):
            return pl.ds(j * SUBLANES, SUBLANES)

        def advance(j, x, sl=sl, l_re=l_re, l_im=l_im):
            x_re, x_im = _cmul(l_re, l_im, *x)
            x_re = x_re + xr_ref[rows(j), sl]
            x_im = x_im + xi_ref[rows(j), sl]
            xr_ref[rows(j), sl] = x_re
            xi_ref[rows(j), sl] = x_im
            return x_re, x_im

        zero = jnp.zeros((SUBLANES, SCAN_LANES), F32)
        f_re, f_im = zero, zero
        for j in range(seg):
            f_re, f_im = advance(j, (f_re, f_im))

        c_re = cr_ref[:, sl]
        c_im = ci_ref[:, sl]
        for i in range(3):
            s_re = pltpu.roll(f_re, 1 << i, axis=0)
            s_im = pltpu.roll(f_im, 1 << i, axis=0)
            m_re, m_im = _cmul(tab_ref[2 + 2 * i, :, sl], tab_ref[3 + 2 * i, :, sl], s_re, s_im)
            f_re, f_im = f_re + m_re, f_im + m_im
        m_re, m_im = _cmul(tab_ref[8, :, sl], tab_ref[9, :, sl], c_re, c_im)
        f_re, f_im = f_re + m_re, f_im + m_im
        e_re = jnp.where(first_row, c_re, pltpu.roll(f_re, 1, axis=0))
        e_im = jnp.where(first_row, c_im, pltpu.roll(f_im, 1, axis=0))
        cr_ref[:, sl] = jnp.broadcast_to(f_re[SUBLANES - 1:SUBLANES, :], (SUBLANES, SCAN_LANES))
        ci_ref[:, sl] = jnp.broadcast_to(f_im[SUBLANES - 1:SUBLANES, :], (SUBLANES, SCAN_LANES))

        def fix(j, w, sl=sl, l_re=l_re, l_im=l_im):
            w_re, w_im = _cmul(l_re, l_im, *w)
            xr_ref[rows(j), sl] += w_re
            xi_ref[rows(j), sl] += w_im
            return w_re, w_im

        w = (e_re, e_im)
        for j in range(seg):
            w = fix(j, w)

    for c in range(SSM_CHUNKS):
        ch = slice(c * MXU_DIM, (c + 1) * MXU_DIM)
        g = _ssm_out(c, u, xr_ref, xi_ref, cdr_ref, cdi_ref, d_ref)
        g_ref[:, ch] = regroup(g, seg, SUBLANES).astype(BF16)

    @pl.when(t == pl.num_programs(1) - 1)
    def _():
        last_re_ref[...] = cr_ref[0:1, :]
        last_im_ref[...] = ci_ref[0:1, :]


def _const_spec(shape):
    return pl.BlockSpec(shape, lambda *_: (0,) * len(shape))


def _ssm_seq(proj, h_re, h_im, tab, bdr, bdi, cdr, cdi, d_skip, batch, seq):
    tt = SSM_TIME_TILE
    nt = seq // tt
    col_block = POOL_WIDTH // SSM_WIDTH
    state_spec = pl.BlockSpec((None, 1, SSM_LANES), lambda b, t: (b, 0, 0))
    state_shape = jax.ShapeDtypeStruct((batch, 1, SSM_LANES), F32)
    return pl.pallas_call(
        _ssm_seq_kernel,
        grid=(batch, nt),
        in_specs=[
            pl.BlockSpec((tt, SSM_WIDTH), lambda b, t: (b * nt + t, col_block)),
            state_spec, state_spec,
            _const_spec(tab.shape),
            _const_spec(bdr.shape), _const_spec(bdi.shape),
            _const_spec(cdr.shape), _const_spec(cdi.shape),
            _const_spec(d_skip.shape),
        ],
        out_specs=[pl.BlockSpec((tt, SSM_WIDTH), lambda b, t: (b * nt + t, 0)), state_spec, state_spec],
        out_shape=[jax.ShapeDtypeStruct((batch * seq, SSM_WIDTH), BF16), state_shape, state_shape],
        scratch_shapes=[
            pltpu.VMEM((tt, SSM_LANES), F32), pltpu.VMEM((tt, SSM_LANES), F32),
            pltpu.VMEM((SUBLANES, SSM_LANES), F32), pltpu.VMEM((SUBLANES, SSM_LANES), F32),
        ],
        compiler_params=_params("parallel", "arbitrary"),
        name="ssm_seq",
    )(proj, h_re, h_im, tab, bdr, bdi, cdr, cdi, d_skip)


def _ssm_step_kernel(u_ref, h_re_ref, h_im_ref, lb_ref, bdr_ref, bdi_ref, cdr_ref, cdi_ref, d_ref,
                     g_ref, x_re_ref, x_im_ref):
    u = u_ref[...]
    ub = u.astype(BF16)
    _ssm_in(ub, bdr_ref, x_re_ref)
    _ssm_in(ub, bdi_ref, x_im_ref)
    m_re, m_im = _cmul(lb_ref[0:1, :], lb_ref[1:2, :], h_re_ref[...], h_im_ref[...])
    x_re_ref[...] += m_re
    x_im_ref[...] += m_im
    for c in range(SSM_CHUNKS):
        ch = slice(c * MXU_DIM, (c + 1) * MXU_DIM)
        g_ref[:, ch] = _ssm_out(c, u, x_re_ref, x_im_ref, cdr_ref, cdi_ref, d_ref).astype(BF16)


def _ssm_step(proj, h_re, h_im, lb, bdr, bdi, cdr, cdi, d_skip):
    m = proj.shape[0]
    state_spec = pl.BlockSpec((m, SSM_LANES), lambda i: (0, 0))
    state_shape = jax.ShapeDtypeStruct((m, SSM_LANES), F32)
    return pl.pallas_call(
        _ssm_step_kernel,
        grid=(1,),
        in_specs=[
            pl.BlockSpec((m, SSM_WIDTH), lambda i: (0, POOL_WIDTH // SSM_WIDTH)),
            state_spec, state_spec,
            _const_spec(lb.shape),
            _const_spec(bdr.shape), _const_spec(bdi.shape),
            _const_spec(cdr.shape), _const_spec(cdi.shape),
            _const_spec(d_skip.shape),
        ],
        out_specs=[pl.BlockSpec((m, SSM_WIDTH), lambda i: (0, 0)), state_spec, state_spec],
        out_shape=[jax.ShapeDtypeStruct((m, SSM_WIDTH), BF16), state_shape, state_shape],
        compiler_params=_params("arbitrary"),
        name="ssm_step",
    )(proj, h_re, h_im, lb, bdr, bdi, cdr, cdi, d_skip)


def _xa_seq_kernel(q_ref, k_ref, v_ref, o_ref):
    for h in range(XA_HEADS):
        ch = slice(h * XA_HEAD_DIM, (h + 1) * XA_HEAD_DIM)
        q = q_ref[:, ch].astype(BF16)
        k = k_ref[:, ch].astype(BF16)
        v = v_ref[:, ch].astype(BF16)
        s = lax.dot_general(q, k, (((1,), (1,)), ((), ())), preferred_element_type=F32) * XA_SCALE
        e = jnp.exp(s - jnp.max(s, axis=-1, keepdims=True))
        p = e / jnp.sum(e, axis=-1, keepdims=True)
        o_ref[:, ch] = _dot(p.astype(BF16), v).astype(BF16)


def _xa_seq(proj, mem_k, mem_v, batch, seq, tq):
    nt = seq // tq
    q_block = (POOL_WIDTH + SSM_WIDTH) // XA_WIDTH
    kv_spec = pl.BlockSpec((N_MEM, XA_WIDTH), lambda b, t: (b, 0))
    return pl.pallas_call(
        _xa_seq_kernel,
        grid=(batch, nt),
        in_specs=[pl.BlockSpec((tq, XA_WIDTH), lambda b, t: (b * nt + t, q_block)), kv_spec, kv_spec],
        out_specs=pl.BlockSpec((tq, XA_WIDTH), lambda b, t: (b * nt + t, 0)),
        out_shape=jax.ShapeDtypeStruct((batch * seq, XA_WIDTH), BF16),
        compiler_params=_params("parallel", "parallel"),
        name="xa_seq",
    )(proj, mem_k, mem_v)


XA_LANE_SPLIT = XA_HEAD_DIM // LANES


def _heads_on_sublanes(x):
    lead = x.shape[:-2]
    x = x.reshape(lead + (XA_HEADS, XA_LANE_SPLIT, LANES)).swapaxes(-3, -2)
    return x.reshape(lead + (XA_LANE_SPLIT * XA_HEADS, LANES))


def _heads_from_sublanes(x):
    lead = x.shape[:-2]
    x = x.reshape(lead + (XA_LANE_SPLIT, XA_HEADS, LANES)).swapaxes(-3, -2)
    return x.reshape(lead + (XA_HEADS, XA_HEAD_DIM))


def _xa_step_kernel(q_ref, k_ref, v_ref, o_ref):
    q = q_ref[...]
    part = jnp.sum(k_ref[...] * q, axis=-1, keepdims=True)
    s = (part + pltpu.roll(part, XA_HEADS, axis=2)) * XA_SCALE
    e = jnp.exp(s - jnp.max(s, axis=1, keepdims=True))
    r = 1.0 / jnp.sum(e, axis=1, keepdims=True)
    o_ref[...] = jnp.sum(e * v_ref[...], axis=1, keepdims=True) * r


def _xa_step(q4, cache_k, cache_v, tb):
    assert XA_LANE_SPLIT * XA_HEADS == SUBLANES
    m = q4.shape[0]
    kv_spec = pl.BlockSpec((tb, N_MEM, SUBLANES, LANES), lambda i: (i, 0, 0, 0))
    q_spec = pl.BlockSpec((tb, 1, SUBLANES, LANES), lambda i: (i, 0, 0, 0))
    o = pl.pallas_call(
        _xa_step_kernel,
        grid=(m // tb,),
        in_specs=[q_spec, kv_spec, kv_spec],
        out_specs=q_spec,
        out_shape=jax.ShapeDtypeStruct((m, 1, SUBLANES, LANES), F32),
        compiler_params=_params("parallel"),
        name="xa_step",
    )(_heads_on_sublanes(q4), _heads_on_sublanes(cache_k), _heads_on_sublanes(cache_v))
    return _heads_from_sublanes(o)


def _gate_kernel(z_ref, g_ref, o_ref, ga_ref, gb_ref, gc_ref, wpo_ref, wv_ref, wg_ref, wxo_ref, m_ref, *copy_refs):
    copies = copy_refs if copy_refs else (None,) * 4
    g = g_ref[...]
    o_pool = _dot(z_ref[...], _weight(wpo_ref, copies[0]))
    o_ssm = _dot(g, _weight(wv_ref, copies[1])) * jax.nn.sigmoid(_dot(g, _weight(wg_ref, copies[2])))
    o_xa = _dot(o_ref[...], _weight(wxo_ref, copies[3]))
    merged = (jax.nn.sigmoid(ga_ref[...]) * o_pool + jax.nn.sigmoid(gb_ref[...]) * o_ssm
              + jax.nn.sigmoid(gc_ref[...]) * o_xa)
    m_ref[...] = merged.astype(BF16)


def _gate(z, g, o, proj, weights, tm, tn):
    m, half = z.shape
    emit_w = weights[0].dtype != BF16
    assert not emit_w or m == tm
    d = weights[0].shape[1] if emit_w else weights[0].shape[0] * tn
    gate0 = OFF_GATE // tn
    per_gate = d // tn
    branch_spec = pl.BlockSpec((tm, half), lambda i, n: (i, 0))
    w_specs, copy_specs, copy_shapes = map(list, zip(*[_col_tiled(w, tn) for w in weights]))
    if not emit_w:
        copy_specs, copy_shapes = [], []

    def gate_spec(k):
        return pl.BlockSpec((tm, tn), lambda i, n: (i, gate0 + k * per_gate + n))

    outs = pl.pallas_call(
        _gate_kernel,
        grid=(m // tm, d // tn),
        in_specs=[branch_spec] * 3 + [gate_spec(0), gate_spec(1), gate_spec(2)] + w_specs,
        out_specs=[pl.BlockSpec((tm, tn), lambda i, n: (i, n))] + copy_specs,
        out_shape=[jax.ShapeDtypeStruct((m, d), BF16)] + copy_shapes,
        compiler_params=_params("parallel", "arbitrary"),
        name="gate",
    )(z, g, o, proj, proj, proj, *weights)
    return outs[0], (tuple(outs[1:]) if emit_w else tuple(weights))


def _mixout_kernel(m_ref, wout_ref, h_ref, gpost_ref, gnext_ref, h2_ref, xn_ref, *copy_ref, sub):
    wout = _weight(wout_ref, *copy_ref)
    for r in range(m_ref.shape[0] // sub):
        rows = slice(r * sub, (r + 1) * sub)
        h2 = h_ref[rows, :] + _rms(_dot(m_ref[rows, :], wout), gpost_ref[...])
        h2_ref[rows, :] = h2
        xn_ref[rows, :] = _rms(h2, gnext_ref[...]).astype(BF16)


def _mixout(merged, w_out, h, g_post, g_next, tm):
    m, d = h.shape
    emit_w = w_out.dtype != BF16
    assert not emit_w or m == tm
    row_spec = pl.BlockSpec((tm, d), lambda i: (i, 0))
    gain_spec = pl.BlockSpec((1, d), lambda i: (0, 0))
    w_spec = pl.BlockSpec((d, d), lambda i: (0, 0))
    copy_specs, copy_shapes = _with_copies([w_spec], [w_out], emit_w)
    outs = pl.pallas_call(
        functools.partial(_mixout_kernel, sub=min(tm, MXU_DIM)),
        grid=(m // tm,),
        in_specs=[row_spec, w_spec, row_spec, gain_spec, gain_spec],
        out_specs=[row_spec, row_spec] + copy_specs,
        out_shape=[jax.ShapeDtypeStruct((m, d), F32), jax.ShapeDtypeStruct((m, d), BF16)] + copy_shapes,
        compiler_params=_params("parallel"),
        name="mixout",
    )(merged, w_out, h, g_post, g_next)
    return outs[:2], (outs[2] if emit_w else w_out)


def _block_diag_in(bb):
    gpc = SSM_GROUPS // SSM_CHUNKS
    x = bb.reshape(SSM_CHUNKS, gpc, SSM_STATE, SSM_GROUP_CH).transpose(0, 1, 3, 2)
    eye = jnp.eye(gpc, dtype=bb.dtype)
    y = x[:, :, :, None, :] * eye[None, :, None, :, None]
    return y.reshape(SSM_CHUNKS, gpc * SSM_GROUP_CH, gpc * SSM_STATE)


def _block_diag_out(c):
    gpc = SSM_GROUPS // SSM_CHUNKS
    x = c.reshape(SSM_CHUNKS, gpc, SSM_GROUP_CH, SSM_STATE).transpose(0, 1, 3, 2)
    eye = jnp.eye(gpc, dtype=c.dtype)
    y = x[:, :, :, None, :] * eye[None, :, None, :, None]
    return y.reshape(SSM_CHUNKS, gpc * SSM_STATE, gpc * SSM_GROUP_CH)


def _scan_table(pw_re, pw_im, ps_re, ps_im):
    flat = lambda p: p.reshape(SUBLANES, SSM_LANES)
    pw_re, pw_im, ps_re, ps_im = flat(pw_re), flat(pw_im), flat(ps_re), flat(ps_im)
    row = jnp.arange(SUBLANES)[:, None]
    tiles = [jnp.broadcast_to(pw_re[0], (SUBLANES, SSM_LANES)), jnp.broadcast_to(pw_im[0], (SUBLANES, SSM_LANES))]
    for d in (1, 2, 4):
        keep = row >= d
        tiles.append(jnp.where(keep, ps_re[d - 1][None, :], 0.0))
        tiles.append(jnp.where(keep, ps_im[d - 1][None, :], 0.0))
    tiles += [ps_re, ps_im]
    return jnp.stack(tiles)


def _row_tile(m, target):
    return target if m % target == 0 else m


FFN1_W = ('w_ff1_gate', 'w_ff1_up', 'w_ff1_down')
FFN2_W = ('w_ff2_gate', 'w_ff2_up', 'w_ff2_down')
GATE_W = ('w_pool_out', 'w_glu_val', 'w_glu_gate', 'w_xa_out')
TILES = dict(tm=512, tf=512, tm_proj=2048, tm_gate=1024, tn=512)


def _layer(x, w, mixers):
    m = x.shape[0]
    tm = _row_tile(m, TILES['tm'])
    wb = dict(w)
    (h, xn), wb_ffn1 = _ffn(x, w['g_ff1_pre'], [w[k] for k in FFN1_W], w['g_ff1_post'], tm, TILES['tf'],
                            g_next=w['g_mix_pre'])
    proj, wb['w_in'] = _proj(xn, w['w_in'], _row_tile(m, TILES['tm_proj']), TILES['tn'])
    z, g, o, states = mixers(proj)
    merged, wb_gate = _gate(z, g, o, proj, [w[k] for k in GATE_W], _row_tile(m, TILES['tm_gate']), TILES['tn'])
    (h, xn), wb['w_out'] = _mixout(merged, w['w_out'], h, w['g_mix_post'], w['g_ff2_pre'], tm)
    (y,), wb_ffn2 = _ffn(h, xn, [w[k] for k in FFN2_W], w['g_ff2_post'], tm, TILES['tf'])
    wb.update(zip(FFN1_W + GATE_W + FFN2_W, wb_ffn1 + wb_gate + wb_ffn2))
    return y, states, wb


def kernel(x_prompt, x_sample, mem_prompt, cache_mem_k, cache_mem_v, state_pool, state_ssm_re, state_ssm_im, g_ff1_pre, w_ff1_gate, w_ff1_up, w_ff1_down, g_ff1_post, g_mix_pre, w_in, w_pool_grp, pool_scale, w_pool_out, ssm_a_re, ssm_a_im, ssm_log_step, ssm_b_re, ssm_b_im, ssm_c_re, ssm_c_im, ssm_d, w_glu_val, w_glu_gate, g_mem, w_mem_k, w_mem_v, w_xa_out, w_out, g_mix_post, g_ff2_pre, w_ff2_gate, w_ff2_up, w_ff2_down, g_ff2_post):
    bp, lp, d = x_prompt.shape
    bs, ls, _ = x_sample.shape
    depth = w_in.shape[0]
    assert depth == 1 and ls == 1 and lp >= POOL_BUF and lp % SSM_TIME_TILE == 0

    mats = dict(w_ff1_gate=w_ff1_gate, w_ff1_up=w_ff1_up, w_ff1_down=w_ff1_down, w_in=w_in,
                w_pool_out=w_pool_out, w_glu_val=w_glu_val, w_glu_gate=w_glu_gate, w_xa_out=w_xa_out,
                w_out=w_out, w_ff2_gate=w_ff2_gate, w_ff2_up=w_ff2_up, w_ff2_down=w_ff2_down)
    w = {k: v[0] for k, v in mats.items()}
    w.update(g_ff1_pre=g_ff1_pre, g_ff1_post=g_ff1_post, g_mix_pre=g_mix_pre, g_mix_post=g_mix_post,
             g_ff2_pre=g_ff2_pre, g_ff2_post=g_ff2_post)
    w_grp = w_pool_grp[0].astype(BF16)
    scale = pool_scale
    d_skip = ssm_d

    bb_re, bb_im, pw_re, pw_im, ps_re, ps_im = _ssm_disc(
        ssm_a_re[0], ssm_a_im[0], ssm_log_step[0], ssm_b_re[0], ssm_b_im[0])
    bdr = _block_diag_in(bb_re.astype(BF16))
    bdi = _block_diag_in(bb_im.astype(BF16))
    cdr = _block_diag_out(ssm_c_re[0].astype(BF16))
    cdi = _block_diag_out(ssm_c_im[0].astype(BF16))
    tab = _scan_table(pw_re, pw_im, ps_re, ps_im)
    lb = jnp.stack([pw_re[0].reshape(SSM_LANES), pw_im[0].reshape(SSM_LANES)])

    def sample_mixers(proj):
        z, new_buf = _pool_step(proj, state_pool[0].reshape(bs, POOL_BUF * POOL_WIDTH), w_grp, scale, 32)
        g, x_re, x_im = _ssm_step(proj, state_ssm_re[0].reshape(bs, SSM_LANES),
                                  state_ssm_im[0].reshape(bs, SSM_LANES), lb, bdr, bdi, cdr, cdi, d_skip)
        q4 = proj[:, POOL_WIDTH + SSM_WIDTH:OFF_GATE].reshape(bs, 1, XA_HEADS, XA_HEAD_DIM)
        o = _xa_step(q4, cache_mem_k[0], cache_mem_v[0], 4)
        return z, g, o.reshape(bs, XA_WIDTH).astype(BF16), (new_buf, x_re, x_im)

    y_sample, (pool_s, re_s, im_s), wb = _layer(x_sample.reshape(bs * ls, d), w, sample_mixers)

    mem = mem_prompt.reshape(bp * N_MEM, d)
    mem_k = _normproj(mem, g_mem, w_mem_k[0], bp * N_MEM, 512)
    mem_v = _normproj(mem, g_mem, w_mem_v[0], bp * N_MEM, 512)

    def prompt_mixers(proj):
        z = _pool_seq(proj, w_grp, scale, bp, lp, 512)
        zeros = jnp.zeros((bp, 1, SSM_LANES), F32)
        g, last_re, last_im = _ssm_seq(proj, zeros, zeros, tab, bdr, bdi, cdr, cdi, d_skip, bp, lp)
        o = _xa_seq(proj, mem_k, mem_v, bp, lp, 512)
        pool_rows = proj.reshape(bp, lp, IN_WIDTH)[:, lp - POOL_BUF:, :POOL_WIDTH]
        return z, g, o, (pool_rows, last_re, last_im)

    y_prompt, (pool_p, re_p, im_p), _ = _layer(x_prompt.reshape(bp * lp, d), wb, prompt_mixers)

    kv_shape = (1, bp, N_MEM, XA_HEADS, XA_HEAD_DIM)
    st_shape = (SSM_GROUPS, SSM_STATE)
    return (y_prompt.reshape(bp, lp, d), y_sample.reshape(bs, ls, d),
            mem_k.reshape(kv_shape), mem_v.reshape(kv_shape),
            pool_p[None],
            re_p.reshape((1, bp) + st_shape), im_p.reshape((1, bp) + st_shape),
            pool_s.reshape(1, bs, POOL_BUF, POOL_WIDTH),
            re_s.reshape((1, bs) + st_shape), im_s.reshape((1, bs) + st_shape))
```

```python
import functools
import math

import jax
import jax.numpy as jnp
from jax import lax
from jax.experimental import pallas as pl
from jax.experimental.pallas import tpu as pltpu

F32 = jnp.float32
BF16 = jnp.bfloat16

D_MODEL = 2048
PAST_LEN = 16384
POOL_WIDTH = D_MODEL // 2
POOL_WINDOWS = (2, 4, 8, 16)
POOL_GROUP_DIM = POOL_WIDTH // len(POOL_WINDOWS)
POOL_BUF = max(POOL_WINDOWS) - 1
SSM_WIDTH = D_MODEL // 2
SSM_GROUP_CH = 16
SSM_GROUPS = SSM_WIDTH // SSM_GROUP_CH
SSM_STATE = 64
SSM_LANES = SSM_GROUPS * SSM_STATE
XA_HEADS = 4
XA_HEAD_DIM = D_MODEL // 8
XA_WIDTH = XA_HEADS * XA_HEAD_DIM
XA_SCALE = XA_HEAD_DIM ** -0.5
N_MEM = 256
OFF_GATE = POOL_WIDTH + SSM_WIDTH + XA_WIDTH
IN_WIDTH = OFF_GATE + 3 * D_MODEL
RMS_EPS = 1e-6

SUBLANES = 8
LANES = 128
MXU_DIM = 256
VMEM_LIMIT_BYTES = 60 * 1024 * 1024
DOWN_CHUNKS = 4
NORM_ROWS = 2 * SUBLANES

SSM_CHUNKS = SSM_WIDTH // MXU_DIM
SSM_CHUNK_LANES = SSM_LANES // SSM_CHUNKS
SSM_TIME_TILE = 256
SSM_SEG = SSM_TIME_TILE // SUBLANES
SCAN_LANES = 4 * LANES


def _params(*sem):
    return pltpu.CompilerParams(dimension_semantics=sem, vmem_limit_bytes=VMEM_LIMIT_BYTES)


def _rms(x, g, half=False):
    scale = lax.rsqrt(jnp.mean(x * x, axis=-1, keepdims=True) + RMS_EPS)
    return (x * (0.5 * scale if half else scale)) * g


def _dot(a, b):
    return jnp.dot(a, b, preferred_element_type=F32)


def _cmul(a_re, a_im, b_re, b_im):
    return a_re * b_re - a_im * b_im, a_re * b_im + a_im * b_re


def _weight(w_ref, copy_ref=None):
    w = w_ref[...]
    if w.dtype != BF16:
        w = w.astype(BF16)
    if copy_ref is not None:
        copy_ref[...] = w
    return w


def _with_copies(specs, weights, emit_w):
    if not emit_w:
        return [], []
    return list(specs), [jax.ShapeDtypeStruct(w.shape, BF16) for w in weights]


def _col_tiled(w, tn):
    tiled = lambda k: pl.BlockSpec((None, k, tn), lambda i, j: (j, 0, 0))
    if w.dtype == BF16:
        assert w.ndim == 3 and w.shape[2] == tn, "bf16 weights come tile-major from an earlier call"
        return tiled(w.shape[1]), None, None
    k, n = w.shape
    return pl.BlockSpec((k, tn), lambda i, j: (0, j)), tiled(k), jax.ShapeDtypeStruct((n // tn, k, tn), BF16)


def _ffn_kernel(*refs, first, emit_w):
    refs = list(refs)
    x_ref, pre_ref, wg_ref, wu_ref, wd_ref, gpost_ref = refs[:6]
    del refs[:6]
    gnext_ref = refs.pop(0) if first else None
    o_ref = refs.pop(0)
    xnext_ref = refs.pop(0) if first else None
    wgb_ref, wub_ref, wdb_ref = (refs.pop(0), refs.pop(0), refs.pop(0)) if emit_w else (None, None, None)
    if first:
        xn_ref, = refs
    else:
        xn_ref = pre_ref
        xbuf_ref, xsem = refs
    i = pl.program_id(0)
    j = pl.program_id(1)
    tm = o_ref.shape[0]

    def x_copy():
        return pltpu.make_async_copy(x_ref.at[pl.ds(i * tm, tm), :], xbuf_ref, xsem)

    @pl.when(j == 0)
    def _():
        if first:
            xn_ref[...] = _rms(x_ref[...], pre_ref[...]).astype(BF16)
        else:
            x_copy().start()
        o_ref[...] = jnp.zeros_like(o_ref)

    xn = xn_ref[...]
    g = _dot(xn, _weight(wg_ref, wgb_ref))
    u = _dot(xn, _weight(wu_ref, wub_ref))
    hid = ((g * jax.nn.sigmoid(g)) * u).astype(BF16)
    wd = _weight(wd_ref, wdb_ref)
    cols = wd.shape[1] // DOWN_CHUNKS
    for n in range(DOWN_CHUNKS):
        sl = slice(n * cols, (n + 1) * cols)
        o_ref[:, sl] += _dot(hid, wd[:, sl])

    @pl.when(j == pl.num_programs(1) - 1)
    def _():
        if first:
            out = x_ref[...] + _rms(o_ref[...], gpost_ref[...], half=True)
            o_ref[...] = out
            xnext_ref[...] = _rms(out, gnext_ref[...]).astype(BF16)
        else:
            x_copy().wait()
            for r in range(tm // NORM_ROWS):
                rs = slice(r * NORM_ROWS, (r + 1) * NORM_ROWS)
                o_ref[rs, :] = xbuf_ref[rs, :] + _rms(o_ref[rs, :], gpost_ref[...], half=True)


def _ffn(x, pre, weights, g_post, tm, tf, g_next=None):
    m, d = x.shape
    w_gate, w_up, w_down = weights
    f = w_down.shape[0]
    first = g_next is not None
    emit_w = w_gate.dtype != BF16
    assert not emit_w or m == tm, "weight copies need every weight block visited exactly once"
    row_spec = pl.BlockSpec((tm, d), lambda i, j: (i, 0))
    gain_spec = pl.BlockSpec((1, d), lambda i, j: (0, 0))
    wg_spec, wg_copy, wg_shape = _col_tiled(w_gate, tf)
    wu_spec, wu_copy, wu_shape = _col_tiled(w_up, tf)
    wd_spec = pl.BlockSpec((tf, d), lambda i, j: (j, 0))
    w_specs = [wg_spec, wu_spec, wd_spec]
    copy_specs = [wg_copy, wu_copy, wd_spec] if emit_w else []
    copy_shapes = [wg_shape, wu_shape, jax.ShapeDtypeStruct(w_down.shape, BF16)] if emit_w else []
    outs = pl.pallas_call(
        functools.partial(_ffn_kernel, first=first, emit_w=emit_w),
        grid=(m // tm, f // tf),
        in_specs=([row_spec, gain_spec] if first else [pl.BlockSpec(memory_space=pl.ANY), row_spec])
        + w_specs + [gain_spec] + ([gain_spec] if first else []),
        out_specs=[row_spec] + ([row_spec] if first else []) + copy_specs,
        out_shape=[jax.ShapeDtypeStruct((m, d), F32)] + ([jax.ShapeDtypeStruct((m, d), BF16)] if first else [])
        + copy_shapes,
        scratch_shapes=[pltpu.VMEM((tm, d), BF16)] if first
        else [pltpu.VMEM((tm, d), F32), pltpu.SemaphoreType.DMA(())],
        compiler_params=_params("parallel", "arbitrary"),
        name="ffn",
    )(*((x, pre, w_gate, w_up, w_down, g_post) + ((g_next,) if first else ())))
    n_out = 2 if first else 1
    return outs[:n_out], (tuple(outs[n_out:]) if emit_w else tuple(weights))


def _proj_kernel(xn_ref, w_ref, o_ref, *copy_ref):
    o_ref[...] = _dot(xn_ref[...], _weight(w_ref, *copy_ref))


def _proj(xn, w, tm, tn):
    m, d = xn.shape
    emit_w = w.dtype != BF16
    assert not emit_w or m == tm
    n = w.shape[1] if emit_w else w.shape[0] * tn
    w_spec, copy_spec, copy_shape = _col_tiled(w, tn)
    copy_specs, copy_shapes = ([copy_spec], [copy_shape]) if emit_w else ([], [])
    outs = pl.pallas_call(
        _proj_kernel,
        grid=(m // tm, n // tn),
        in_specs=[pl.BlockSpec((tm, d), lambda i, j: (i, 0)), w_spec],
        out_specs=[pl.BlockSpec((tm, tn), lambda i, j: (i, j))] + copy_specs,
        out_shape=[jax.ShapeDtypeStruct((m, n), F32)] + copy_shapes,
        compiler_params=_params("parallel", "arbitrary"),
        name="proj",
    )(xn, w)
    return outs[0], (outs[1] if emit_w else w)


def _memproj_kernel(x_ref, g_ref, w_ref, o_ref, oh_ref):
    res = _dot(_rms(x_ref[...], g_ref[...]).astype(BF16), _weight(w_ref))
    o_ref[...] = res
    for h in range(XA_HEADS):
        for p in range(XA_LANE_SPLIT):
            lane0 = h * XA_HEAD_DIM + p * LANES
            oh_ref[:, p * XA_HEADS + h, :] = res[:, lane0:lane0 + LANES]


def _memproj(x, g, w):
    m, d = x.shape
    n = w.shape[1]
    assert n == XA_WIDTH
    tm = _row_tile(m, TILES['tm'])
    flat, by_head = pl.pallas_call(
        _memproj_kernel,
        grid=(m // tm,),
        in_specs=[pl.BlockSpec((tm, d), lambda i: (i, 0)), pl.BlockSpec((1, d), lambda i: (0, 0)),
                  pl.BlockSpec((d, n), lambda i: (0, 0))],
        out_specs=[pl.BlockSpec((tm, n), lambda i: (i, 0)), pl.BlockSpec((tm, SUBLANES, LANES), lambda i: (i, 0, 0))],
        out_shape=[jax.ShapeDtypeStruct((m, n), F32), jax.ShapeDtypeStruct((m, SUBLANES, LANES), F32)],
        compiler_params=_params("parallel"),
        name="memproj",
    )(x, g, w)
    return flat, _heads_from_sublanes(by_head)


def _pool_group_out(k, diff, wgrp_ref, scale_ref, z_ref):
    ch = slice(k * POOL_GROUP_DIM, (k + 1) * POOL_GROUP_DIM)
    zk = _dot(diff.astype(BF16), wgrp_ref[k]) * scale_ref[:, ch]
    z_ref[:, ch] = zk.astype(BF16)


def _pool_seq_kernel(u_ref, wgrp_ref, scale_ref, z_ref, ext_ref, *, halo):
    t = pl.program_id(1)
    tt = u_ref.shape[0]

    @pl.when(t == 0)
    def _():
        ext_ref[0:halo, :] = jnp.zeros((halo, POOL_WIDTH), F32)

    u = u_ref[...]
    ext_ref[halo:halo + tt, :] = u
    pos = t * tt + lax.broadcasted_iota(jnp.int32, (tt, 1), 0)
    for k, w in enumerate(POOL_WINDOWS):
        ch = slice(k * POOL_GROUP_DIM, (k + 1) * POOL_GROUP_DIM)
        a = ext_ref[:, ch]
        d = 1
        while d < w:
            a = a + pltpu.roll(a, d, axis=0)
            d *= 2
        count = jnp.minimum(pos + 1, w).astype(F32)
        diff = a[halo:, :] / count - u[:, ch]
        _pool_group_out(k, diff, wgrp_ref, scale_ref, z_ref)
    ext_ref[0:halo, :] = ext_ref[tt:tt + halo, :]


def _pool_seq(proj, w_grp, scale, batch, seq, tt):
    halo = 2 * SUBLANES
    nt = seq // tt
    return pl.pallas_call(
        functools.partial(_pool_seq_kernel, halo=halo),
        grid=(batch, nt),
        in_specs=[
            pl.BlockSpec((tt, POOL_WIDTH), lambda b, t: (b * nt + t, 0)),
            pl.BlockSpec(w_grp.shape, lambda b, t: (0, 0, 0)),
            pl.BlockSpec((1, POOL_WIDTH), lambda b, t: (0, 0)),
        ],
        out_specs=pl.BlockSpec((tt, POOL_WIDTH), lambda b, t: (b * nt + t, 0)),
        out_shape=jax.ShapeDtypeStruct((batch * seq, POOL_WIDTH), BF16),
        scratch_shapes=[pltpu.VMEM((tt + halo, POOL_WIDTH), F32)],
        compiler_params=_params("parallel", "arbitrary"),
        name="pool_seq",
    )(proj, w_grp, scale)


def _pool_step_kernel(u_ref, buf_ref, wgrp_ref, scale_ref, z_ref, nbuf_ref):
    u = u_ref[...]
    for k, w in enumerate(POOL_WINDOWS):
        ch = slice(k * POOL_GROUP_DIM, (k + 1) * POOL_GROUP_DIM)
        s = u[:, ch]
        for j in range(1, w):
            row = POOL_BUF - j
            s = s + buf_ref[:, row * POOL_WIDTH + k * POOL_GROUP_DIM:
                            row * POOL_WIDTH + (k + 1) * POOL_GROUP_DIM]
        diff = s / float(min(PAST_LEN + 1, w)) - u[:, ch]
        _pool_group_out(k, diff, wgrp_ref, scale_ref, z_ref)
    keep = (POOL_BUF - 1) * POOL_WIDTH
    nbuf_ref[:, 0:keep] = buf_ref[:, POOL_WIDTH:]
    nbuf_ref[:, keep:] = u


def _pool_step(proj, buf, w_grp, scale, tb):
    m = proj.shape[0]
    return pl.pallas_call(
        _pool_step_kernel,
        grid=(m // tb,),
        in_specs=[
            pl.BlockSpec((tb, POOL_WIDTH), lambda i: (i, 0)),
            pl.BlockSpec((tb, POOL_BUF * POOL_WIDTH), lambda i: (i, 0)),
            pl.BlockSpec(w_grp.shape, lambda i: (0, 0, 0)),
            pl.BlockSpec((1, POOL_WIDTH), lambda i: (0, 0)),
        ],
        out_specs=[
            pl.BlockSpec((tb, POOL_WIDTH), lambda i: (i, 0)),
            pl.BlockSpec((tb, POOL_BUF * POOL_WIDTH), lambda i: (i, 0)),
        ],
        out_shape=[
            jax.ShapeDtypeStruct((m, POOL_WIDTH), BF16),
            jax.ShapeDtypeStruct((m, POOL_BUF * POOL_WIDTH), F32),
        ],
        compiler_params=_params("parallel"),
        name="pool_step",
    )(proj, buf, w_grp, scale)


def _zoh_lambda(a_re, a_im, log_step):
    dt = jnp.exp(log_step)
    mag = jnp.exp(a_re * dt)
    ang = a_im * dt
    return mag * jnp.cos(ang), mag * jnp.sin(ang)


def _ssm_disc_kernel(a_re_ref, a_im_ref, ls_ref, a_re_col_ref, a_im_col_ref, ls_col_ref, b_re_ref, b_im_ref,
                     bb_re_ref, bb_im_ref, pw_re_ref, pw_im_ref, ps_re_ref, ps_im_ref):
    a_re = a_re_col_ref[...]
    a_im = a_im_col_ref[...]
    lb_re, lb_im = _zoh_lambda(a_re, a_im, ls_col_ref[...])
    den = a_re * a_re + a_im * a_im
    n_re = lb_re - 1.0
    f_re = (n_re * a_re + lb_im * a_im) / den
    f_im = (lb_im * a_re - n_re * a_im) / den
    b_re = b_re_ref[...]
    b_im = b_im_ref[...]
    bb_re_ref[...] = f_re * b_re - f_im * b_im
    bb_im_ref[...] = f_re * b_im + f_im * b_re

    lb_re, lb_im = _zoh_lambda(a_re_ref[...], a_im_ref[...], ls_ref[...])

    def powers(base_re, base_im, out_re_ref, out_im_ref):
        p_re, p_im = base_re, base_im
        out_re_ref[0] = p_re
        out_im_ref[0] = p_im
        for r in range(1, SUBLANES):
            p_re, p_im = _cmul(p_re, p_im, base_re, base_im)
            out_re_ref[r] = p_re
            out_im_ref[r] = p_im
        return p_re, p_im

    s_re, s_im = powers(lb_re, lb_im, pw_re_ref, pw_im_ref)
    for _ in range(int(math.log2(SSM_SEG // SUBLANES))):
        s_re, s_im = _cmul(s_re, s_im, s_re, s_im)
    powers(s_re, s_im, ps_re_ref, ps_im_ref)


def _ssm_disc(a_re, a_im, log_step, b_re, b_im):
    g, n, h = b_re.shape
    col = (g * n, 1)
    pw = jax.ShapeDtypeStruct((SUBLANES, g, n), F32)
    bb = jax.ShapeDtypeStruct((g * n, h), F32)
    bb_re, bb_im, pw_re, pw_im, ps_re, ps_im = pl.pallas_call(
        _ssm_disc_kernel,
        out_shape=[bb, bb, pw, pw, pw, pw],
        name="ssm_disc",
    )(a_re, a_im, log_step[:, None],
      a_re.reshape(col), a_im.reshape(col), jnp.broadcast_to(log_step[:, None], (g, n)).reshape(col),
      b_re.reshape(g * n, h), b_im.reshape(g * n, h))
    return bb_re.reshape(g, n, h), bb_im.reshape(g, n, h), pw_re, pw_im, ps_re, ps_im


def _ssm_in(ub, bd_ref, x_ref):
    for c in range(SSM_CHUNKS):
        x_ref[:, c * SSM_CHUNK_LANES:(c + 1) * SSM_CHUNK_LANES] = _dot(
            ub[:, c * MXU_DIM:(c + 1) * MXU_DIM], bd_ref[c])


def _ssm_out(c, u, x_re_ref, x_im_ref, cdr_ref, cdi_ref, d_ref):
    ch = slice(c * MXU_DIM, (c + 1) * MXU_DIM)
    st = slice(c * SSM_CHUNK_LANES, (c + 1) * SSM_CHUNK_LANES)
    y = _dot(x_re_ref[:, st].astype(BF16), cdr_ref[c]) - _dot(x_im_ref[:, st].astype(BF16), cdi_ref[c])
    return jax.nn.gelu(y + d_ref[:, ch] * u[:, ch])


def _ssm_seq_kernel(u_ref, h_re_ref, h_im_ref, tab_ref, bdr_ref, bdi_ref, cdr_ref, cdi_ref, d_ref,
                    g_ref, last_re_ref, last_im_ref, xr_ref, xi_ref, cr_ref, ci_ref):
    t = pl.program_id(1)
    tt = u_ref.shape[0]
    seg = tt // SUBLANES

    @pl.when(t == 0)
    def _():
        cr_ref[...] = jnp.broadcast_to(h_re_ref[...], cr_ref.shape)
        ci_ref[...] = jnp.broadcast_to(h_im_ref[...], ci_ref.shape)

    def regroup(x, rows_major, rows_minor):
        w = x.shape[1]
        return jnp.swapaxes(x.reshape(rows_major, rows_minor, w), 0, 1).reshape(tt, w)

    u = regroup(u_ref[...], SUBLANES, seg)
    ub = u.astype(BF16)
    _ssm_in(ub, bdr_ref, xr_ref)
    _ssm_in(ub, bdi_ref, xi_ref)

    first_row = lax.broadcasted_iota(jnp.int32, (SUBLANES, SCAN_LANES), 0) == 0
    for lc in range(SSM_LANES // SCAN_LANES):
        sl = slice(lc * SCAN_LANES, (lc + 1) * SCAN_LANES)
        l_re = tab_ref[0, :, sl]
        l_im = tab_ref[1, :, sl]

        def rows(j):
            return pl.ds(j * SUBLANES, SUBLANES)

        def advance(j, x, sl=sl, l_re=l_re, l_im=l_im):
            x_re, x_im = _cmul(l_re, l_im, *x)
            x_re = x_re + xr_ref[rows(j), sl]
            x_im = x_im + xi_ref[rows(j), sl]
            xr_ref[rows(j), sl] = x_re
            xi_ref[rows(j), sl] = x_im
            return x_re, x_im

        zero = jnp.zeros((SUBLANES, SCAN_LANES), F32)
        f_re, f_im = zero, zero
        for j in range(seg):
            f_re, f_im = advance(j, (f_re, f_im))

        c_re = cr_ref[:, sl]
        c_im = ci_ref[:, sl]
        for i in range(3):
            s_re = pltpu.roll(f_re, 1 << i, axis=0)
            s_im = pltpu.roll(f_im, 1 << i, axis=0)
            m_re, m_im = _cmul(tab_ref[2 + 2 * i, :, sl], tab_ref[3 + 2 * i, :, sl], s_re, s_im)
            f_re, f_im = f_re + m_re, f_im + m_im
        m_re, m_im = _cmul(tab_ref[8, :, sl], tab_ref[9, :, sl], c_re, c_im)
        f_re, f_im = f_re + m_re, f_im + m_im
        e_re = jnp.where(first_row, c_re, pltpu.roll(f_re, 1, axis=0))
        e_im = jnp.where(first_row, c_im, pltpu.roll(f_im, 1, axis=0))
        cr_ref[:, sl] = jnp.broadcast_to(f_re[SUBLANES - 1:SUBLANES, :], (SUBLANES, SCAN_LANES))
        ci_ref[:, sl] = jnp.broadcast_to(f_im[SUBLANES - 1:SUBLANES, :], (SUBLANES, SCAN_LANES))

        def fix(j, w, sl=sl, l_re=l_re, l_im=l_im):
            w_re, w_im = _cmul(l_re, l_im, *w)
            xr_ref[rows(j), sl] += w_re
            xi_ref[rows(j), sl] += w_im
            return w_re, w_im

        w = (e_re, e_im)
        for j in range(seg):
            w = fix(j, w)

    for c in range(SSM_CHUNKS):
        ch = slice(c * MXU_DIM, (c + 1) * MXU_DIM)
        g = _ssm_out(c, u, xr_ref, xi_ref, cdr_ref, cdi_ref, d_ref)
        g_ref[:, ch] = regroup(g, seg, SUBLANES).astype(BF16)

    @pl.when(t == pl.num_programs(1) - 1)
    def _():
        last_re_ref[...] = cr_ref[0:1, :]
        last_im_ref[...] = ci_ref[0:1, :]


def _const_spec(shape):
    return pl.BlockSpec(shape, lambda *_: (0,) * len(shape))


def _ssm_seq(proj, h_re, h_im, tab, bdr, bdi, cdr, cdi, d_skip, batch, seq):
    tt = SSM_TIME_TILE
    nt = seq // tt
    col_block = POOL_WIDTH // SSM_WIDTH
    state_spec = pl.BlockSpec((None, 1, SSM_LANES), lambda b, t: (b, 0, 0))
    state_shape = jax.ShapeDtypeStruct((batch, 1, SSM_LANES), F32)
    return pl.pallas_call(
        _ssm_seq_kernel,
        grid=(batch, nt),
        in_specs=[
            pl.BlockSpec((tt, SSM_WIDTH), lambda b, t: (b * nt + t, col_block)),
            state_spec, state_spec,
            _const_spec(tab.shape),
            _const_spec(bdr.shape), _const_spec(bdi.shape),
            _const_spec(cdr.shape), _const_spec(cdi.shape),
            _const_spec(d_skip.shape),
        ],
        out_specs=[pl.BlockSpec((tt, SSM_WIDTH), lambda b, t: (b * nt + t, 0)), state_spec, state_spec],
        out_shape=[jax.ShapeDtypeStruct((batch * seq, SSM_WIDTH), BF16), state_shape, state_shape],
        scratch_shapes=[
            pltpu.VMEM((tt, SSM_LANES), F32), pltpu.VMEM((tt, SSM_LANES), F32),
            pltpu.VMEM((SUBLANES, SSM_LANES), F32), pltpu.VMEM((SUBLANES, SSM_LANES), F32),
        ],
        compiler_params=_params("parallel", "arbitrary"),
        name="ssm_seq",
    )(proj, h_re, h_im, tab, bdr, bdi, cdr, cdi, d_skip)


def _ssm_step_kernel(u_ref, h_re_ref, h_im_ref, lb_ref, bdr_ref, bdi_ref, cdr_ref, cdi_ref, d_ref,
                     g_ref, x_re_ref, x_im_ref):
    u = u_ref[...]
    ub = u.astype(BF16)
    _ssm_in(ub, bdr_ref, x_re_ref)
    _ssm_in(ub, bdi_ref, x_im_ref)
    m_re, m_im = _cmul(lb_ref[0:1, :], lb_ref[1:2, :], h_re_ref[...], h_im_ref[...])
    x_re_ref[...] += m_re
    x_im_ref[...] += m_im
    for c in range(SSM_CHUNKS):
        ch = slice(c * MXU_DIM, (c + 1) * MXU_DIM)
        g_ref[:, ch] = _ssm_out(c, u, x_re_ref, x_im_ref, cdr_ref, cdi_ref, d_ref).astype(BF16)


def _ssm_step(proj, h_re, h_im, lb, bdr, bdi, cdr, cdi, d_skip):
    m = proj.shape[0]
    state_spec = pl.BlockSpec((m, SSM_LANES), lambda i: (0, 0))
    state_shape = jax.ShapeDtypeStruct((m, SSM_LANES), F32)
    return pl.pallas_call(
        _ssm_step_kernel,
        grid=(1,),
        in_specs=[
            pl.BlockSpec((m, SSM_WIDTH), lambda i: (0, POOL_WIDTH // SSM_WIDTH)),
            state_spec, state_spec,
            _const_spec(lb.shape),
            _const_spec(bdr.shape), _const_spec(bdi.shape),
            _const_spec(cdr.shape), _const_spec(cdi.shape),
            _const_spec(d_skip.shape),
        ],
        out_specs=[pl.BlockSpec((m, SSM_WIDTH), lambda i: (0, 0)), state_spec, state_spec],
        out_shape=[jax.ShapeDtypeStruct((m, SSM_WIDTH), BF16), state_shape, state_shape],
        compiler_params=_params("arbitrary"),
        name="ssm_step",
    )(proj, h_re, h_im, lb, bdr, bdi, cdr, cdi, d_skip)


def _xa_seq_kernel(q_ref, k_ref, v_ref, o_ref):
    for h in range(XA_HEADS):
        ch = slice(h * XA_HEAD_DIM, (h + 1) * XA_HEAD_DIM)
        q = q_ref[:, ch].astype(BF16)
        k = k_ref[:, ch].astype(BF16)
        v = v_ref[:, ch].astype(BF16)
        s = lax.dot_general(q, k, (((1,), (1,)), ((), ())), preferred_element_type=F32) * XA_SCALE
        e = jnp.exp(s - jnp.max(s, axis=-1, keepdims=True))
        p = e / jnp.sum(e, axis=-1, keepdims=True)
        o_ref[:, ch] = _dot(p.astype(BF16), v).astype(BF16)


def _xa_seq(proj, mem_k, mem_v, batch, seq, tq):
    nt = seq // tq
    q_block = (POOL_WIDTH + SSM_WIDTH) // XA_WIDTH
    kv_spec = pl.BlockSpec((N_MEM, XA_WIDTH), lambda b, t: (b, 0))
    return pl.pallas_call(
        _xa_seq_kernel,
        grid=(batch, nt),
        in_specs=[pl.BlockSpec((tq, XA_WIDTH), lambda b, t: (b * nt + t, q_block)), kv_spec, kv_spec],
        out_specs=pl.BlockSpec((tq, XA_WIDTH), lambda b, t: (b * nt + t, 0)),
        out_shape=jax.ShapeDtypeStruct((batch * seq, XA_WIDTH), BF16),
        compiler_params=_params("parallel", "parallel"),
        name="xa_seq",
    )(proj, mem_k, mem_v)


XA_LANE_SPLIT = XA_HEAD_DIM // LANES


def _heads_on_sublanes(x):
    lead = x.shape[:-2]
    x = x.reshape(lead + (XA_HEADS, XA_LANE_SPLIT, LANES)).swapaxes(-3, -2)
    return x.reshape(lead + (XA_LANE_SPLIT * XA_HEADS, LANES))


def _heads_from_sublanes(x):
    lead = x.shape[:-2]
    x = x.reshape(lead + (XA_LANE_SPLIT, XA_HEADS, LANES)).swapaxes(-3, -2)
    return x.reshape(lead + (XA_HEADS, XA_HEAD_DIM))


def _xa_step_kernel(q_ref, k_ref, v_ref, o_ref):
    q = q_ref[...]
    part = jnp.sum(k_ref[...] * q, axis=-1, keepdims=True)
    s = (part + pltpu.roll(part, XA_HEADS, axis=2)) * XA_SCALE
    e = jnp.exp(s - jnp.max(s, axis=1, keepdims=True))
    r = 1.0 / jnp.sum(e, axis=1, keepdims=True)
    o_ref[...] = jnp.sum(e * v_ref[...], axis=1, keepdims=True) * r


def _xa_step(q4, cache_k, cache_v, tb):
    assert XA_LANE_SPLIT * XA_HEADS == SUBLANES
    m = q4.shape[0]
    kv_spec = pl.BlockSpec((tb, N_MEM, SUBLANES, LANES), lambda i: (i, 0, 0, 0))
    q_spec = pl.BlockSpec((tb, 1, SUBLANES, LANES), lambda i: (i, 0, 0, 0))
    o = pl.pallas_call(
        _xa_step_kernel,
        grid=(m // tb,),
        in_specs=[q_spec, kv_spec, kv_spec],
        out_specs=q_spec,
        out_shape=jax.ShapeDtypeStruct((m, 1, SUBLANES, LANES), F32),
        compiler_params=_params("parallel"),
        name="xa_step",
    )(_heads_on_sublanes(q4), _heads_on_sublanes(cache_k), _heads_on_sublanes(cache_v))
    return _heads_from_sublanes(o)


def _gate_kernel(z_ref, g_ref, o_ref, ga_ref, gb_ref, gc_ref, wpo_ref, wv_ref, wg_ref, wxo_ref, m_ref, *copy_refs):
    copies = copy_refs if copy_refs else (None,) * 4
    g = g_ref[...]
    o_pool = _dot(z_ref[...], _weight(wpo_ref, copies[0]))
    o_ssm = _dot(g, _weight(wv_ref, copies[1])) * jax.nn.sigmoid(_dot(g, _weight(wg_ref, copies[2])))
    o_xa = _dot(o_ref[...], _weight(wxo_ref, copies[3]))
    merged = (jax.nn.sigmoid(ga_ref[...]) * o_pool + jax.nn.sigmoid(gb_ref[...]) * o_ssm
              + jax.nn.sigmoid(gc_ref[...]) * o_xa)
    m_ref[...] = merged.astype(BF16)


def _gate(z, g, o, proj, weights, tm, tn):
    m, half = z.shape
    emit_w = weights[0].dtype != BF16
    assert not emit_w or m == tm
    d = weights[0].shape[1] if emit_w else weights[0].shape[0] * tn
    gate0 = OFF_GATE // tn
    per_gate = d // tn
    branch_spec = pl.BlockSpec((tm, half), lambda i, n: (i, 0))
    w_specs, copy_specs, copy_shapes = map(list, zip(*[_col_tiled(w, tn) for w in weights]))
    if not emit_w:
        copy_specs, copy_shapes = [], []

    def gate_spec(k):
        return pl.BlockSpec((tm, tn), lambda i, n: (i, gate0 + k * per_gate + n))

    outs = pl.pallas_call(
        _gate_kernel,
        grid=(m // tm, d // tn),
        in_specs=[branch_spec] * 3 + [gate_spec(0), gate_spec(1), gate_spec(2)] + w_specs,
        out_specs=[pl.BlockSpec((tm, tn), lambda i, n: (i, n))] + copy_specs,
        out_shape=[jax.ShapeDtypeStruct((m, d), BF16)] + copy_shapes,
        compiler_params=_params("parallel", "arbitrary"),
        name="gate",
    )(z, g, o, proj, proj, proj, *weights)
    return outs[0], (tuple(outs[1:]) if emit_w else tuple(weights))


def _mixout_kernel(m_ref, wout_ref, h_ref, gpost_ref, gnext_ref, h2_ref, xn_ref, *copy_ref, sub):
    wout = _weight(wout_ref, *copy_ref)
    for r in range(m_ref.shape[0] // sub):
        rows = slice(r * sub, (r + 1) * sub)
        h2 = h_ref[rows, :] + _rms(_dot(m_ref[rows, :], wout), gpost_ref[...])
        h2_ref[rows, :] = h2
        xn_ref[rows, :] = _rms(h2, gnext_ref[...]).astype(BF16)


def _mixout(merged, w_out, h, g_post, g_next, tm):
    m, d = h.shape
    emit_w = w_out.dtype != BF16
    assert not emit_w or m == tm
    row_spec = pl.BlockSpec((tm, d), lambda i: (i, 0))
    gain_spec = pl.BlockSpec((1, d), lambda i: (0, 0))
    w_spec = pl.BlockSpec((d, d), lambda i: (0, 0))
    copy_specs, copy_shapes = _with_copies([w_spec], [w_out], emit_w)
    outs = pl.pallas_call(
        functools.partial(_mixout_kernel, sub=min(tm, MXU_DIM)),
        grid=(m // tm,),
        in_specs=[row_spec, w_spec, row_spec, gain_spec, gain_spec],
        out_specs=[row_spec, row_spec] + copy_specs,
        out_shape=[jax.ShapeDtypeStruct((m, d), F32), jax.ShapeDtypeStruct((m, d), BF16)] + copy_shapes,
        compiler_params=_params("parallel"),
        name="mixout",
    )(merged, w_out, h, g_post, g_next)
    return outs[:2], (outs[2] if emit_w else w_out)


def _block_diag_in(bb):
    gpc = SSM_GROUPS // SSM_CHUNKS
    x = bb.reshape(SSM_CHUNKS, gpc, SSM_STATE, SSM_GROUP_CH).transpose(0, 1, 3, 2)
    eye = jnp.eye(gpc, dtype=bb.dtype)
    y = x[:, :, :, None, :] * eye[None, :, None, :, None]
    return y.reshape(SSM_CHUNKS, gpc * SSM_GROUP_CH, gpc * SSM_STATE)


def _block_diag_out(c):
    gpc = SSM_GROUPS // SSM_CHUNKS
    x = c.reshape(SSM_CHUNKS, gpc, SSM_GROUP_CH, SSM_STATE).transpose(0, 1, 3, 2)
    eye = jnp.eye(gpc, dtype=c.dtype)
    y = x[:, :, :, None, :] * eye[None, :, None, :, None]
    return y.reshape(SSM_CHUNKS, gpc * SSM_STATE, gpc * SSM_GROUP_CH)


def _scan_table(pw_re, pw_im, ps_re, ps_im):
    flat = lambda p: p.reshape(SUBLANES, SSM_LANES)
    pw_re, pw_im, ps_re, ps_im = flat(pw_re), flat(pw_im), flat(ps_re), flat(ps_im)
    row = jnp.arange(SUBLANES)[:, None]
    tiles = [jnp.broadcast_to(pw_re[0], (SUBLANES, SSM_LANES)), jnp.broadcast_to(pw_im[0], (SUBLANES, SSM_LANES))]
    for d in (1, 2, 4):
        keep = row >= d
        tiles.append(jnp.where(keep, ps_re[d - 1][None, :], 0.0))
        tiles.append(jnp.where(keep, ps_im[d - 1][None, :], 0.0))
    tiles += [ps_re, ps_im]
    return jnp.stack(tiles)


def _row_tile(m, target):
    return target if m % target == 0 else m


FFN1_W = ('w_ff1_gate', 'w_ff1_up', 'w_ff1_down')
FFN2_W = ('w_ff2_gate', 'w_ff2_up', 'w_ff2_down')
GATE_W = ('w_pool_out', 'w_glu_val', 'w_glu_gate', 'w_xa_out')
TILES = dict(tm=512, tm_ffn2=1024, tf=512, tm_proj=2048, tm_gate=1024, tn=512)


def _layer(x, w, mixers):
    m = x.shape[0]
    tm = _row_tile(m, TILES['tm'])
    wb = dict(w)
    (h, xn), wb_ffn1 = _ffn(x, w['g_ff1_pre'], [w[k] for k in FFN1_W], w['g_ff1_post'], tm, TILES['tf'],
                            g_next=w['g_mix_pre'])
    proj, wb['w_in'] = _proj(xn, w['w_in'], _row_tile(m, TILES['tm_proj']), TILES['tn'])
    z, g, o, states = mixers(proj)
    merged, wb_gate = _gate(z, g, o, proj, [w[k] for k in GATE_W], _row_tile(m, TILES['tm_gate']), TILES['tn'])
    (h, xn), wb['w_out'] = _mixout(merged, w['w_out'], h, w['g_mix_post'], w['g_ff2_pre'], tm)
    (y,), wb_ffn2 = _ffn(h, xn, [w[k] for k in FFN2_W], w['g_ff2_post'], _row_tile(m, TILES['tm_ffn2']),
                         TILES['tf'])
    wb.update(zip(FFN1_W + GATE_W + FFN2_W, wb_ffn1 + wb_gate + wb_ffn2))
    return y, states, wb


def kernel(x_prompt, x_sample, mem_prompt, cache_mem_k, cache_mem_v, state_pool, state_ssm_re, state_ssm_im, g_ff1_pre, w_ff1_gate, w_ff1_up, w_ff1_down, g_ff1_post, g_mix_pre, w_in, w_pool_grp, pool_scale, w_pool_out, ssm_a_re, ssm_a_im, ssm_log_step, ssm_b_re, ssm_b_im, ssm_c_re, ssm_c_im, ssm_d, w_glu_val, w_glu_gate, g_mem, w_mem_k, w_mem_v, w_xa_out, w_out, g_mix_post, g_ff2_pre, w_ff2_gate, w_ff2_up, w_ff2_down, g_ff2_post):
    bp, lp, d = x_prompt.shape
    bs, ls, _ = x_sample.shape
    depth = w_in.shape[0]
    assert depth == 1 and ls == 1 and lp >= POOL_BUF and lp % SSM_TIME_TILE == 0

    mats = dict(w_ff1_gate=w_ff1_gate, w_ff1_up=w_ff1_up, w_ff1_down=w_ff1_down, w_in=w_in,
                w_pool_out=w_pool_out, w_glu_val=w_glu_val, w_glu_gate=w_glu_gate, w_xa_out=w_xa_out,
                w_out=w_out, w_ff2_gate=w_ff2_gate, w_ff2_up=w_ff2_up, w_ff2_down=w_ff2_down)
    w = {k: v[0] for k, v in mats.items()}
    w.update(g_ff1_pre=g_ff1_pre, g_ff1_post=g_ff1_post, g_mix_pre=g_mix_pre, g_mix_post=g_mix_post,
             g_ff2_pre=g_ff2_pre, g_ff2_post=g_ff2_post)
    w_grp = w_pool_grp[0].astype(BF16)
    scale = pool_scale
    d_skip = ssm_d

    bb_re, bb_im, pw_re, pw_im, ps_re, ps_im = _ssm_disc(
        ssm_a_re[0], ssm_a_im[0], ssm_log_step[0], ssm_b_re[0], ssm_b_im[0])
    bdr = _block_diag_in(bb_re.astype(BF16))
    bdi = _block_diag_in(bb_im.astype(BF16))
    cdr = _block_diag_out(ssm_c_re[0].astype(BF16))
    cdi = _block_diag_out(ssm_c_im[0].astype(BF16))
    tab = _scan_table(pw_re, pw_im, ps_re, ps_im)
    lb = jnp.stack([pw_re[0].reshape(SSM_LANES), pw_im[0].reshape(SSM_LANES)])

    def sample_mixers(proj):
        z, new_buf = _pool_step(proj, state_pool[0].reshape(bs, POOL_BUF * POOL_WIDTH), w_grp, scale, 32)
        g, x_re, x_im = _ssm_step(proj, state_ssm_re[0].reshape(bs, SSM_LANES),
                                  state_ssm_im[0].reshape(bs, SSM_LANES), lb, bdr, bdi, cdr, cdi, d_skip)
        q4 = proj[:, POOL_WIDTH + SSM_WIDTH:OFF_GATE].reshape(bs, 1, XA_HEADS, XA_HEAD_DIM)
        o = _xa_step(q4, cache_mem_k[0], cache_mem_v[0], 4)
        return z, g, o.reshape(bs, XA_WIDTH).astype(BF16), (new_buf, x_re, x_im)

    y_sample, (pool_s, re_s, im_s), wb = _layer(x_sample.reshape(bs * ls, d), w, sample_mixers)

    mem = mem_prompt.reshape(bp * N_MEM, d)
    mem_k, mem_k_heads = _memproj(mem, g_mem, w_mem_k[0])
    mem_v, mem_v_heads = _memproj(mem, g_mem, w_mem_v[0])

    def prompt_mixers(proj):
        z = _pool_seq(proj, w_grp, scale, bp, lp, 512)
        zeros = jnp.zeros((bp, 1, SSM_LANES), F32)
        g, last_re, last_im = _ssm_seq(proj, zeros, zeros, tab, bdr, bdi, cdr, cdi, d_skip, bp, lp)
        o = _xa_seq(proj, mem_k, mem_v, bp, lp, 512)
        pool_rows = proj.reshape(bp, lp, IN_WIDTH)[:, lp - POOL_BUF:, :POOL_WIDTH]
        return z, g, o, (pool_rows, last_re, last_im)

    y_prompt, (pool_p, re_p, im_p), _ = _layer(x_prompt.reshape(bp * lp, d), wb, prompt_mixers)

    kv_shape = (1, bp, N_MEM, XA_HEADS, XA_HEAD_DIM)
    st_shape = (SSM_GROUPS, SSM_STATE)
    return (y_prompt.reshape(bp, lp, d), y_sample.reshape(bs, ls, d),
            mem_k_heads.reshape(kv_shape), mem_v_heads.reshape(kv_shape),
            pool_p[None],
            re_p.reshape((1, bp) + st_shape), im_p.reshape((1, bp) + st_shape),
            pool_s.reshape(1, bs, POOL_BUF, POOL_WIDTH),
            re_s.reshape((1, bs) + st_shape), im_s.reshape((1, bs) + st_shape))
```

```python
import functools
import math

import jax
import jax.numpy as jnp
from jax import lax
from jax.experimental import pallas as pl
from jax.experimental.pallas import tpu as pltpu

F32 = jnp.float32
BF16 = jnp.bfloat16

D_MODEL = 2048
PAST_LEN = 16384
POOL_WIDTH = D_MODEL // 2
POOL_WINDOWS = (2, 4, 8, 16)
POOL_GROUP_DIM = POOL_WIDTH // len(POOL_WINDOWS)
POOL_BUF = max(POOL_WINDOWS) - 1
SSM_WIDTH = D_MODEL // 2
SSM_GROUP_CH = 16
SSM_GROUPS = SSM_WIDTH // SSM_GROUP_CH
SSM_STATE = 64
SSM_LANES = SSM_GROUPS * SSM_STATE
XA_HEADS = 4
XA_HEAD_DIM = D_MODEL // 8
XA_WIDTH = XA_HEADS * XA_HEAD_DIM
XA_SCALE = XA_HEAD_DIM ** -0.5
N_MEM = 256
OFF_GATE = POOL_WIDTH + SSM_WIDTH + XA_WIDTH
IN_WIDTH = OFF_GATE + 3 * D_MODEL
RMS_EPS = 1e-6

SUBLANES = 8
LANES = 128
MXU_DIM = 256
VMEM_LIMIT_BYTES = 60 * 1024 * 1024
DOWN_CHUNKS = 4
NORM_ROWS = 2 * SUBLANES

SSM_CHUNKS = SSM_WIDTH // MXU_DIM
SSM_CHUNK_LANES = SSM_LANES // SSM_CHUNKS
SSM_TIME_TILE = 256
SSM_SEG = SSM_TIME_TILE // SUBLANES
SCAN_LANES = 4 * LANES


def _params(*sem):
    return pltpu.CompilerParams(dimension_semantics=sem, vmem_limit_bytes=VMEM_LIMIT_BYTES)


def _rms(x, g, half=False):
    scale = lax.rsqrt(jnp.mean(x * x, axis=-1, keepdims=True) + RMS_EPS)
    return (x * (0.5 * scale if half else scale)) * g


def _dot(a, b):
    return jnp.dot(a, b, preferred_element_type=F32)


def _cmul(a_re, a_im, b_re, b_im):
    return a_re * b_re - a_im * b_im, a_re * b_im + a_im * b_re


def _weight(w_ref, copy_ref=None):
    w = w_ref[...]
    if w.dtype != BF16:
        w = w.astype(BF16)
    if copy_ref is not None:
        copy_ref[...] = w
    return w


def _with_copies(specs, weights, emit_w):
    if not emit_w:
        return [], []
    return list(specs), [jax.ShapeDtypeStruct(w.shape, BF16) for w in weights]


def _col_tiled(w, tn):
    tiled = lambda k: pl.BlockSpec((None, k, tn), lambda i, j: (j, 0, 0))
    if w.dtype == BF16:
        assert w.ndim == 3 and w.shape[2] == tn, "bf16 weights come tile-major from an earlier call"
        return tiled(w.shape[1]), None, None
    k, n = w.shape
    return pl.BlockSpec((k, tn), lambda i, j: (0, j)), tiled(k), jax.ShapeDtypeStruct((n // tn, k, tn), BF16)


def _ffn_kernel(*refs, first, emit_w):
    refs = list(refs)
    x_ref, pre_ref, wg_ref, wu_ref, wd_ref, gpost_ref = refs[:6]
    del refs[:6]
    gnext_ref = refs.pop(0) if first else None
    o_ref = refs.pop(0)
    xnext_ref = refs.pop(0) if first else None
    wgb_ref, wub_ref, wdb_ref = (refs.pop(0), refs.pop(0), refs.pop(0)) if emit_w else (None, None, None)
    if first:
        xn_ref, = refs
    else:
        xn_ref = pre_ref
        xbuf_ref, xsem = refs
    i = pl.program_id(0)
    j = pl.program_id(1)
    tm = o_ref.shape[0]

    def x_copy():
        return pltpu.make_async_copy(x_ref.at[pl.ds(i * tm, tm), :], xbuf_ref, xsem)

    @pl.when(j == 0)
    def _():
        if first:
            xn_ref[...] = _rms(x_ref[...], pre_ref[...]).astype(BF16)
        else:
            x_copy().start()
        o_ref[...] = jnp.zeros_like(o_ref)

    xn = xn_ref[...]
    g = _dot(xn, _weight(wg_ref, wgb_ref))
    u = _dot(xn, _weight(wu_ref, wub_ref))
    hid = ((g * jax.nn.sigmoid(g)) * u).astype(BF16)
    wd = _weight(wd_ref, wdb_ref)
    cols = wd.shape[1] // DOWN_CHUNKS
    for n in range(DOWN_CHUNKS):
        sl = slice(n * cols, (n + 1) * cols)
        o_ref[:, sl] += _dot(hid, wd[:, sl])

    @pl.when(j == pl.num_programs(1) - 1)
    def _():
        if first:
            out = x_ref[...] + _rms(o_ref[...], gpost_ref[...], half=True)
            o_ref[...] = out
            xnext_ref[...] = _rms(out, gnext_ref[...]).astype(BF16)
        else:
            x_copy().wait()
            for r in range(tm // NORM_ROWS):
                rs = slice(r * NORM_ROWS, (r + 1) * NORM_ROWS)
                o_ref[rs, :] = xbuf_ref[rs, :] + _rms(o_ref[rs, :], gpost_ref[...], half=True)


def _ffn(x, pre, weights, g_post, tm, tf, g_next=None):
    m, d = x.shape
    w_gate, w_up, w_down = weights
    f = w_down.shape[0]
    first = g_next is not None
    emit_w = w_gate.dtype != BF16
    assert not emit_w or m == tm, "weight copies need every weight block visited exactly once"
    row_spec = pl.BlockSpec((tm, d), lambda i, j: (i, 0))
    gain_spec = pl.BlockSpec((1, d), lambda i, j: (0, 0))
    wg_spec, wg_copy, wg_shape = _col_tiled(w_gate, tf)
    wu_spec, wu_copy, wu_shape = _col_tiled(w_up, tf)
    wd_spec = pl.BlockSpec((tf, d), lambda i, j: (j, 0))
    w_specs = [wg_spec, wu_spec, wd_spec]
    copy_specs = [wg_copy, wu_copy, wd_spec] if emit_w else []
    copy_shapes = [wg_shape, wu_shape, jax.ShapeDtypeStruct(w_down.shape, BF16)] if emit_w else []
    outs = pl.pallas_call(
        functools.partial(_ffn_kernel, first=first, emit_w=emit_w),
        grid=(m // tm, f // tf),
        in_specs=([row_spec, gain_spec] if first else [pl.BlockSpec(memory_space=pl.ANY), row_spec])
        + w_specs + [gain_spec] + ([gain_spec] if first else []),
        out_specs=[row_spec] + ([row_spec] if first else []) + copy_specs,
        out_shape=[jax.ShapeDtypeStruct((m, d), F32)] + ([jax.ShapeDtypeStruct((m, d), BF16)] if first else [])
        + copy_shapes,
        scratch_shapes=[pltpu.VMEM((tm, d), BF16)] if first
        else [pltpu.VMEM((tm, d), F32), pltpu.SemaphoreType.DMA(())],
        compiler_params=_params("parallel", "arbitrary"),
        name="ffn",
    )(*((x, pre, w_gate, w_up, w_down, g_post) + ((g_next,) if first else ())))
    n_out = 2 if first else 1
    return outs[:n_out], (tuple(outs[n_out:]) if emit_w else tuple(weights))


def _proj_kernel(xn_ref, w_ref, o_ref, *copy_ref):
    o_ref[...] = _dot(xn_ref[...], _weight(w_ref, *copy_ref))


def _proj(xn, w, tm, tn):
    m, d = xn.shape
    emit_w = w.dtype != BF16
    assert not emit_w or m == tm
    n = w.shape[1] if emit_w else w.shape[0] * tn
    w_spec, copy_spec, copy_shape = _col_tiled(w, tn)
    copy_specs, copy_shapes = ([copy_spec], [copy_shape]) if emit_w else ([], [])
    outs = pl.pallas_call(
        _proj_kernel,
        grid=(m // tm, n // tn),
        in_specs=[pl.BlockSpec((tm, d), lambda i, j: (i, 0)), w_spec],
        out_specs=[pl.BlockSpec((tm, tn), lambda i, j: (i, j))] + copy_specs,
        out_shape=[jax.ShapeDtypeStruct((m, n), F32)] + copy_shapes,
        compiler_params=_params("parallel", "arbitrary"),
        name="proj",
    )(xn, w)
    return outs[0], (outs[1] if emit_w else w)


def _memproj_kernel(x_ref, g_ref, w_ref, o_ref, oh_ref):
    res = _dot(_rms(x_ref[...], g_ref[...]).astype(BF16), _weight(w_ref))
    o_ref[...] = res
    for h in range(XA_HEADS):
        for p in range(XA_LANE_SPLIT):
            lane0 = h * XA_HEAD_DIM + p * LANES
            oh_ref[:, p * XA_HEADS + h, :] = res[:, lane0:lane0 + LANES]


def _memproj(x, g, w):
    m, d = x.shape
    n = w.shape[1]
    assert n == XA_WIDTH
    tm = _row_tile(m, TILES['tm'])
    flat, by_head = pl.pallas_call(
        _memproj_kernel,
        grid=(m // tm,),
        in_specs=[pl.BlockSpec((tm, d), lambda i: (i, 0)), pl.BlockSpec((1, d), lambda i: (0, 0)),
                  pl.BlockSpec((d, n), lambda i: (0, 0))],
        out_specs=[pl.BlockSpec((tm, n), lambda i: (i, 0)), pl.BlockSpec((tm, SUBLANES, LANES), lambda i: (i, 0, 0))],
        out_shape=[jax.ShapeDtypeStruct((m, n), F32), jax.ShapeDtypeStruct((m, SUBLANES, LANES), F32)],
        compiler_params=_params("parallel"),
        name="memproj",
    )(x, g, w)
    return flat, _heads_from_sublanes(by_head)


def _pool_group_out(k, diff, wgrp_ref, scale_ref, z_ref):
    ch = slice(k * POOL_GROUP_DIM, (k + 1) * POOL_GROUP_DIM)
    zk = _dot(diff.astype(BF16), wgrp_ref[k]) * scale_ref[:, ch]
    z_ref[:, ch] = zk.astype(BF16)


def _pool_seq_kernel(u_ref, wgrp_ref, scale_ref, z_ref, ext_ref, *, halo):
    t = pl.program_id(1)
    tt = u_ref.shape[0]

    @pl.when(t == 0)
    def _():
        ext_ref[0:halo, :] = jnp.zeros((halo, POOL_WIDTH), F32)

    u = u_ref[...]
    ext_ref[halo:halo + tt, :] = u
    pos = t * tt + lax.broadcasted_iota(jnp.int32, (tt, 1), 0)
    for k, w in enumerate(POOL_WINDOWS):
        ch = slice(k * POOL_GROUP_DIM, (k + 1) * POOL_GROUP_DIM)
        a = ext_ref[:, ch]
        d = 1
        while d < w:
            a = a + pltpu.roll(a, d, axis=0)
            d *= 2
        count = jnp.minimum(pos + 1, w).astype(F32)
        diff = a[halo:, :] / count - u[:, ch]
        _pool_group_out(k, diff, wgrp_ref, scale_ref, z_ref)
    ext_ref[0:halo, :] = ext_ref[tt:tt + halo, :]


def _pool_seq(proj, w_grp, scale, batch, seq, tt):
    halo = 2 * SUBLANES
    nt = seq // tt
    return pl.pallas_call(
        functools.partial(_pool_seq_kernel, halo=halo),
        grid=(batch, nt),
        in_specs=[
            pl.BlockSpec((tt, POOL_WIDTH), lambda b, t: (b * nt + t, 0)),
            pl.BlockSpec(w_grp.shape, lambda b, t: (0, 0, 0)),
            pl.BlockSpec((1, POOL_WIDTH), lambda b, t: (0, 0)),
        ],
        out_specs=pl.BlockSpec((tt, POOL_WIDTH), lambda b, t: (b * nt + t, 0)),
        out_shape=jax.ShapeDtypeStruct((batch * seq, POOL_WIDTH), BF16),
        scratch_shapes=[pltpu.VMEM((tt + halo, POOL_WIDTH), F32)],
        compiler_params=_params("parallel", "arbitrary"),
        name="pool_seq",
    )(proj, w_grp, scale)


def _pool_step_kernel(u_ref, buf_ref, wgrp_ref, scale_ref, z_ref, nbuf_ref):
    u = u_ref[...]
    for k, w in enumerate(POOL_WINDOWS):
        ch = slice(k * POOL_GROUP_DIM, (k + 1) * POOL_GROUP_DIM)
        s = u[:, ch]
        for j in range(1, w):
            s = s + buf_ref[:, POOL_BUF - j, ch]
        diff = s / float(min(PAST_LEN + 1, w)) - u[:, ch]
        _pool_group_out(k, diff, wgrp_ref, scale_ref, z_ref)
    nbuf_ref[:, 0:POOL_BUF - 1, :] = buf_ref[:, 1:POOL_BUF, :]
    nbuf_ref[:, POOL_BUF - 1, :] = u


def _pool_step(proj, buf, w_grp, scale, tb):
    m = proj.shape[0]
    buf_spec = pl.BlockSpec((tb, POOL_BUF, POOL_WIDTH), lambda i: (i, 0, 0))
    return pl.pallas_call(
        _pool_step_kernel,
        grid=(m // tb,),
        in_specs=[
            pl.BlockSpec((tb, POOL_WIDTH), lambda i: (i, 0)),
            buf_spec,
            pl.BlockSpec(w_grp.shape, lambda i: (0, 0, 0)),
            pl.BlockSpec((1, POOL_WIDTH), lambda i: (0, 0)),
        ],
        out_specs=[pl.BlockSpec((tb, POOL_WIDTH), lambda i: (i, 0)), buf_spec],
        out_shape=[jax.ShapeDtypeStruct((m, POOL_WIDTH), BF16), jax.ShapeDtypeStruct(buf.shape, F32)],
        compiler_params=_params("parallel"),
        name="pool_step",
    )(proj, buf, w_grp, scale)


def _zoh_lambda(a_re, a_im, log_step):
    dt = jnp.exp(log_step)
    mag = jnp.exp(a_re * dt)
    ang = a_im * dt
    return mag * jnp.cos(ang), mag * jnp.sin(ang)


def _ssm_disc_kernel(a_re_ref, a_im_ref, ls_ref, a_re3_ref, a_im3_ref, ls3_ref, bt_ref, bb_ref, lb_ref, tab_ref):
    a_re = a_re3_ref[...]
    a_im = a_im3_ref[...]
    lb_re, lb_im = _zoh_lambda(a_re, a_im, ls3_ref[...])
    den = a_re * a_re + a_im * a_im
    n_re = lb_re - 1.0
    f_re = (n_re * a_re + lb_im * a_im) / den
    f_im = (lb_im * a_re - n_re * a_im) / den
    b_re = bt_ref[0]
    b_im = bt_ref[1]
    bb_ref[0] = f_re * b_re - f_im * b_im
    bb_ref[1] = f_re * b_im + f_im * b_re

    lb_re, lb_im = _zoh_lambda(a_re_ref[...], a_im_ref[...], ls_ref[...])
    lb_ref[0] = lb_re
    lb_ref[1] = lb_im
    s_re, s_im = lb_re, lb_im
    for _ in range(int(math.log2(SSM_SEG))):
        s_re, s_im = _cmul(s_re, s_im, s_re, s_im)
    powers = [(s_re, s_im)]
    for _ in range(1, SUBLANES):
        powers.append(_cmul(*powers[-1], s_re, s_im))
    zero = jnp.zeros_like(lb_re)
    for r in range(SUBLANES):
        tab_ref[0, r] = lb_re
        tab_ref[1, r] = lb_im
        for i in range(3):
            d = 1 << i
            tab_ref[2 + 2 * i, r] = powers[d - 1][0] if r >= d else zero
            tab_ref[3 + 2 * i, r] = powers[d - 1][1] if r >= d else zero
        tab_ref[8, r] = powers[r][0]
        tab_ref[9, r] = powers[r][1]


def _ssm_disc(a_re, a_im, log_step, b_re, b_im):
    g, n, h = b_re.shape
    bt = jnp.stack([b_re, b_im]).swapaxes(2, 3)
    bb, lb, tab = pl.pallas_call(
        _ssm_disc_kernel,
        out_shape=[jax.ShapeDtypeStruct((2, g, h, n), F32), jax.ShapeDtypeStruct((2, g, n), F32),
                   jax.ShapeDtypeStruct((10, SUBLANES, g, n), F32)],
        name="ssm_disc",
    )(a_re, a_im, log_step[:, None], a_re[:, None, :], a_im[:, None, :], log_step[:, None, None], bt)
    return bb, lb.reshape(2, g * n), tab.reshape(10, SUBLANES, g * n)


def _ssm_in(ub, bd_ref, x_ref):
    for c in range(SSM_CHUNKS):
        x_ref[:, c * SSM_CHUNK_LANES:(c + 1) * SSM_CHUNK_LANES] = _dot(
            ub[:, c * MXU_DIM:(c + 1) * MXU_DIM], bd_ref[c])


def _ssm_out(c, u, x_re_ref, x_im_ref, cdr_ref, cdi_ref, d_ref):
    ch = slice(c * MXU_DIM, (c + 1) * MXU_DIM)
    st = slice(c * SSM_CHUNK_LANES, (c + 1) * SSM_CHUNK_LANES)
    y = _dot(x_re_ref[:, st].astype(BF16), cdr_ref[c]) - _dot(x_im_ref[:, st].astype(BF16), cdi_ref[c])
    return jax.nn.gelu(y + d_ref[:, ch] * u[:, ch])


def _ssm_seq_kernel(u_ref, h_re_ref, h_im_ref, tab_ref, bdr_ref, bdi_ref, cdr_ref, cdi_ref, d_ref,
                    g_ref, last_re_ref, last_im_ref, xr_ref, xi_ref, cr_ref, ci_ref):
    t = pl.program_id(1)
    tt = u_ref.shape[0]
    seg = tt // SUBLANES

    @pl.when(t == 0)
    def _():
        cr_ref[...] = jnp.broadcast_to(h_re_ref[...], cr_ref.shape)
        ci_ref[...] = jnp.broadcast_to(h_im_ref[...], ci_ref.shape)

    def regroup(x, rows_major, rows_minor):
        w = x.shape[1]
        return jnp.swapaxes(x.reshape(rows_major, rows_minor, w), 0, 1).reshape(tt, w)

    u = regroup(u_ref[...], SUBLANES, seg)
    ub = u.astype(BF16)
    _ssm_in(ub, bdr_ref, xr_ref)
    _ssm_in(ub, bdi_ref, xi_ref)

    first_row = lax.broadcasted_iota(jnp.int32, (SUBLANES, SCAN_LANES), 0) == 0
    for lc in range(SSM_LANES // SCAN_LANES):
        sl = slice(lc * SCAN_LANES, (lc + 1) * SCAN_LANES)
        l_re = tab_ref[0, :, sl]
        l_im = tab_ref[1, :, sl]

        def rows(j):
            return pl.ds(j * SUBLANES, SUBLANES)

        def advance(j, x, sl=sl, l_re=l_re, l_im=l_im):
            x_re, x_im = _cmul(l_re, l_im, *x)
            x_re = x_re + xr_ref[rows(j), sl]
            x_im = x_im + xi_ref[rows(j), sl]
            xr_ref[rows(j), sl] = x_re
            xi_ref[rows(j), sl] = x_im
            return x_re, x_im

        zero = jnp.zeros((SUBLANES, SCAN_LANES), F32)
        f_re, f_im = zero, zero
        for j in range(seg):
            f_re, f_im = advance(j, (f_re, f_im))

        c_re = cr_ref[:, sl]
        c_im = ci_ref[:, sl]
        for i in range(3):
            s_re = pltpu.roll(f_re, 1 << i, axis=0)
            s_im = pltpu.roll(f_im, 1 << i, axis=0)
            m_re, m_im = _cmul(tab_ref[2 + 2 * i, :, sl], tab_ref[3 + 2 * i, :, sl], s_re, s_im)
            f_re, f_im = f_re + m_re, f_im + m_im
        m_re, m_im = _cmul(tab_ref[8, :, sl], tab_ref[9, :, sl], c_re, c_im)
        f_re, f_im = f_re + m_re, f_im + m_im
        e_re = jnp.where(first_row, c_re, pltpu.roll(f_re, 1, axis=0))
        e_im = jnp.where(first_row, c_im, pltpu.roll(f_im, 1, axis=0))
        cr_ref[:, sl] = jnp.broadcast_to(f_re[SUBLANES - 1:SUBLANES, :], (SUBLANES, SCAN_LANES))
        ci_ref[:, sl] = jnp.broadcast_to(f_im[SUBLANES - 1:SUBLANES, :], (SUBLANES, SCAN_LANES))

        def fix(j, w, sl=sl, l_re=l_re, l_im=l_im):
            w_re, w_im = _cmul(l_re, l_im, *w)
            xr_ref[rows(j), sl] += w_re
            xi_ref[rows(j), sl] += w_im
            return w_re, w_im

        w = (e_re, e_im)
        for j in range(seg):
            w = fix(j, w)

    for c in range(SSM_CHUNKS):
        ch = slice(c * MXU_DIM, (c + 1) * MXU_DIM)
        g = _ssm_out(c, u, xr_ref, xi_ref, cdr_ref, cdi_ref, d_ref)
        g_ref[:, ch] = regroup(g, seg, SUBLANES).astype(BF16)

    @pl.when(t == pl.num_programs(1) - 1)
    def _():
        last_re_ref[...] = cr_ref[0:1, :]
        last_im_ref[...] = ci_ref[0:1, :]


def _const_spec(shape):
    return pl.BlockSpec(shape, lambda *_: (0,) * len(shape))


def _part_spec(stacked, part):
    rest = stacked.shape[1:]
    return pl.BlockSpec((None,) + rest, lambda *_: (part,) + (0,) * len(rest))


def _ssm_seq(proj, h_re, h_im, tab, bd, cd, d_skip, batch, seq):
    tt = SSM_TIME_TILE
    nt = seq // tt
    col_block = POOL_WIDTH // SSM_WIDTH
    state_spec = pl.BlockSpec((None, 1, SSM_LANES), lambda b, t: (b, 0, 0))
    state_shape = jax.ShapeDtypeStruct((batch, 1, SSM_LANES), F32)
    return pl.pallas_call(
        _ssm_seq_kernel,
        grid=(batch, nt),
        in_specs=[
            pl.BlockSpec((tt, SSM_WIDTH), lambda b, t: (b * nt + t, col_block)),
            state_spec, state_spec,
            _const_spec(tab.shape),
            _part_spec(bd, 0), _part_spec(bd, 1), _part_spec(cd, 0), _part_spec(cd, 1),
            _const_spec(d_skip.shape),
        ],
        out_specs=[pl.BlockSpec((tt, SSM_WIDTH), lambda b, t: (b * nt + t, 0)), state_spec, state_spec],
        out_shape=[jax.ShapeDtypeStruct((batch * seq, SSM_WIDTH), BF16), state_shape, state_shape],
        scratch_shapes=[
            pltpu.VMEM((tt, SSM_LANES), F32), pltpu.VMEM((tt, SSM_LANES), F32),
            pltpu.VMEM((SUBLANES, SSM_LANES), F32), pltpu.VMEM((SUBLANES, SSM_LANES), F32),
        ],
        compiler_params=_params("parallel", "arbitrary"),
        name="ssm_seq",
    )(proj, h_re, h_im, tab, bd, bd, cd, cd, d_skip)


def _ssm_step_kernel(u_ref, h_re_ref, h_im_ref, lb_ref, bdr_ref, bdi_ref, cdr_ref, cdi_ref, d_ref,
                     g_ref, x_re_ref, x_im_ref):
    u = u_ref[...]
    ub = u.astype(BF16)
    _ssm_in(ub, bdr_ref, x_re_ref)
    _ssm_in(ub, bdi_ref, x_im_ref)
    m_re, m_im = _cmul(lb_ref[0:1, :], lb_ref[1:2, :], h_re_ref[...], h_im_ref[...])
    x_re_ref[...] += m_re
    x_im_ref[...] += m_im
    for c in range(SSM_CHUNKS):
        ch = slice(c * MXU_DIM, (c + 1) * MXU_DIM)
        g_ref[:, ch] = _ssm_out(c, u, x_re_ref, x_im_ref, cdr_ref, cdi_ref, d_ref).astype(BF16)


def _ssm_step(proj, h_re, h_im, lb, bd, cd, d_skip):
    m = proj.shape[0]
    state_spec = pl.BlockSpec((m, SSM_LANES), lambda i: (0, 0))
    state_shape = jax.ShapeDtypeStruct((m, SSM_LANES), F32)
    return pl.pallas_call(
        _ssm_step_kernel,
        grid=(1,),
        in_specs=[
            pl.BlockSpec((m, SSM_WIDTH), lambda i: (0, POOL_WIDTH // SSM_WIDTH)),
            state_spec, state_spec,
            _const_spec(lb.shape),
            _part_spec(bd, 0), _part_spec(bd, 1), _part_spec(cd, 0), _part_spec(cd, 1),
            _const_spec(d_skip.shape),
        ],
        out_specs=[pl.BlockSpec((m, SSM_WIDTH), lambda i: (0, 0)), state_spec, state_spec],
        out_shape=[jax.ShapeDtypeStruct((m, SSM_WIDTH), BF16), state_shape, state_shape],
        compiler_params=_params("arbitrary"),
        name="ssm_step",
    )(proj, h_re, h_im, lb, bd, bd, cd, cd, d_skip)


def _xa_seq_kernel(q_ref, k_ref, v_ref, o_ref):
    for h in range(XA_HEADS):
        ch = slice(h * XA_HEAD_DIM, (h + 1) * XA_HEAD_DIM)
        q = q_ref[:, ch].astype(BF16)
        k = k_ref[:, ch].astype(BF16)
        v = v_ref[:, ch].astype(BF16)
        s = lax.dot_general(q, k, (((1,), (1,)), ((), ())), preferred_element_type=F32) * XA_SCALE
        e = jnp.exp(s - jnp.max(s, axis=-1, keepdims=True))
        p = e / jnp.sum(e, axis=-1, keepdims=True)
        o_ref[:, ch] = _dot(p.astype(BF16), v).astype(BF16)


def _xa_seq(proj, mem_k, mem_v, batch, seq, tq):
    nt = seq // tq
    q_block = (POOL_WIDTH + SSM_WIDTH) // XA_WIDTH
    kv_spec = pl.BlockSpec((N_MEM, XA_WIDTH), lambda b, t: (b, 0))
    return pl.pallas_call(
        _xa_seq_kernel,
        grid=(batch, nt),
        in_specs=[pl.BlockSpec((tq, XA_WIDTH), lambda b, t: (b * nt + t, q_block)), kv_spec, kv_spec],
        out_specs=pl.BlockSpec((tq, XA_WIDTH), lambda b, t: (b * nt + t, 0)),
        out_shape=jax.ShapeDtypeStruct((batch * seq, XA_WIDTH), BF16),
        compiler_params=_params("parallel", "parallel"),
        name="xa_seq",
    )(proj, mem_k, mem_v)


XA_LANE_SPLIT = XA_HEAD_DIM // LANES


def _heads_on_sublanes(x):
    lead = x.shape[:-2]
    x = x.reshape(lead + (XA_HEADS, XA_LANE_SPLIT, LANES)).swapaxes(-3, -2)
    return x.reshape(lead + (XA_LANE_SPLIT * XA_HEADS, LANES))


def _heads_from_sublanes(x):
    lead = x.shape[:-2]
    x = x.reshape(lead + (XA_LANE_SPLIT, XA_HEADS, LANES)).swapaxes(-3, -2)
    return x.reshape(lead + (XA_HEADS, XA_HEAD_DIM))


def _xa_step_kernel(q_ref, k_ref, v_ref, o_ref):
    q = q_ref[...]
    part = jnp.sum(k_ref[...] * q, axis=-1, keepdims=True)
    s = (part + pltpu.roll(part, XA_HEADS, axis=2)) * XA_SCALE
    e = jnp.exp(s - jnp.max(s, axis=1, keepdims=True))
    r = 1.0 / jnp.sum(e, axis=1, keepdims=True)
    o_ref[...] = jnp.sum(e * v_ref[...], axis=1, keepdims=True) * r


def _xa_step(q4, cache_k, cache_v, tb):
    assert XA_LANE_SPLIT * XA_HEADS == SUBLANES
    m = q4.shape[0]
    kv_spec = pl.BlockSpec((tb, N_MEM, SUBLANES, LANES), lambda i: (i, 0, 0, 0))
    q_spec = pl.BlockSpec((tb, 1, SUBLANES, LANES), lambda i: (i, 0, 0, 0))
    o = pl.pallas_call(
        _xa_step_kernel,
        grid=(m // tb,),
        in_specs=[q_spec, kv_spec, kv_spec],
        out_specs=q_spec,
        out_shape=jax.ShapeDtypeStruct((m, 1, SUBLANES, LANES), F32),
        compiler_params=_params("parallel"),
        name="xa_step",
    )(_heads_on_sublanes(q4), _heads_on_sublanes(cache_k), _heads_on_sublanes(cache_v))
    return _heads_from_sublanes(o)


def _gate_kernel(z_ref, g_ref, o_ref, ga_ref, gb_ref, gc_ref, wpo_ref, wv_ref, wg_ref, wxo_ref, m_ref, *copy_refs):
    copies = copy_refs if copy_refs else (None,) * 4
    g = g_ref[...]
    o_pool = _dot(z_ref[...], _weight(wpo_ref, copies[0]))
    o_ssm = _dot(g, _weight(wv_ref, copies[1])) * jax.nn.sigmoid(_dot(g, _weight(wg_ref, copies[2])))
    o_xa = _dot(o_ref[...], _weight(wxo_ref, copies[3]))
    merged = (jax.nn.sigmoid(ga_ref[...]) * o_pool + jax.nn.sigmoid(gb_ref[...]) * o_ssm
              + jax.nn.sigmoid(gc_ref[...]) * o_xa)
    m_ref[...] = merged.astype(BF16)


def _gate(z, g, o, proj, weights, tm, tn):
    m, half = z.shape
    emit_w = weights[0].dtype != BF16
    assert not emit_w or m == tm
    d = weights[0].shape[1] if emit_w else weights[0].shape[0] * tn
    gate0 = OFF_GATE // tn
    per_gate = d // tn
    branch_spec = pl.BlockSpec((tm, half), lambda i, n: (i, 0))
    w_specs, copy_specs, copy_shapes = map(list, zip(*[_col_tiled(w, tn) for w in weights]))
    if not emit_w:
        copy_specs, copy_shapes = [], []

    def gate_spec(k):
        return pl.BlockSpec((tm, tn), lambda i, n: (i, gate0 + k * per_gate + n))

    outs = pl.pallas_call(
        _gate_kernel,
        grid=(m // tm, d // tn),
        in_specs=[branch_spec] * 3 + [gate_spec(0), gate_spec(1), gate_spec(2)] + w_specs,
        out_specs=[pl.BlockSpec((tm, tn), lambda i, n: (i, n))] + copy_specs,
        out_shape=[jax.ShapeDtypeStruct((m, d), BF16)] + copy_shapes,
        compiler_params=_params("parallel", "arbitrary"),
        name="gate",
    )(z, g, o, proj, proj, proj, *weights)
    return outs[0], (tuple(outs[1:]) if emit_w else tuple(weights))


def _mixout_kernel(m_ref, wout_ref, h_ref, gpost_ref, gnext_ref, h2_ref, xn_ref, *copy_ref, sub):
    wout = _weight(wout_ref, *copy_ref)
    for r in range(m_ref.shape[0] // sub):
        rows = slice(r * sub, (r + 1) * sub)
        h2 = h_ref[rows, :] + _rms(_dot(m_ref[rows, :], wout), gpost_ref[...])
        h2_ref[rows, :] = h2
        xn_ref[rows, :] = _rms(h2, gnext_ref[...]).astype(BF16)


def _mixout(merged, w_out, h, g_post, g_next, tm):
    m, d = h.shape
    emit_w = w_out.dtype != BF16
    assert not emit_w or m == tm
    row_spec = pl.BlockSpec((tm, d), lambda i: (i, 0))
    gain_spec = pl.BlockSpec((1, d), lambda i: (0, 0))
    w_spec = pl.BlockSpec((d, d), lambda i: (0, 0))
    copy_specs, copy_shapes = _with_copies([w_spec], [w_out], emit_w)
    outs = pl.pallas_call(
        functools.partial(_mixout_kernel, sub=min(tm, MXU_DIM)),
        grid=(m // tm,),
        in_specs=[row_spec, w_spec, row_spec, gain_spec, gain_spec],
        out_specs=[row_spec, row_spec] + copy_specs,
        out_shape=[jax.ShapeDtypeStruct((m, d), F32), jax.ShapeDtypeStruct((m, d), BF16)] + copy_shapes,
        compiler_params=_params("parallel"),
        name="mixout",
    )(merged, w_out, h, g_post, g_next)
    return outs[:2], (outs[2] if emit_w else w_out)


def _block_diag(x):
    gpc = SSM_GROUPS // SSM_CHUNKS
    _, _, a, b = x.shape
    x = x.reshape(2, SSM_CHUNKS, gpc, a, b)
    eye = jnp.eye(gpc, dtype=x.dtype)
    y = x[:, :, :, :, None, :] * eye[None, None, :, None, :, None]
    return y.reshape(2, SSM_CHUNKS, gpc * a, gpc * b)


def _row_tile(m, target):
    return target if m % target == 0 else m


FFN1_W = ('w_ff1_gate', 'w_ff1_up', 'w_ff1_down')
FFN2_W = ('w_ff2_gate', 'w_ff2_up', 'w_ff2_down')
GATE_W = ('w_pool_out', 'w_glu_val', 'w_glu_gate', 'w_xa_out')
TILES = dict(tm=512, tm_ffn2=1024, tf=512, tm_proj=2048, tm_gate=1024, tn=512)


def _layer(x, w, mixers):
    m = x.shape[0]
    tm = _row_tile(m, TILES['tm'])
    wb = dict(w)
    (h, xn), wb_ffn1 = _ffn(x, w['g_ff1_pre'], [w[k] for k in FFN1_W], w['g_ff1_post'], tm, TILES['tf'],
                            g_next=w['g_mix_pre'])
    proj, wb['w_in'] = _proj(xn, w['w_in'], _row_tile(m, TILES['tm_proj']), TILES['tn'])
    z, g, o, states = mixers(proj)
    merged, wb_gate = _gate(z, g, o, proj, [w[k] for k in GATE_W], _row_tile(m, TILES['tm_gate']), TILES['tn'])
    (h, xn), wb['w_out'] = _mixout(merged, w['w_out'], h, w['g_mix_post'], w['g_ff2_pre'], tm)
    (y,), wb_ffn2 = _ffn(h, xn, [w[k] for k in FFN2_W], w['g_ff2_post'], _row_tile(m, TILES['tm_ffn2']),
                         TILES['tf'])
    wb.update(zip(FFN1_W + GATE_W + FFN2_W, wb_ffn1 + wb_gate + wb_ffn2))
    return y, states, wb


def kernel(x_prompt, x_sample, mem_prompt, cache_mem_k, cache_mem_v, state_pool, state_ssm_re, state_ssm_im, g_ff1_pre, w_ff1_gate, w_ff1_up, w_ff1_down, g_ff1_post, g_mix_pre, w_in, w_pool_grp, pool_scale, w_pool_out, ssm_a_re, ssm_a_im, ssm_log_step, ssm_b_re, ssm_b_im, ssm_c_re, ssm_c_im, ssm_d, w_glu_val, w_glu_gate, g_mem, w_mem_k, w_mem_v, w_xa_out, w_out, g_mix_post, g_ff2_pre, w_ff2_gate, w_ff2_up, w_ff2_down, g_ff2_post):
    bp, lp, d = x_prompt.shape
    bs, ls, _ = x_sample.shape
    depth = w_in.shape[0]
    assert depth == 1 and ls == 1 and lp >= POOL_BUF and lp % SSM_TIME_TILE == 0

    mats = dict(w_ff1_gate=w_ff1_gate, w_ff1_up=w_ff1_up, w_ff1_down=w_ff1_down, w_in=w_in,
                w_pool_out=w_pool_out, w_glu_val=w_glu_val, w_glu_gate=w_glu_gate, w_xa_out=w_xa_out,
                w_out=w_out, w_ff2_gate=w_ff2_gate, w_ff2_up=w_ff2_up, w_ff2_down=w_ff2_down)
    w = {k: v[0] for k, v in mats.items()}
    w.update(g_ff1_pre=g_ff1_pre, g_ff1_post=g_ff1_post, g_mix_pre=g_mix_pre, g_mix_post=g_mix_post,
             g_ff2_pre=g_ff2_pre, g_ff2_post=g_ff2_post)
    w_grp = w_pool_grp[0].astype(BF16)
    scale = pool_scale
    d_skip = ssm_d

    bb_t, lb, tab = _ssm_disc(ssm_a_re[0], ssm_a_im[0], ssm_log_step[0], ssm_b_re[0], ssm_b_im[0])
    bd = _block_diag(bb_t.astype(BF16))
    cd = _block_diag(jnp.stack([ssm_c_re[0], ssm_c_im[0]]).astype(BF16).swapaxes(2, 3))

    def sample_mixers(proj):
        z, new_buf = _pool_step(proj, state_pool[0], w_grp, scale, 32)
        g, x_re, x_im = _ssm_step(proj, state_ssm_re[0].reshape(bs, SSM_LANES),
                                  state_ssm_im[0].reshape(bs, SSM_LANES), lb, bd, cd, d_skip)
        q4 = proj[:, POOL_WIDTH + SSM_WIDTH:OFF_GATE].reshape(bs, 1, XA_HEADS, XA_HEAD_DIM)
        o = _xa_step(q4, cache_mem_k[0], cache_mem_v[0], 4)
        return z, g, o.reshape(bs, XA_WIDTH).astype(BF16), (new_buf, x_re, x_im)

    y_sample, (pool_s, re_s, im_s), wb = _layer(x_sample.reshape(bs * ls, d), w, sample_mixers)

    mem = mem_prompt.reshape(bp * N_MEM, d)
    mem_k, mem_k_heads = _memproj(mem, g_mem, w_mem_k[0])
    mem_v, mem_v_heads = _memproj(mem, g_mem, w_mem_v[0])

    def prompt_mixers(proj):
        z = _pool_seq(proj, w_grp, scale, bp, lp, 512)
        zeros = jnp.zeros((bp, 1, SSM_LANES), F32)
        g, last_re, last_im = _ssm_seq(proj, zeros, zeros, tab, bd, cd, d_skip, bp, lp)
        o = _xa_seq(proj, mem_k, mem_v, bp, lp, 512)
        pool_rows = proj.reshape(bp, lp, IN_WIDTH)[:, lp - POOL_BUF:, :POOL_WIDTH]
        return z, g, o, (pool_rows, last_re, last_im)

    y_prompt, (pool_p, re_p, im_p), _ = _layer(x_prompt.reshape(bp * lp, d), wb, prompt_mixers)

    kv_shape = (1, bp, N_MEM, XA_HEADS, XA_HEAD_DIM)
    st_shape = (SSM_GROUPS, SSM_STATE)
    return (y_prompt.reshape(bp, lp, d), y_sample.reshape(bs, ls, d),
            mem_k_heads.reshape(kv_shape), mem_v_heads.reshape(kv_shape),
            pool_p[None],
            re_p.reshape((1, bp) + st_shape), im_p.reshape((1, bp) + st_shape),
            pool_s[None],
            re_s.reshape((1, bs) + st_shape), im_s.reshape((1, bs) + st_shape))
```

```python
import functools
import math

import jax
import jax.numpy as jnp
from jax import lax
from jax.experimental import pallas as pl
from jax.experimental.pallas import tpu as pltpu

F32 = jnp.float32
BF16 = jnp.bfloat16

D_MODEL = 2048
PAST_LEN = 16384
POOL_WIDTH = D_MODEL // 2
POOL_WINDOWS = (2, 4, 8, 16)
POOL_GROUP_DIM = POOL_WIDTH // len(POOL_WINDOWS)
POOL_BUF = max(POOL_WINDOWS) - 1
SSM_WIDTH = D_MODEL // 2
SSM_GROUP_CH = 16
SSM_GROUPS = SSM_WIDTH // SSM_GROUP_CH
SSM_STATE = 64
SSM_LANES = SSM_GROUPS * SSM_STATE
XA_HEADS = 4
XA_HEAD_DIM = D_MODEL // 8
XA_WIDTH = XA_HEADS * XA_HEAD_DIM
XA_SCALE = XA_HEAD_DIM ** -0.5
N_MEM = 256
OFF_GATE = POOL_WIDTH + SSM_WIDTH + XA_WIDTH
IN_WIDTH = OFF_GATE + 3 * D_MODEL
RMS_EPS = 1e-6

SUBLANES = 8
LANES = 128
MXU_DIM = 256
VMEM_LIMIT_BYTES = 60 * 1024 * 1024
DOWN_CHUNKS = 4
NORM_ROWS = 2 * SUBLANES

SSM_CHUNKS = SSM_WIDTH // MXU_DIM
SSM_CHUNK_LANES = SSM_LANES // SSM_CHUNKS
SSM_TIME_TILE = 256
SSM_SEG = SSM_TIME_TILE // SUBLANES
SCAN_LANES = 4 * LANES


def _params(*sem):
    return pltpu.CompilerParams(dimension_semantics=sem, vmem_limit_bytes=VMEM_LIMIT_BYTES)


def _rms(x, g, half=False):
    scale = lax.rsqrt(jnp.mean(x * x, axis=-1, keepdims=True) + RMS_EPS)
    return (x * (0.5 * scale if half else scale)) * g


def _dot(a, b):
    return jnp.dot(a, b, preferred_element_type=F32)


def _cmul(a_re, a_im, b_re, b_im):
    return a_re * b_re - a_im * b_im, a_re * b_im + a_im * b_re


def _weight(w_ref, copy_ref=None):
    w = w_ref[...]
    if w.dtype != BF16:
        w = w.astype(BF16)
    if copy_ref is not None:
        copy_ref[...] = w
    return w


def _with_copies(specs, weights, emit_w):
    if not emit_w:
        return [], []
    return list(specs), [jax.ShapeDtypeStruct(w.shape, BF16) for w in weights]


def _col_tiled(w, tn):
    tiled = lambda k: pl.BlockSpec((None, k, tn), lambda i, j: (j, 0, 0))
    if w.dtype == BF16:
        assert w.ndim == 3 and w.shape[2] == tn, "bf16 weights come tile-major from an earlier call"
        return tiled(w.shape[1]), None, None
    k, n = w.shape
    return pl.BlockSpec((k, tn), lambda i, j: (0, j)), tiled(k), jax.ShapeDtypeStruct((n // tn, k, tn), BF16)


def _ffn_kernel(*refs, first, emit_w):
    refs = list(refs)
    x_ref, pre_ref, wg_ref, wu_ref, wd_ref, gpost_ref = refs[:6]
    del refs[:6]
    gnext_ref = refs.pop(0) if first else None
    o_ref = refs.pop(0)
    xnext_ref = refs.pop(0) if first else None
    wgb_ref, wub_ref, wdb_ref = (refs.pop(0), refs.pop(0), refs.pop(0)) if emit_w else (None, None, None)
    if first:
        xn_ref, = refs
    else:
        xn_ref = pre_ref
        xbuf_ref, xsem = refs
    i = pl.program_id(0)
    j = pl.program_id(1)
    tm = o_ref.shape[0]

    def x_copy():
        return pltpu.make_async_copy(x_ref.at[pl.ds(i * tm, tm), :], xbuf_ref, xsem)

    @pl.when(j == 0)
    def _():
        if first:
            xn_ref[...] = _rms(x_ref[...], pre_ref[...]).astype(BF16)
        else:
            x_copy().start()
        o_ref[...] = jnp.zeros_like(o_ref)

    xn = xn_ref[...]
    g = _dot(xn, _weight(wg_ref, wgb_ref))
    u = _dot(xn, _weight(wu_ref, wub_ref))
    hid = ((g * jax.nn.sigmoid(g)) * u).astype(BF16)
    wd = _weight(wd_ref, wdb_ref)
    cols = wd.shape[1] // DOWN_CHUNKS
    for n in range(DOWN_CHUNKS):
        sl = slice(n * cols, (n + 1) * cols)
        o_ref[:, sl] += _dot(hid, wd[:, sl])

    @pl.when(j == pl.num_programs(1) - 1)
    def _():
        if first:
            out = x_ref[...] + _rms(o_ref[...], gpost_ref[...], half=True)
            o_ref[...] = out
            xnext_ref[...] = _rms(out, gnext_ref[...]).astype(BF16)
        else:
            x_copy().wait()
            for r in range(tm // NORM_ROWS):
                rs = slice(r * NORM_ROWS, (r + 1) * NORM_ROWS)
                o_ref[rs, :] = xbuf_ref[rs, :] + _rms(o_ref[rs, :], gpost_ref[...], half=True)


def _ffn(x, pre, weights, g_post, tm, tf, g_next=None):
    m, d = x.shape
    w_gate, w_up, w_down = weights
    f = w_down.shape[0]
    first = g_next is not None
    emit_w = w_gate.dtype != BF16
    assert not emit_w or m == tm, "weight copies need every weight block visited exactly once"
    row_spec = pl.BlockSpec((tm, d), lambda i, j: (i, 0))
    gain_spec = pl.BlockSpec((1, d), lambda i, j: (0, 0))
    wg_spec, wg_copy, wg_shape = _col_tiled(w_gate, tf)
    wu_spec, wu_copy, wu_shape = _col_tiled(w_up, tf)
    wd_spec = pl.BlockSpec((tf, d), lambda i, j: (j, 0))
    w_specs = [wg_spec, wu_spec, wd_spec]
    copy_specs = [wg_copy, wu_copy, wd_spec] if emit_w else []
    copy_shapes = [wg_shape, wu_shape, jax.ShapeDtypeStruct(w_down.shape, BF16)] if emit_w else []
    outs = pl.pallas_call(
        functools.partial(_ffn_kernel, first=first, emit_w=emit_w),
        grid=(m // tm, f // tf),
        in_specs=([row_spec, gain_spec] if first else [pl.BlockSpec(memory_space=pl.ANY), row_spec])
        + w_specs + [gain_spec] + ([gain_spec] if first else []),
        out_specs=[row_spec] + ([row_spec] if first else []) + copy_specs,
        out_shape=[jax.ShapeDtypeStruct((m, d), F32)] + ([jax.ShapeDtypeStruct((m, d), BF16)] if first else [])
        + copy_shapes,
        scratch_shapes=[pltpu.VMEM((tm, d), BF16)] if first
        else [pltpu.VMEM((tm, d), F32), pltpu.SemaphoreType.DMA(())],
        compiler_params=_params("parallel", "arbitrary"),
        name="ffn",
    )(*((x, pre, w_gate, w_up, w_down, g_post) + ((g_next,) if first else ())))
    n_out = 2 if first else 1
    return outs[:n_out], (tuple(outs[n_out:]) if emit_w else tuple(weights))


def _proj_kernel(xn_ref, w_ref, o_ref, *copy_ref):
    o_ref[...] = _dot(xn_ref[...], _weight(w_ref, *copy_ref))


def _proj(xn, w, tm, tn):
    m, d = xn.shape
    emit_w = w.dtype != BF16
    assert not emit_w or m == tm
    n = w.shape[1] if emit_w else w.shape[0] * tn
    w_spec, copy_spec, copy_shape = _col_tiled(w, tn)
    copy_specs, copy_shapes = ([copy_spec], [copy_shape]) if emit_w else ([], [])
    outs = pl.pallas_call(
        _proj_kernel,
        grid=(m // tm, n // tn),
        in_specs=[pl.BlockSpec((tm, d), lambda i, j: (i, 0)), w_spec],
        out_specs=[pl.BlockSpec((tm, tn), lambda i, j: (i, j))] + copy_specs,
        out_shape=[jax.ShapeDtypeStruct((m, n), F32)] + copy_shapes,
        compiler_params=_params("parallel", "arbitrary"),
        name="proj",
    )(xn, w)
    return outs[0], (outs[1] if emit_w else w)


def _memproj_kernel(x_ref, g_ref, w_ref, o_ref, oh_ref):
    res = _dot(_rms(x_ref[...], g_ref[...]).astype(BF16), _weight(w_ref))
    o_ref[...] = res
    for h in range(XA_HEADS):
        for p in range(XA_LANE_SPLIT):
            lane0 = h * XA_HEAD_DIM + p * LANES
            oh_ref[:, p * XA_HEADS + h, :] = res[:, lane0:lane0 + LANES]


def _memproj(x, g, w):
    m, d = x.shape
    n = w.shape[1]
    assert n == XA_WIDTH
    tm = _row_tile(m, TILES['tm'])
    flat, by_head = pl.pallas_call(
        _memproj_kernel,
        grid=(m // tm,),
        in_specs=[pl.BlockSpec((tm, d), lambda i: (i, 0)), pl.BlockSpec((1, d), lambda i: (0, 0)),
                  pl.BlockSpec((d, n), lambda i: (0, 0))],
        out_specs=[pl.BlockSpec((tm, n), lambda i: (i, 0)), pl.BlockSpec((tm, SUBLANES, LANES), lambda i: (i, 0, 0))],
        out_shape=[jax.ShapeDtypeStruct((m, n), F32), jax.ShapeDtypeStruct((m, SUBLANES, LANES), F32)],
        compiler_params=_params("parallel"),
        name="memproj",
    )(x, g, w)
    return flat, _heads_from_sublanes(by_head)


def _pool_group_out(k, diff, wgrp_ref, scale_ref, z_ref):
    ch = slice(k * POOL_GROUP_DIM, (k + 1) * POOL_GROUP_DIM)
    zk = _dot(diff.astype(BF16), wgrp_ref[k]) * scale_ref[:, ch]
    z_ref[:, ch] = zk.astype(BF16)


def _pool_seq_kernel(u_ref, wgrp_ref, scale_ref, z_ref, ext_ref, *, halo):
    t = pl.program_id(1)
    tt = u_ref.shape[0]

    @pl.when(t == 0)
    def _():
        ext_ref[0:halo, :] = jnp.zeros((halo, POOL_WIDTH), F32)

    u = u_ref[...]
    ext_ref[halo:halo + tt, :] = u
    pos = t * tt + lax.broadcasted_iota(jnp.int32, (tt, 1), 0)
    for k, w in enumerate(POOL_WINDOWS):
        ch = slice(k * POOL_GROUP_DIM, (k + 1) * POOL_GROUP_DIM)
        a = ext_ref[:, ch]
        d = 1
        while d < w:
            a = a + pltpu.roll(a, d, axis=0)
            d *= 2
        count = jnp.minimum(pos + 1, w).astype(F32)
        diff = a[halo:, :] / count - u[:, ch]
        _pool_group_out(k, diff, wgrp_ref, scale_ref, z_ref)
    ext_ref[0:halo, :] = ext_ref[tt:tt + halo, :]


def _pool_seq(proj, w_grp, scale, batch, seq, tt):
    halo = 2 * SUBLANES
    nt = seq // tt
    return pl.pallas_call(
        functools.partial(_pool_seq_kernel, halo=halo),
        grid=(batch, nt),
        in_specs=[
            pl.BlockSpec((tt, POOL_WIDTH), lambda b, t: (b * nt + t, 0)),
            pl.BlockSpec(w_grp.shape, lambda b, t: (0, 0, 0)),
            pl.BlockSpec((1, POOL_WIDTH), lambda b, t: (0, 0)),
        ],
        out_specs=pl.BlockSpec((tt, POOL_WIDTH), lambda b, t: (b * nt + t, 0)),
        out_shape=jax.ShapeDtypeStruct((batch * seq, POOL_WIDTH), BF16),
        scratch_shapes=[pltpu.VMEM((tt + halo, POOL_WIDTH), F32)],
        compiler_params=_params("parallel", "arbitrary"),
        name="pool_seq",
    )(proj, w_grp, scale)


def _pool_step_kernel(u_ref, buf_ref, wgrp_ref, scale_ref, z_ref, nbuf_ref):
    u = u_ref[...]
    for k, w in enumerate(POOL_WINDOWS):
        ch = slice(k * POOL_GROUP_DIM, (k + 1) * POOL_GROUP_DIM)
        s = u[:, ch]
        for j in range(1, w):
            s = s + buf_ref[POOL_BUF - j, :, ch]
        diff = s / float(min(PAST_LEN + 1, w)) - u[:, ch]
        _pool_group_out(k, diff, wgrp_ref, scale_ref, z_ref)
    nbuf_ref[0:POOL_BUF - 1] = buf_ref[1:POOL_BUF]
    nbuf_ref[POOL_BUF - 1] = u


def _pool_step(proj, buf, w_grp, scale, tb):
    m = proj.shape[0]
    buf_spec = pl.BlockSpec((POOL_BUF, tb, POOL_WIDTH), lambda i: (0, i, 0))
    return pl.pallas_call(
        _pool_step_kernel,
        grid=(m // tb,),
        in_specs=[
            pl.BlockSpec((tb, POOL_WIDTH), lambda i: (i, 0)),
            buf_spec,
            pl.BlockSpec(w_grp.shape, lambda i: (0, 0, 0)),
            pl.BlockSpec((1, POOL_WIDTH), lambda i: (0, 0)),
        ],
        out_specs=[pl.BlockSpec((tb, POOL_WIDTH), lambda i: (i, 0)), buf_spec],
        out_shape=[jax.ShapeDtypeStruct((m, POOL_WIDTH), BF16), jax.ShapeDtypeStruct(buf.shape, F32)],
        compiler_params=_params("parallel"),
        name="pool_step",
    )(proj, buf, w_grp, scale)


def _zoh_lambda(a_re, a_im, log_step):
    dt = jnp.exp(log_step)
    mag = jnp.exp(a_re * dt)
    ang = a_im * dt
    return mag * jnp.cos(ang), mag * jnp.sin(ang)


def _tile_diag(blocks, rows_per_group):
    groups = SSM_GROUPS // SSM_CHUNKS
    rows, c = blocks.shape
    wide = groups * c
    spread = (lax.broadcasted_iota(jnp.int32, (c, wide), 1) % c
              == lax.broadcasted_iota(jnp.int32, (c, wide), 0)).astype(BF16)
    tiled = _dot(blocks.astype(BF16), spread)
    keep = (lax.broadcasted_iota(jnp.int32, (rows, wide), 0) // rows_per_group
            == lax.broadcasted_iota(jnp.int32, (rows, wide), 1) // c)
    return jnp.where(keep, tiled, 0.0).astype(BF16)


def _ssm_disc_kernel(a_re_ref, a_im_ref, ls_ref, a_re3_ref, a_im3_ref, ls3_ref, bt_ref, ct_ref,
                     bd_ref, cd_ref, lb_ref, tab_ref):
    a_re = a_re3_ref[...]
    a_im = a_im3_ref[...]
    lb_re, lb_im = _zoh_lambda(a_re, a_im, ls3_ref[...])
    den = a_re * a_re + a_im * a_im
    n_re = lb_re - 1.0
    f_re = (n_re * a_re + lb_im * a_im) / den
    f_im = (lb_im * a_re - n_re * a_im) / den
    b_re = bt_ref[0]
    b_im = bt_ref[1]
    bb = (f_re * b_re - f_im * b_im, f_re * b_im + f_im * b_re)
    groups = SSM_GROUPS // SSM_CHUNKS
    for part in range(2):
        for c in range(SSM_CHUNKS):
            grp = slice(c * groups, (c + 1) * groups)
            bd_ref[part, c] = _tile_diag(
                bb[part][grp].reshape(groups * SSM_GROUP_CH, SSM_STATE), SSM_GROUP_CH)
            cd_ref[part, c] = _tile_diag(
                ct_ref[part, grp].reshape(groups * SSM_STATE, SSM_GROUP_CH), SSM_STATE)

    lb_re, lb_im = _zoh_lambda(a_re_ref[...], a_im_ref[...], ls_ref[...])
    lb_ref[0] = lb_re
    lb_ref[1] = lb_im
    s_re, s_im = lb_re, lb_im
    for _ in range(int(math.log2(SSM_SEG))):
        s_re, s_im = _cmul(s_re, s_im, s_re, s_im)
    powers = [(s_re, s_im)]
    for _ in range(1, SUBLANES):
        powers.append(_cmul(*powers[-1], s_re, s_im))
    zero = jnp.zeros_like(lb_re)
    for r in range(SUBLANES):
        tab_ref[0, r] = lb_re
        tab_ref[1, r] = lb_im
        for i in range(3):
            d = 1 << i
            tab_ref[2 + 2 * i, r] = powers[d - 1][0] if r >= d else zero
            tab_ref[3 + 2 * i, r] = powers[d - 1][1] if r >= d else zero
        tab_ref[8, r] = powers[r][0]
        tab_ref[9, r] = powers[r][1]


def _ssm_disc(a_re, a_im, log_step, b_re, b_im, c_re, c_im):
    g, n, h = b_re.shape
    bt = jnp.stack([b_re, b_im]).swapaxes(2, 3)
    ct = jnp.stack([c_re, c_im]).swapaxes(2, 3)
    per_chunk = g // SSM_CHUNKS
    bd, cd, lb, tab = pl.pallas_call(
        _ssm_disc_kernel,
        out_shape=[jax.ShapeDtypeStruct((2, SSM_CHUNKS, per_chunk * h, per_chunk * n), BF16),
                   jax.ShapeDtypeStruct((2, SSM_CHUNKS, per_chunk * n, per_chunk * h), BF16),
                   jax.ShapeDtypeStruct((2, g, n), F32),
                   jax.ShapeDtypeStruct((10, SUBLANES, g, n), F32)],
        name="ssm_disc",
    )(a_re, a_im, log_step[:, None], a_re[:, None, :], a_im[:, None, :], log_step[:, None, None], bt, ct)
    return bd, cd, lb.reshape(2, g * n), tab.reshape(10, SUBLANES, g * n)


def _ssm_in(ub, bd_ref, x_ref):
    for c in range(SSM_CHUNKS):
        x_ref[:, c * SSM_CHUNK_LANES:(c + 1) * SSM_CHUNK_LANES] = _dot(
            ub[:, c * MXU_DIM:(c + 1) * MXU_DIM], bd_ref[c])


def _ssm_out(c, u, x_re_ref, x_im_ref, cdr_ref, cdi_ref, d_ref):
    ch = slice(c * MXU_DIM, (c + 1) * MXU_DIM)
    st = slice(c * SSM_CHUNK_LANES, (c + 1) * SSM_CHUNK_LANES)
    y = _dot(x_re_ref[:, st].astype(BF16), cdr_ref[c]) - _dot(x_im_ref[:, st].astype(BF16), cdi_ref[c])
    return jax.nn.gelu(y + d_ref[:, ch] * u[:, ch])


def _ssm_seq_kernel(u_ref, h_re_ref, h_im_ref, tab_ref, bdr_ref, bdi_ref, cdr_ref, cdi_ref, d_ref,
                    g_ref, last_re_ref, last_im_ref, xr_ref, xi_ref, cr_ref, ci_ref):
    t = pl.program_id(1)
    tt = u_ref.shape[0]
    seg = tt // SUBLANES

    @pl.when(t == 0)
    def _():
        cr_ref[...] = jnp.broadcast_to(h_re_ref[...], cr_ref.shape)
        ci_ref[...] = jnp.broadcast_to(h_im_ref[...], ci_ref.shape)

    def regroup(x, rows_major, rows_minor):
        w = x.shape[1]
        return jnp.swapaxes(x.reshape(rows_major, rows_minor, w), 0, 1).reshape(tt, w)

    u = regroup(u_ref[...], SUBLANES, seg)
    ub = u.astype(BF16)
    _ssm_in(ub, bdr_ref, xr_ref)
    _ssm_in(ub, bdi_ref, xi_ref)

    first_row = lax.broadcasted_iota(jnp.int32, (SUBLANES, SCAN_LANES), 0) == 0
    for lc in range(SSM_LANES // SCAN_LANES):
        sl = slice(lc * SCAN_LANES, (lc + 1) * SCAN_LANES)
        l_re = tab_ref[0, :, sl]
        l_im = tab_ref[1, :, sl]

        def rows(j):
            return pl.ds(j * SUBLANES, SUBLANES)

        def advance(j, x, sl=sl, l_re=l_re, l_im=l_im):
            x_re, x_im = _cmul(l_re, l_im, *x)
            x_re = x_re + xr_ref[rows(j), sl]
            x_im = x_im + xi_ref[rows(j), sl]
            xr_ref[rows(j), sl] = x_re
            xi_ref[rows(j), sl] = x_im
            return x_re, x_im

        zero = jnp.zeros((SUBLANES, SCAN_LANES), F32)
        f_re, f_im = zero, zero
        for j in range(seg):
            f_re, f_im = advance(j, (f_re, f_im))

        c_re = cr_ref[:, sl]
        c_im = ci_ref[:, sl]
        for i in range(3):
            s_re = pltpu.roll(f_re, 1 << i, axis=0)
            s_im = pltpu.roll(f_im, 1 << i, axis=0)
            m_re, m_im = _cmul(tab_ref[2 + 2 * i, :, sl], tab_ref[3 + 2 * i, :, sl], s_re, s_im)
            f_re, f_im = f_re + m_re, f_im + m_im
        m_re, m_im = _cmul(tab_ref[8, :, sl], tab_ref[9, :, sl], c_re, c_im)
        f_re, f_im = f_re + m_re, f_im + m_im
        e_re = jnp.where(first_row, c_re, pltpu.roll(f_re, 1, axis=0))
        e_im = jnp.where(first_row, c_im, pltpu.roll(f_im, 1, axis=0))
        cr_ref[:, sl] = jnp.broadcast_to(f_re[SUBLANES - 1:SUBLANES, :], (SUBLANES, SCAN_LANES))
        ci_ref[:, sl] = jnp.broadcast_to(f_im[SUBLANES - 1:SUBLANES, :], (SUBLANES, SCAN_LANES))

        def fix(j, w, sl=sl, l_re=l_re, l_im=l_im):
            w_re, w_im = _cmul(l_re, l_im, *w)
            xr_ref[rows(j), sl] += w_re
            xi_ref[rows(j), sl] += w_im
            return w_re, w_im

        w = (e_re, e_im)
        for j in range(seg):
            w = fix(j, w)

    for c in range(SSM_CHUNKS):
        ch = slice(c * MXU_DIM, (c + 1) * MXU_DIM)
        g = _ssm_out(c, u, xr_ref, xi_ref, cdr_ref, cdi_ref, d_ref)
        g_ref[:, ch] = regroup(g, seg, SUBLANES).astype(BF16)

    @pl.when(t == pl.num_programs(1) - 1)
    def _():
        last_re_ref[...] = cr_ref[0:1, :]
        last_im_ref[...] = ci_ref[0:1, :]


def _const_spec(shape):
    return pl.BlockSpec(shape, lambda *_: (0,) * len(shape))


def _part_spec(stacked, part):
    rest = stacked.shape[1:]
    return pl.BlockSpec((None,) + rest, lambda *_: (part,) + (0,) * len(rest))


def _ssm_seq(proj, h_re, h_im, tab, bd, cd, d_skip, batch, seq):
    tt = SSM_TIME_TILE
    nt = seq // tt
    col_block = POOL_WIDTH // SSM_WIDTH
    state_spec = pl.BlockSpec((None, 1, SSM_LANES), lambda b, t: (b, 0, 0))
    state_shape = jax.ShapeDtypeStruct((batch, 1, SSM_LANES), F32)
    return pl.pallas_call(
        _ssm_seq_kernel,
        grid=(batch, nt),
        in_specs=[
            pl.BlockSpec((tt, SSM_WIDTH), lambda b, t: (b * nt + t, col_block)),
            state_spec, state_spec,
            _const_spec(tab.shape),
            _part_spec(bd, 0), _part_spec(bd, 1), _part_spec(cd, 0), _part_spec(cd, 1),
            _const_spec(d_skip.shape),
        ],
        out_specs=[pl.BlockSpec((tt, SSM_WIDTH), lambda b, t: (b * nt + t, 0)), state_spec, state_spec],
        out_shape=[jax.ShapeDtypeStruct((batch * seq, SSM_WIDTH), BF16), state_shape, state_shape],
        scratch_shapes=[
            pltpu.VMEM((tt, SSM_LANES), F32), pltpu.VMEM((tt, SSM_LANES), F32),
            pltpu.VMEM((SUBLANES, SSM_LANES), F32), pltpu.VMEM((SUBLANES, SSM_LANES), F32),
        ],
        compiler_params=_params("parallel", "arbitrary"),
        name="ssm_seq",
    )(proj, h_re, h_im, tab, bd, bd, cd, cd, d_skip)


def _ssm_step_kernel(u_ref, h_re_ref, h_im_ref, lb_ref, bdr_ref, bdi_ref, cdr_ref, cdi_ref, d_ref,
                     g_ref, x_re_ref, x_im_ref):
    u = u_ref[...]
    ub = u.astype(BF16)
    _ssm_in(ub, bdr_ref, x_re_ref)
    _ssm_in(ub, bdi_ref, x_im_ref)
    m_re, m_im = _cmul(lb_ref[0:1, :], lb_ref[1:2, :], h_re_ref[...], h_im_ref[...])
    x_re_ref[...] += m_re
    x_im_ref[...] += m_im
    for c in range(SSM_CHUNKS):
        ch = slice(c * MXU_DIM, (c + 1) * MXU_DIM)
        g_ref[:, ch] = _ssm_out(c, u, x_re_ref, x_im_ref, cdr_ref, cdi_ref, d_ref).astype(BF16)


def _ssm_step(proj, h_re, h_im, lb, bd, cd, d_skip):
    m = proj.shape[0]
    state_spec = pl.BlockSpec((m, SSM_LANES), lambda i: (0, 0))
    state_shape = jax.ShapeDtypeStruct((m, SSM_LANES), F32)
    return pl.pallas_call(
        _ssm_step_kernel,
        grid=(1,),
        in_specs=[
            pl.BlockSpec((m, SSM_WIDTH), lambda i: (0, POOL_WIDTH // SSM_WIDTH)),
            state_spec, state_spec,
            _const_spec(lb.shape),
            _part_spec(bd, 0), _part_spec(bd, 1), _part_spec(cd, 0), _part_spec(cd, 1),
            _const_spec(d_skip.shape),
        ],
        out_specs=[pl.BlockSpec((m, SSM_WIDTH), lambda i: (0, 0)), state_spec, state_spec],
        out_shape=[jax.ShapeDtypeStruct((m, SSM_WIDTH), BF16), state_shape, state_shape],
        compiler_params=_params("arbitrary"),
        name="ssm_step",
    )(proj, h_re, h_im, lb, bd, bd, cd, cd, d_skip)


def _xa_seq_kernel(q_ref, k_ref, v_ref, o_ref):
    for h in range(XA_HEADS):
        ch = slice(h * XA_HEAD_DIM, (h + 1) * XA_HEAD_DIM)
        q = q_ref[:, ch].astype(BF16)
        k = k_ref[:, ch].astype(BF16)
        v = v_ref[:, ch].astype(BF16)
        s = lax.dot_general(q, k, (((1,), (1,)), ((), ())), preferred_element_type=F32) * XA_SCALE
        e = jnp.exp(s - jnp.max(s, axis=-1, keepdims=True))
        p = e / jnp.sum(e, axis=-1, keepdims=True)
        o_ref[:, ch] = _dot(p.astype(BF16), v).astype(BF16)


def _xa_seq(proj, mem_k, mem_v, batch, seq, tq):
    nt = seq // tq
    q_block = (POOL_WIDTH + SSM_WIDTH) // XA_WIDTH
    kv_spec = pl.BlockSpec((N_MEM, XA_WIDTH), lambda b, t: (b, 0))
    return pl.pallas_call(
        _xa_seq_kernel,
        grid=(batch, nt),
        in_specs=[pl.BlockSpec((tq, XA_WIDTH), lambda b, t: (b * nt + t, q_block)), kv_spec, kv_spec],
        out_specs=pl.BlockSpec((tq, XA_WIDTH), lambda b, t: (b * nt + t, 0)),
        out_shape=jax.ShapeDtypeStruct((batch * seq, XA_WIDTH), BF16),
        compiler_params=_params("parallel", "parallel"),
        name="xa_seq",
    )(proj, mem_k, mem_v)


XA_LANE_SPLIT = XA_HEAD_DIM // LANES


def _heads_on_sublanes(x):
    lead = x.shape[:-2]
    x = x.reshape(lead + (XA_HEADS, XA_LANE_SPLIT, LANES)).swapaxes(-3, -2)
    return x.reshape(lead + (XA_LANE_SPLIT * XA_HEADS, LANES))


def _heads_from_sublanes(x):
    lead = x.shape[:-2]
    x = x.reshape(lead + (XA_LANE_SPLIT, XA_HEADS, LANES)).swapaxes(-3, -2)
    return x.reshape(lead + (XA_HEADS, XA_HEAD_DIM))


def _xa_step_kernel(q_ref, k_ref, v_ref, o_ref):
    q = q_ref[...]
    part = jnp.sum(k_ref[...] * q, axis=-1, keepdims=True)
    s = (part + pltpu.roll(part, XA_HEADS, axis=2)) * XA_SCALE
    e = jnp.exp(s - jnp.max(s, axis=1, keepdims=True))
    r = 1.0 / jnp.sum(e, axis=1, keepdims=True)
    o_ref[...] = jnp.sum(e * v_ref[...], axis=1, keepdims=True) * r


def _xa_step(q4, cache_k, cache_v, tb):
    assert XA_LANE_SPLIT * XA_HEADS == SUBLANES
    m = q4.shape[0]
    kv_spec = pl.BlockSpec((tb, N_MEM, SUBLANES, LANES), lambda i: (i, 0, 0, 0))
    q_spec = pl.BlockSpec((tb, 1, SUBLANES, LANES), lambda i: (i, 0, 0, 0))
    o = pl.pallas_call(
        _xa_step_kernel,
        grid=(m // tb,),
        in_specs=[q_spec, kv_spec, kv_spec],
        out_specs=q_spec,
        out_shape=jax.ShapeDtypeStruct((m, 1, SUBLANES, LANES), F32),
        compiler_params=_params("parallel"),
        name="xa_step",
    )(_heads_on_sublanes(q4), _heads_on_sublanes(cache_k), _heads_on_sublanes(cache_v))
    return _heads_from_sublanes(o)


def _gate_kernel(z_ref, g_ref, o_ref, ga_ref, gb_ref, gc_ref, wpo_ref, wv_ref, wg_ref, wxo_ref, m_ref, *copy_refs):
    copies = copy_refs if copy_refs else (None,) * 4
    g = g_ref[...]
    o_pool = _dot(z_ref[...], _weight(wpo_ref, copies[0]))
    o_ssm = _dot(g, _weight(wv_ref, copies[1])) * jax.nn.sigmoid(_dot(g, _weight(wg_ref, copies[2])))
    o_xa = _dot(o_ref[...], _weight(wxo_ref, copies[3]))
    merged = (jax.nn.sigmoid(ga_ref[...]) * o_pool + jax.nn.sigmoid(gb_ref[...]) * o_ssm
              + jax.nn.sigmoid(gc_ref[...]) * o_xa)
    m_ref[...] = merged.astype(BF16)


def _gate(z, g, o, proj, weights, tm, tn):
    m, half = z.shape
    emit_w = weights[0].dtype != BF16
    assert not emit_w or m == tm
    d = weights[0].shape[1] if emit_w else weights[0].shape[0] * tn
    gate0 = OFF_GATE // tn
    per_gate = d // tn
    branch_spec = pl.BlockSpec((tm, half), lambda i, n: (i, 0))
    w_specs, copy_specs, copy_shapes = map(list, zip(*[_col_tiled(w, tn) for w in weights]))
    if not emit_w:
        copy_specs, copy_shapes = [], []

    def gate_spec(k):
        return pl.BlockSpec((tm, tn), lambda i, n: (i, gate0 + k * per_gate + n))

    outs = pl.pallas_call(
        _gate_kernel,
        grid=(m // tm, d // tn),
        in_specs=[branch_spec] * 3 + [gate_spec(0), gate_spec(1), gate_spec(2)] + w_specs,
        out_specs=[pl.BlockSpec((tm, tn), lambda i, n: (i, n))] + copy_specs,
        out_shape=[jax.ShapeDtypeStruct((m, d), BF16)] + copy_shapes,
        compiler_params=_params("parallel", "arbitrary"),
        name="gate",
    )(z, g, o, proj, proj, proj, *weights)
    return outs[0], (tuple(outs[1:]) if emit_w else tuple(weights))


def _mixout_kernel(m_ref, wout_ref, h_ref, gpost_ref, gnext_ref, h2_ref, xn_ref, *copy_ref, sub):
    wout = _weight(wout_ref, *copy_ref)
    for r in range(m_ref.shape[0] // sub):
        rows = slice(r * sub, (r + 1) * sub)
        h2 = h_ref[rows, :] + _rms(_dot(m_ref[rows, :], wout), gpost_ref[...])
        h2_ref[rows, :] = h2
        xn_ref[rows, :] = _rms(h2, gnext_ref[...]).astype(BF16)


def _mixout(merged, w_out, h, g_post, g_next, tm):
    m, d = h.shape
    emit_w = w_out.dtype != BF16
    assert not emit_w or m == tm
    row_spec = pl.BlockSpec((tm, d), lambda i: (i, 0))
    gain_spec = pl.BlockSpec((1, d), lambda i: (0, 0))
    w_spec = pl.BlockSpec((d, d), lambda i: (0, 0))
    copy_specs, copy_shapes = _with_copies([w_spec], [w_out], emit_w)
    outs = pl.pallas_call(
        functools.partial(_mixout_kernel, sub=min(tm, MXU_DIM)),
        grid=(m // tm,),
        in_specs=[row_spec, w_spec, row_spec, gain_spec, gain_spec],
        out_specs=[row_spec, row_spec] + copy_specs,
        out_shape=[jax.ShapeDtypeStruct((m, d), F32), jax.ShapeDtypeStruct((m, d), BF16)] + copy_shapes,
        compiler_params=_params("parallel"),
        name="mixout",
    )(merged, w_out, h, g_post, g_next)
    return outs[:2], (outs[2] if emit_w else w_out)


def _row_tile(m, target):
    return target if m % target == 0 else m


FFN1_W = ('w_ff1_gate', 'w_ff1_up', 'w_ff1_down')
FFN2_W = ('w_ff2_gate', 'w_ff2_up', 'w_ff2_down')
GATE_W = ('w_pool_out', 'w_glu_val', 'w_glu_gate', 'w_xa_out')
TILES = dict(tm=512, tm_ffn2=1024, tf=512, tm_proj=2048, tn_proj=1024, tm_gate=1024, tn=512)


def _layer(x, w, mixers):
    m = x.shape[0]
    tm = _row_tile(m, TILES['tm'])
    wb = dict(w)
    (h, xn), wb_ffn1 = _ffn(x, w['g_ff1_pre'], [w[k] for k in FFN1_W], w['g_ff1_post'], tm, TILES['tf'],
                            g_next=w['g_mix_pre'])
    proj, wb['w_in'] = _proj(xn, w['w_in'], _row_tile(m, TILES['tm_proj']), TILES['tn_proj'])
    z, g, o, states = mixers(proj)
    merged, wb_gate = _gate(z, g, o, proj, [w[k] for k in GATE_W], _row_tile(m, TILES['tm_gate']), TILES['tn'])
    (h, xn), wb['w_out'] = _mixout(merged, w['w_out'], h, w['g_mix_post'], w['g_ff2_pre'], tm)
    (y,), wb_ffn2 = _ffn(h, xn, [w[k] for k in FFN2_W], w['g_ff2_post'], _row_tile(m, TILES['tm_ffn2']),
                         TILES['tf'])
    wb.update(zip(FFN1_W + GATE_W + FFN2_W, wb_ffn1 + wb_gate + wb_ffn2))
    return y, states, wb


def kernel(x_prompt, x_sample, mem_prompt, cache_mem_k, cache_mem_v, state_pool, state_ssm_re, state_ssm_im, g_ff1_pre, w_ff1_gate, w_ff1_up, w_ff1_down, g_ff1_post, g_mix_pre, w_in, w_pool_grp, pool_scale, w_pool_out, ssm_a_re, ssm_a_im, ssm_log_step, ssm_b_re, ssm_b_im, ssm_c_re, ssm_c_im, ssm_d, w_glu_val, w_glu_gate, g_mem, w_mem_k, w_mem_v, w_xa_out, w_out, g_mix_post, g_ff2_pre, w_ff2_gate, w_ff2_up, w_ff2_down, g_ff2_post):
    bp, lp, d = x_prompt.shape
    bs, ls, _ = x_sample.shape
    depth = w_in.shape[0]
    assert depth == 1 and ls == 1 and lp >= POOL_BUF and lp % SSM_TIME_TILE == 0

    mats = dict(w_ff1_gate=w_ff1_gate, w_ff1_up=w_ff1_up, w_ff1_down=w_ff1_down, w_in=w_in,
                w_pool_out=w_pool_out, w_glu_val=w_glu_val, w_glu_gate=w_glu_gate, w_xa_out=w_xa_out,
                w_out=w_out, w_ff2_gate=w_ff2_gate, w_ff2_up=w_ff2_up, w_ff2_down=w_ff2_down)
    w = {k: v[0] for k, v in mats.items()}
    w.update(g_ff1_pre=g_ff1_pre, g_ff1_post=g_ff1_post, g_mix_pre=g_mix_pre, g_mix_post=g_mix_post,
             g_ff2_pre=g_ff2_pre, g_ff2_post=g_ff2_post)
    w_grp = w_pool_grp[0].astype(BF16)
    scale = pool_scale
    d_skip = ssm_d

    bd, cd, lb, tab = _ssm_disc(ssm_a_re[0], ssm_a_im[0], ssm_log_step[0], ssm_b_re[0], ssm_b_im[0],
                                ssm_c_re[0], ssm_c_im[0])

    def sample_mixers(proj):
        z, new_buf = _pool_step(proj, state_pool[0].swapaxes(0, 1), w_grp, scale, 32)
        g, x_re, x_im = _ssm_step(proj, state_ssm_re[0].reshape(bs, SSM_LANES),
                                  state_ssm_im[0].reshape(bs, SSM_LANES), lb, bd, cd, d_skip)
        q4 = proj[:, POOL_WIDTH + SSM_WIDTH:OFF_GATE].reshape(bs, 1, XA_HEADS, XA_HEAD_DIM)
        o = _xa_step(q4, cache_mem_k[0], cache_mem_v[0], 4)
        return z, g, o.reshape(bs, XA_WIDTH).astype(BF16), (new_buf, x_re, x_im)

    y_sample, (pool_s, re_s, im_s), wb = _layer(x_sample.reshape(bs * ls, d), w, sample_mixers)

    mem = mem_prompt.reshape(bp * N_MEM, d)
    mem_k, mem_k_heads = _memproj(mem, g_mem, w_mem_k[0])
    mem_v, mem_v_heads = _memproj(mem, g_mem, w_mem_v[0])

    def prompt_mixers(proj):
        z = _pool_seq(proj, w_grp, scale, bp, lp, 512)
        zeros = jnp.zeros((bp, 1, SSM_LANES), F32)
        g, last_re, last_im = _ssm_seq(proj, zeros, zeros, tab, bd, cd, d_skip, bp, lp)
        o = _xa_seq(proj, mem_k, mem_v, bp, lp, 512)
        pool_rows = proj.reshape(bp, lp, IN_WIDTH)[:, lp - POOL_BUF:, :POOL_WIDTH]
        return z, g, o, (pool_rows, last_re, last_im)

    y_prompt, (pool_p, re_p, im_p), _ = _layer(x_prompt.reshape(bp * lp, d), wb, prompt_mixers)

    kv_shape = (1, bp, N_MEM, XA_HEADS, XA_HEAD_DIM)
    st_shape = (SSM_GROUPS, SSM_STATE)
    return (y_prompt.reshape(bp, lp, d), y_sample.reshape(bs, ls, d),
            mem_k_heads.reshape(kv_shape), mem_v_heads.reshape(kv_shape),
            pool_p[None],
            re_p.reshape((1, bp) + st_shape), im_p.reshape((1, bp) + st_shape),
            pool_s.swapaxes(0, 1)[None],
            re_s.reshape((1, bs) + st_shape), im_s.reshape((1, bs) + st_shape))
```

```python
import functools
import math

import jax
import jax.numpy as jnp
from jax import lax
from jax.experimental import pallas as pl
from jax.experimental.pallas import tpu as pltpu

F32 = jnp.float32
BF16 = jnp.bfloat16

D_MODEL = 2048
PAST_LEN = 16384
POOL_WIDTH = D_MODEL // 2
POOL_WINDOWS = (2, 4, 8, 16)
POOL_GROUP_DIM = POOL_WIDTH // len(POOL_WINDOWS)
POOL_BUF = max(POOL_WINDOWS) - 1
SSM_WIDTH = D_MODEL // 2
SSM_GROUP_CH = 16
SSM_GROUPS = SSM_WIDTH // SSM_GROUP_CH
SSM_STATE = 64
SSM_LANES = SSM_GROUPS * SSM_STATE
XA_HEADS = 4
XA_HEAD_DIM = D_MODEL // 8
XA_WIDTH = XA_HEADS * XA_HEAD_DIM
XA_SCALE = XA_HEAD_DIM ** -0.5
N_MEM = 256
OFF_GATE = POOL_WIDTH + SSM_WIDTH + XA_WIDTH
IN_WIDTH = OFF_GATE + 3 * D_MODEL
RMS_EPS = 1e-6

SUBLANES = 8
LANES = 128
MXU_DIM = 256
VMEM_LIMIT_BYTES = 60 * 1024 * 1024
DOWN_CHUNKS = 4
NORM_ROWS = 2 * SUBLANES

SSM_CHUNKS = SSM_WIDTH // MXU_DIM
SSM_CHUNK_LANES = SSM_LANES // SSM_CHUNKS
SSM_TIME_TILE = 256
SSM_SEG = SSM_TIME_TILE // SUBLANES
SCAN_LANES = 4 * LANES


def _params(*sem):
    return pltpu.CompilerParams(dimension_semantics=sem, vmem_limit_bytes=VMEM_LIMIT_BYTES)


def _rms(x, g, half=False):
    scale = lax.rsqrt(jnp.mean(x * x, axis=-1, keepdims=True) + RMS_EPS)
    return (x * (0.5 * scale if half else scale)) * g


def _dot(a, b):
    return jnp.dot(a, b, preferred_element_type=F32)


def _cmul(a_re, a_im, b_re, b_im):
    return a_re * b_re - a_im * b_im, a_re * b_im + a_im * b_re


def _weight(w_ref, copy_ref=None):
    w = w_ref[...]
    if w.dtype != BF16:
        w = w.astype(BF16)
    if copy_ref is not None:
        copy_ref[...] = w
    return w


def _with_copies(specs, weights, emit_w):
    if not emit_w:
        return [], []
    return list(specs), [jax.ShapeDtypeStruct(w.shape, BF16) for w in weights]


def _col_tiled(w, tn):
    tiled = lambda k: pl.BlockSpec((None, k, tn), lambda i, j: (j, 0, 0))
    if w.dtype == BF16:
        assert w.ndim == 3 and w.shape[2] == tn, "bf16 weights come tile-major from an earlier call"
        return tiled(w.shape[1]), None, None
    k, n = w.shape
    return pl.BlockSpec((k, tn), lambda i, j: (0, j)), tiled(k), jax.ShapeDtypeStruct((n // tn, k, tn), BF16)


def _ffn_kernel(*refs, first, emit_w):
    refs = list(refs)
    x_ref, pre_ref, wg_ref, wu_ref, wd_ref, gpost_ref = refs[:6]
    del refs[:6]
    gnext_ref = refs.pop(0) if first else None
    o_ref = refs.pop(0)
    xnext_ref = refs.pop(0) if first else None
    wgb_ref, wub_ref, wdb_ref = (refs.pop(0), refs.pop(0), refs.pop(0)) if emit_w else (None, None, None)
    if first:
        xn_ref, = refs
    else:
        xn_ref = pre_ref
        xbuf_ref, xsem = refs
    i = pl.program_id(0)
    j = pl.program_id(1)
    tm = o_ref.shape[0]

    def x_copy():
        return pltpu.make_async_copy(x_ref.at[pl.ds(i * tm, tm), :], xbuf_ref, xsem)

    @pl.when(j == 0)
    def _():
        if first:
            xn_ref[...] = _rms(x_ref[...], pre_ref[...]).astype(BF16)
        else:
            x_copy().start()
        o_ref[...] = jnp.zeros_like(o_ref)

    xn = xn_ref[...]
    g = _dot(xn, _weight(wg_ref, wgb_ref))
    u = _dot(xn, _weight(wu_ref, wub_ref))
    hid = ((g * jax.nn.sigmoid(g)) * u).astype(BF16)
    wd = _weight(wd_ref, wdb_ref)
    cols = wd.shape[1] // DOWN_CHUNKS
    for n in range(DOWN_CHUNKS):
        sl = slice(n * cols, (n + 1) * cols)
        o_ref[:, sl] += _dot(hid, wd[:, sl])

    @pl.when(j == pl.num_programs(1) - 1)
    def _():
        if first:
            out = x_ref[...] + _rms(o_ref[...], gpost_ref[...], half=True)
            o_ref[...] = out
            xnext_ref[...] = _rms(out, gnext_ref[...]).astype(BF16)
        else:
            x_copy().wait()
            for r in range(tm // NORM_ROWS):
                rs = slice(r * NORM_ROWS, (r + 1) * NORM_ROWS)
                o_ref[rs, :] = xbuf_ref[rs, :] + _rms(o_ref[rs, :], gpost_ref[...], half=True)


def _ffn(x, pre, weights, g_post, tm, tf, g_next=None):
    m, d = x.shape
    w_gate, w_up, w_down = weights
    f = w_down.shape[0]
    first = g_next is not None
    emit_w = w_gate.dtype != BF16
    assert not emit_w or m == tm, "weight copies need every weight block visited exactly once"
    row_spec = pl.BlockSpec((tm, d), lambda i, j: (i, 0))
    gain_spec = pl.BlockSpec((1, d), lambda i, j: (0, 0))
    wg_spec, wg_copy, wg_shape = _col_tiled(w_gate, tf)
    wu_spec, wu_copy, wu_shape = _col_tiled(w_up, tf)
    wd_spec = pl.BlockSpec((tf, d), lambda i, j: (j, 0))
    w_specs = [wg_spec, wu_spec, wd_spec]
    copy_specs = [wg_copy, wu_copy, wd_spec] if emit_w else []
    copy_shapes = [wg_shape, wu_shape, jax.ShapeDtypeStruct(w_down.shape, BF16)] if emit_w else []
    outs = pl.pallas_call(
        functools.partial(_ffn_kernel, first=first, emit_w=emit_w),
        grid=(m // tm, f // tf),
        in_specs=([row_spec, gain_spec] if first else [pl.BlockSpec(memory_space=pl.ANY), row_spec])
        + w_specs + [gain_spec] + ([gain_spec] if first else []),
        out_specs=[row_spec] + ([row_spec] if first else []) + copy_specs,
        out_shape=[jax.ShapeDtypeStruct((m, d), F32)] + ([jax.ShapeDtypeStruct((m, d), BF16)] if first else [])
        + copy_shapes,
        scratch_shapes=[pltpu.VMEM((tm, d), BF16)] if first
        else [pltpu.VMEM((tm, d), F32), pltpu.SemaphoreType.DMA(())],
        compiler_params=_params("parallel", "arbitrary"),
        name="ffn",
    )(*((x, pre, w_gate, w_up, w_down, g_post) + ((g_next,) if first else ())))
    n_out = 2 if first else 1
    return outs[:n_out], (tuple(outs[n_out:]) if emit_w else tuple(weights))


def _proj_kernel(xn_ref, w_ref, o_ref, *copy_ref):
    o_ref[...] = _dot(xn_ref[...], _weight(w_ref, *copy_ref))


def _proj(xn, w, tm, tn):
    m, d = xn.shape
    emit_w = w.dtype != BF16
    assert not emit_w or m == tm
    n = w.shape[1] if emit_w else w.shape[0] * tn
    w_spec, copy_spec, copy_shape = _col_tiled(w, tn)
    copy_specs, copy_shapes = ([copy_spec], [copy_shape]) if emit_w else ([], [])
    outs = pl.pallas_call(
        _proj_kernel,
        grid=(m // tm, n // tn),
        in_specs=[pl.BlockSpec((tm, d), lambda i, j: (i, 0)), w_spec],
        out_specs=[pl.BlockSpec((tm, tn), lambda i, j: (i, j))] + copy_specs,
        out_shape=[jax.ShapeDtypeStruct((m, n), F32)] + copy_shapes,
        compiler_params=_params("parallel", "arbitrary"),
        name="proj",
    )(xn, w)
    return outs[0], (outs[1] if emit_w else w)


def _memproj_kernel(x_ref, g_ref, wk_ref, wv_ref, k_ref, kh_ref, v_ref, vh_ref):
    xn = _rms(x_ref[...], g_ref[...]).astype(BF16)
    for w_ref, o_ref, oh_ref in ((wk_ref, k_ref, kh_ref), (wv_ref, v_ref, vh_ref)):
        res = _dot(xn, _weight(w_ref))
        o_ref[...] = res
        for h in range(XA_HEADS):
            for p in range(XA_LANE_SPLIT):
                lane0 = h * XA_HEAD_DIM + p * LANES
                oh_ref[:, p * XA_HEADS + h, :] = res[:, lane0:lane0 + LANES]


def _memproj(x, g, w_k, w_v):
    m, d = x.shape
    n = w_k.shape[1]
    assert n == XA_WIDTH
    tm = _row_tile(m, TILES['tm_mem'])
    w_spec = pl.BlockSpec((d, n), lambda i: (0, 0), pipeline_mode=pl.Buffered(1))
    flat_spec = pl.BlockSpec((tm, n), lambda i: (i, 0))
    head_spec = pl.BlockSpec((tm, SUBLANES, LANES), lambda i: (i, 0, 0))
    flat_shape = jax.ShapeDtypeStruct((m, n), F32)
    head_shape = jax.ShapeDtypeStruct((m, SUBLANES, LANES), F32)
    k, kh, v, vh = pl.pallas_call(
        _memproj_kernel,
        grid=(m // tm,),
        in_specs=[pl.BlockSpec((tm, d), lambda i: (i, 0)), pl.BlockSpec((1, d), lambda i: (0, 0)), w_spec, w_spec],
        out_specs=[flat_spec, head_spec, flat_spec, head_spec],
        out_shape=[flat_shape, head_shape, flat_shape, head_shape],
        compiler_params=_params("parallel"),
        name="memproj",
    )(x, g, w_k, w_v)
    return k, _heads_from_sublanes(kh), v, _heads_from_sublanes(vh)


def _pool_group_out(k, diff, wgrp_ref, scale_ref, z_ref):
    ch = slice(k * POOL_GROUP_DIM, (k + 1) * POOL_GROUP_DIM)
    zk = _dot(diff.astype(BF16), wgrp_ref[k]) * scale_ref[:, ch]
    z_ref[:, ch] = zk.astype(BF16)


def _pool_seq_kernel(u_ref, wgrp_ref, scale_ref, z_ref, ext_ref, *, halo):
    t = pl.program_id(1)
    tt = u_ref.shape[0]

    @pl.when(t == 0)
    def _():
        ext_ref[0:halo, :] = jnp.zeros((halo, POOL_WIDTH), F32)

    u = u_ref[...]
    ext_ref[halo:halo + tt, :] = u
    pos = t * tt + lax.broadcasted_iota(jnp.int32, (tt, 1), 0)
    for k, w in enumerate(POOL_WINDOWS):
        ch = slice(k * POOL_GROUP_DIM, (k + 1) * POOL_GROUP_DIM)
        a = ext_ref[:, ch]
        d = 1
        while d < w:
            a = a + pltpu.roll(a, d, axis=0)
            d *= 2
        count = jnp.minimum(pos + 1, w).astype(F32)
        diff = a[halo:, :] / count - u[:, ch]
        _pool_group_out(k, diff, wgrp_ref, scale_ref, z_ref)
    ext_ref[0:halo, :] = ext_ref[tt:tt + halo, :]


def _pool_seq(proj, w_grp, scale, batch, seq, tt):
    halo = 2 * SUBLANES
    nt = seq // tt
    return pl.pallas_call(
        functools.partial(_pool_seq_kernel, halo=halo),
        grid=(batch, nt),
        in_specs=[
            pl.BlockSpec((tt, POOL_WIDTH), lambda b, t: (b * nt + t, 0)),
            pl.BlockSpec(w_grp.shape, lambda b, t: (0, 0, 0)),
            pl.BlockSpec((1, POOL_WIDTH), lambda b, t: (0, 0)),
        ],
        out_specs=pl.BlockSpec((tt, POOL_WIDTH), lambda b, t: (b * nt + t, 0)),
        out_shape=jax.ShapeDtypeStruct((batch * seq, POOL_WIDTH), BF16),
        scratch_shapes=[pltpu.VMEM((tt + halo, POOL_WIDTH), F32)],
        compiler_params=_params("parallel", "arbitrary"),
        name="pool_seq",
    )(proj, w_grp, scale)


def _pool_step_kernel(u_ref, buf_ref, wgrp_ref, scale_ref, z_ref, nbuf_ref):
    u = u_ref[...]
    for k, w in enumerate(POOL_WINDOWS):
        ch = slice(k * POOL_GROUP_DIM, (k + 1) * POOL_GROUP_DIM)
        s = u[:, ch]
        for j in range(1, w):
            s = s + buf_ref[POOL_BUF - j, :, ch]
        diff = s / float(min(PAST_LEN + 1, w)) - u[:, ch]
        _pool_group_out(k, diff, wgrp_ref, scale_ref, z_ref)
    nbuf_ref[0:POOL_BUF - 1] = buf_ref[1:POOL_BUF]
    nbuf_ref[POOL_BUF - 1] = u


def _pool_step(proj, buf, w_grp, scale, tb):
    m = proj.shape[0]
    buf_spec = pl.BlockSpec((POOL_BUF, tb, POOL_WIDTH), lambda i: (0, i, 0))
    return pl.pallas_call(
        _pool_step_kernel,
        grid=(m // tb,),
        in_specs=[
            pl.BlockSpec((tb, POOL_WIDTH), lambda i: (i, 0)),
            buf_spec,
            pl.BlockSpec(w_grp.shape, lambda i: (0, 0, 0)),
            pl.BlockSpec((1, POOL_WIDTH), lambda i: (0, 0)),
        ],
        out_specs=[pl.BlockSpec((tb, POOL_WIDTH), lambda i: (i, 0)), buf_spec],
        out_shape=[jax.ShapeDtypeStruct((m, POOL_WIDTH), BF16), jax.ShapeDtypeStruct(buf.shape, F32)],
        compiler_params=_params("parallel"),
        name="pool_step",
    )(proj, buf, w_grp, scale)


def _zoh_lambda(a_re, a_im, log_step):
    dt = jnp.exp(log_step)
    mag = jnp.exp(a_re * dt)
    ang = a_im * dt
    return mag * jnp.cos(ang), mag * jnp.sin(ang)


def _tile_diag(blocks, rows_per_group):
    groups = SSM_GROUPS // SSM_CHUNKS
    rows, c = blocks.shape
    wide = groups * c
    spread = (lax.broadcasted_iota(jnp.int32, (c, wide), 1) % c
              == lax.broadcasted_iota(jnp.int32, (c, wide), 0)).astype(BF16)
    tiled = _dot(blocks.astype(BF16), spread)
    keep = (lax.broadcasted_iota(jnp.int32, (rows, wide), 0) // rows_per_group
            == lax.broadcasted_iota(jnp.int32, (rows, wide), 1) // c)
    return jnp.where(keep, tiled, 0.0).astype(BF16)


def _ssm_disc_kernel(a_re_ref, a_im_ref, ls_ref, a_re3_ref, a_im3_ref, ls3_ref, bt_ref, ct_ref,
                     bd_ref, cd_ref, lb_ref, tab_ref):
    a_re = a_re3_ref[...]
    a_im = a_im3_ref[...]
    lb_re, lb_im = _zoh_lambda(a_re, a_im, ls3_ref[...])
    den = a_re * a_re + a_im * a_im
    n_re = lb_re - 1.0
    f_re = (n_re * a_re + lb_im * a_im) / den
    f_im = (lb_im * a_re - n_re * a_im) / den
    b_re = bt_ref[0]
    b_im = bt_ref[1]
    bb = (f_re * b_re - f_im * b_im, f_re * b_im + f_im * b_re)
    groups = SSM_GROUPS // SSM_CHUNKS
    for part in range(2):
        for c in range(SSM_CHUNKS):
            grp = slice(c * groups, (c + 1) * groups)
            bd_ref[part, c] = _tile_diag(
                bb[part][grp].reshape(groups * SSM_GROUP_CH, SSM_STATE), SSM_GROUP_CH)
            cd_ref[part, c] = _tile_diag(
                ct_ref[part, grp].reshape(groups * SSM_STATE, SSM_GROUP_CH), SSM_STATE)

    lb_re, lb_im = _zoh_lambda(a_re_ref[...], a_im_ref[...], ls_ref[...])
    lb_ref[0] = lb_re
    lb_ref[1] = lb_im
    s_re, s_im = lb_re, lb_im
    for _ in range(int(math.log2(SSM_SEG))):
        s_re, s_im = _cmul(s_re, s_im, s_re, s_im)
    powers = [(s_re, s_im)]
    for _ in range(1, SUBLANES):
        powers.append(_cmul(*powers[-1], s_re, s_im))
    zero = jnp.zeros_like(lb_re)
    for r in range(SUBLANES):
        tab_ref[0, r] = lb_re
        tab_ref[1, r] = lb_im
        for i in range(3):
            d = 1 << i
            tab_ref[2 + 2 * i, r] = powers[d - 1][0] if r >= d else zero
            tab_ref[3 + 2 * i, r] = powers[d - 1][1] if r >= d else zero
        tab_ref[8, r] = powers[r][0]
        tab_ref[9, r] = powers[r][1]


def _ssm_disc(a_re, a_im, log_step, b_re, b_im, c_re, c_im):
    g, n, h = b_re.shape
    bt = jnp.stack([b_re, b_im]).swapaxes(2, 3)
    ct = jnp.stack([c_re, c_im]).swapaxes(2, 3)
    per_chunk = g // SSM_CHUNKS
    bd, cd, lb, tab = pl.pallas_call(
        _ssm_disc_kernel,
        out_shape=[jax.ShapeDtypeStruct((2, SSM_CHUNKS, per_chunk * h, per_chunk * n), BF16),
                   jax.ShapeDtypeStruct((2, SSM_CHUNKS, per_chunk * n, per_chunk * h), BF16),
                   jax.ShapeDtypeStruct((2, g, n), F32),
                   jax.ShapeDtypeStruct((10, SUBLANES, g, n), F32)],
        name="ssm_disc",
    )(a_re, a_im, log_step[:, None], a_re[:, None, :], a_im[:, None, :], log_step[:, None, None], bt, ct)
    return bd, cd, lb.reshape(2, g * n), tab.reshape(10, SUBLANES, g * n)


def _ssm_in(ub, bd_ref, x_ref):
    for c in range(SSM_CHUNKS):
        x_ref[:, c * SSM_CHUNK_LANES:(c + 1) * SSM_CHUNK_LANES] = _dot(
            ub[:, c * MXU_DIM:(c + 1) * MXU_DIM], bd_ref[c])


def _ssm_out(c, u, x_re_ref, x_im_ref, cdr_ref, cdi_ref, d_ref):
    ch = slice(c * MXU_DIM, (c + 1) * MXU_DIM)
    st = slice(c * SSM_CHUNK_LANES, (c + 1) * SSM_CHUNK_LANES)
    y = _dot(x_re_ref[:, st].astype(BF16), cdr_ref[c]) - _dot(x_im_ref[:, st].astype(BF16), cdi_ref[c])
    return jax.nn.gelu(y + d_ref[:, ch] * u[:, ch])


def _ssm_seq_kernel(u_ref, h_re_ref, h_im_ref, tab_ref, bdr_ref, bdi_ref, cdr_ref, cdi_ref, d_ref,
                    g_ref, last_re_ref, last_im_ref, xr_ref, xi_ref, cr_ref, ci_ref):
    t = pl.program_id(1)
    tt = u_ref.shape[0]
    seg = tt // SUBLANES

    @pl.when(t == 0)
    def _():
        cr_ref[...] = jnp.broadcast_to(h_re_ref[...], cr_ref.shape)
        ci_ref[...] = jnp.broadcast_to(h_im_ref[...], ci_ref.shape)

    def regroup(x, rows_major, rows_minor):
        w = x.shape[1]
        return jnp.swapaxes(x.reshape(rows_major, rows_minor, w), 0, 1).reshape(tt, w)

    u = regroup(u_ref[...], SUBLANES, seg)
    ub = u.astype(BF16)
    _ssm_in(ub, bdr_ref, xr_ref)
    _ssm_in(ub, bdi_ref, xi_ref)

    first_row = lax.broadcasted_iota(jnp.int32, (SUBLANES, SCAN_LANES), 0) == 0
    for lc in range(SSM_LANES // SCAN_LANES):
        sl = slice(lc * SCAN_LANES, (lc + 1) * SCAN_LANES)
        l_re = tab_ref[0, :, sl]
        l_im = tab_ref[1, :, sl]

        def rows(j):
            return pl.ds(j * SUBLANES, SUBLANES)

        def advance(j, x, sl=sl, l_re=l_re, l_im=l_im):
            x_re, x_im = _cmul(l_re, l_im, *x)
            x_re = x_re + xr_ref[rows(j), sl]
            x_im = x_im + xi_ref[rows(j), sl]
            xr_ref[rows(j), sl] = x_re
            xi_ref[rows(j), sl] = x_im
            return x_re, x_im

        zero = jnp.zeros((SUBLANES, SCAN_LANES), F32)
        f_re, f_im = zero, zero
        for j in range(seg):
            f_re, f_im = advance(j, (f_re, f_im))

        c_re = cr_ref[:, sl]
        c_im = ci_ref[:, sl]
        for i in range(3):
            s_re = pltpu.roll(f_re, 1 << i, axis=0)
            s_im = pltpu.roll(f_im, 1 << i, axis=0)
            m_re, m_im = _cmul(tab_ref[2 + 2 * i, :, sl], tab_ref[3 + 2 * i, :, sl], s_re, s_im)
            f_re, f_im = f_re + m_re, f_im + m_im
        m_re, m_im = _cmul(tab_ref[8, :, sl], tab_ref[9, :, sl], c_re, c_im)
        f_re, f_im = f_re + m_re, f_im + m_im
        e_re = jnp.where(first_row, c_re, pltpu.roll(f_re, 1, axis=0))
        e_im = jnp.where(first_row, c_im, pltpu.roll(f_im, 1, axis=0))
        cr_ref[:, sl] = jnp.broadcast_to(f_re[SUBLANES - 1:SUBLANES, :], (SUBLANES, SCAN_LANES))
        ci_ref[:, sl] = jnp.broadcast_to(f_im[SUBLANES - 1:SUBLANES, :], (SUBLANES, SCAN_LANES))

        def fix(j, w, sl=sl, l_re=l_re, l_im=l_im):
            w_re, w_im = _cmul(l_re, l_im, *w)
            xr_ref[rows(j), sl] += w_re
            xi_ref[rows(j), sl] += w_im
            return w_re, w_im

        w = (e_re, e_im)
        for j in range(seg):
            w = fix(j, w)

    for c in range(SSM_CHUNKS):
        ch = slice(c * MXU_DIM, (c + 1) * MXU_DIM)
        g = _ssm_out(c, u, xr_ref, xi_ref, cdr_ref, cdi_ref, d_ref)
        g_ref[:, ch] = regroup(g, seg, SUBLANES).astype(BF16)

    @pl.when(t == pl.num_programs(1) - 1)
    def _():
        last_re_ref[...] = cr_ref[0:1, :]
        last_im_ref[...] = ci_ref[0:1, :]


def _const_spec(shape):
    return pl.BlockSpec(shape, lambda *_: (0,) * len(shape))


def _part_spec(stacked, part):
    rest = stacked.shape[1:]
    return pl.BlockSpec((None,) + rest, lambda *_: (part,) + (0,) * len(rest))


def _ssm_seq(proj, h_re, h_im, tab, bd, cd, d_skip, batch, seq):
    tt = SSM_TIME_TILE
    nt = seq // tt
    col_block = POOL_WIDTH // SSM_WIDTH
    state_spec = pl.BlockSpec((None, 1, SSM_LANES), lambda b, t: (b, 0, 0))
    state_shape = jax.ShapeDtypeStruct((batch, 1, SSM_LANES), F32)
    return pl.pallas_call(
        _ssm_seq_kernel,
        grid=(batch, nt),
        in_specs=[
            pl.BlockSpec((tt, SSM_WIDTH), lambda b, t: (b * nt + t, col_block)),
            state_spec, state_spec,
            _const_spec(tab.shape),
            _part_spec(bd, 0), _part_spec(bd, 1), _part_spec(cd, 0), _part_spec(cd, 1),
            _const_spec(d_skip.shape),
        ],
        out_specs=[pl.BlockSpec((tt, SSM_WIDTH), lambda b, t: (b * nt + t, 0)), state_spec, state_spec],
        out_shape=[jax.ShapeDtypeStruct((batch * seq, SSM_WIDTH), BF16), state_shape, state_shape],
        scratch_shapes=[
            pltpu.VMEM((tt, SSM_LANES), F32), pltpu.VMEM((tt, SSM_LANES), F32),
            pltpu.VMEM((SUBLANES, SSM_LANES), F32), pltpu.VMEM((SUBLANES, SSM_LANES), F32),
        ],
        compiler_params=_params("parallel", "arbitrary"),
        name="ssm_seq",
    )(proj, h_re, h_im, tab, bd, bd, cd, cd, d_skip)


def _ssm_step_kernel(u_ref, h_re_ref, h_im_ref, lb_ref, bdr_ref, bdi_ref, cdr_ref, cdi_ref, d_ref,
                     g_ref, x_re_ref, x_im_ref):
    u = u_ref[...]
    ub = u.astype(BF16)
    _ssm_in(ub, bdr_ref, x_re_ref)
    _ssm_in(ub, bdi_ref, x_im_ref)
    m_re, m_im = _cmul(lb_ref[0:1, :], lb_ref[1:2, :], h_re_ref[...], h_im_ref[...])
    x_re_ref[...] += m_re
    x_im_ref[...] += m_im
    for c in range(SSM_CHUNKS):
        ch = slice(c * MXU_DIM, (c + 1) * MXU_DIM)
        g_ref[:, ch] = _ssm_out(c, u, x_re_ref, x_im_ref, cdr_ref, cdi_ref, d_ref).astype(BF16)


def _ssm_step(proj, h_re, h_im, lb, bd, cd, d_skip):
    m = proj.shape[0]
    state_spec = pl.BlockSpec((m, SSM_LANES), lambda i: (0, 0))
    state_shape = jax.ShapeDtypeStruct((m, SSM_LANES), F32)
    return pl.pallas_call(
        _ssm_step_kernel,
        grid=(1,),
        in_specs=[
            pl.BlockSpec((m, SSM_WIDTH), lambda i: (0, POOL_WIDTH // SSM_WIDTH)),
            state_spec, state_spec,
            _const_spec(lb.shape),
            _part_spec(bd, 0), _part_spec(bd, 1), _part_spec(cd, 0), _part_spec(cd, 1),
            _const_spec(d_skip.shape),
        ],
        out_specs=[pl.BlockSpec((m, SSM_WIDTH), lambda i: (0, 0)), state_spec, state_spec],
        out_shape=[jax.ShapeDtypeStruct((m, SSM_WIDTH), BF16), state_shape, state_shape],
        compiler_params=_params("arbitrary"),
        name="ssm_step",
    )(proj, h_re, h_im, lb, bd, bd, cd, cd, d_skip)


def _xa_seq_kernel(q_ref, k_ref, v_ref, o_ref):
    for h in range(XA_HEADS):
        ch = slice(h * XA_HEAD_DIM, (h + 1) * XA_HEAD_DIM)
        q = q_ref[:, ch].astype(BF16)
        k = k_ref[:, ch].astype(BF16)
        v = v_ref[:, ch].astype(BF16)
        s = lax.dot_general(q, k, (((1,), (1,)), ((), ())), preferred_element_type=F32) * XA_SCALE
        e = jnp.exp(s - jnp.max(s, axis=-1, keepdims=True))
        p = e / jnp.sum(e, axis=-1, keepdims=True)
        o_ref[:, ch] = _dot(p.astype(BF16), v).astype(BF16)


def _xa_seq(proj, mem_k, mem_v, batch, seq, tq):
    nt = seq // tq
    q_block = (POOL_WIDTH + SSM_WIDTH) // XA_WIDTH
    kv_spec = pl.BlockSpec((N_MEM, XA_WIDTH), lambda b, t: (b, 0))
    return pl.pallas_call(
        _xa_seq_kernel,
        grid=(batch, nt),
        in_specs=[pl.BlockSpec((tq, XA_WIDTH), lambda b, t: (b * nt + t, q_block)), kv_spec, kv_spec],
        out_specs=pl.BlockSpec((tq, XA_WIDTH), lambda b, t: (b * nt + t, 0)),
        out_shape=jax.ShapeDtypeStruct((batch * seq, XA_WIDTH), BF16),
        compiler_params=_params("parallel", "parallel"),
        name="xa_seq",
    )(proj, mem_k, mem_v)


XA_LANE_SPLIT = XA_HEAD_DIM // LANES


def _heads_on_sublanes(x):
    lead = x.shape[:-2]
    x = x.reshape(lead + (XA_HEADS, XA_LANE_SPLIT, LANES)).swapaxes(-3, -2)
    return x.reshape(lead + (XA_LANE_SPLIT * XA_HEADS, LANES))


def _heads_from_sublanes(x):
    lead = x.shape[:-2]
    x = x.reshape(lead + (XA_LANE_SPLIT, XA_HEADS, LANES)).swapaxes(-3, -2)
    return x.reshape(lead + (XA_HEADS, XA_HEAD_DIM))


def _xa_step_kernel(q_ref, k_ref, v_ref, o_ref):
    q = q_ref[...]
    part = jnp.sum(k_ref[...] * q, axis=-1, keepdims=True)
    s = (part + pltpu.roll(part, XA_HEADS, axis=2)) * XA_SCALE
    e = jnp.exp(s - jnp.max(s, axis=1, keepdims=True))
    r = 1.0 / jnp.sum(e, axis=1, keepdims=True)
    o_ref[...] = jnp.sum(e * v_ref[...], axis=1, keepdims=True) * r


def _xa_step(q4, cache_k, cache_v, tb):
    assert XA_LANE_SPLIT * XA_HEADS == SUBLANES
    m = q4.shape[0]
    kv_spec = pl.BlockSpec((tb, N_MEM, SUBLANES, LANES), lambda i: (i, 0, 0, 0))
    q_spec = pl.BlockSpec((tb, 1, SUBLANES, LANES), lambda i: (i, 0, 0, 0))
    o = pl.pallas_call(
        _xa_step_kernel,
        grid=(m // tb,),
        in_specs=[q_spec, kv_spec, kv_spec],
        out_specs=q_spec,
        out_shape=jax.ShapeDtypeStruct((m, 1, SUBLANES, LANES), F32),
        compiler_params=_params("parallel"),
        name="xa_step",
    )(_heads_on_sublanes(q4), _heads_on_sublanes(cache_k), _heads_on_sublanes(cache_v))
    return _heads_from_sublanes(o)


def _gate_kernel(z_ref, g_ref, o_ref, ga_ref, gb_ref, gc_ref, wpo_ref, wv_ref, wg_ref, wxo_ref, m_ref, *copy_refs):
    copies = copy_refs if copy_refs else (None,) * 4
    g = g_ref[...]
    o_pool = _dot(z_ref[...], _weight(wpo_ref, copies[0]))
    o_ssm = _dot(g, _weight(wv_ref, copies[1])) * jax.nn.sigmoid(_dot(g, _weight(wg_ref, copies[2])))
    o_xa = _dot(o_ref[...], _weight(wxo_ref, copies[3]))
    merged = (jax.nn.sigmoid(ga_ref[...]) * o_pool + jax.nn.sigmoid(gb_ref[...]) * o_ssm
              + jax.nn.sigmoid(gc_ref[...]) * o_xa)
    m_ref[...] = merged.astype(BF16)


def _gate(z, g, o, proj, weights, tm, tn):
    m, half = z.shape
    emit_w = weights[0].dtype != BF16
    assert not emit_w or m == tm
    d = weights[0].shape[1] if emit_w else weights[0].shape[0] * tn
    gate0 = OFF_GATE // tn
    per_gate = d // tn
    branch_spec = pl.BlockSpec((tm, half), lambda i, n: (i, 0))
    w_specs, copy_specs, copy_shapes = map(list, zip(*[_col_tiled(w, tn) for w in weights]))
    if not emit_w:
        copy_specs, copy_shapes = [], []

    def gate_spec(k):
        return pl.BlockSpec((tm, tn), lambda i, n: (i, gate0 + k * per_gate + n))

    outs = pl.pallas_call(
        _gate_kernel,
        grid=(m // tm, d // tn),
        in_specs=[branch_spec] * 3 + [gate_spec(0), gate_spec(1), gate_spec(2)] + w_specs,
        out_specs=[pl.BlockSpec((tm, tn), lambda i, n: (i, n))] + copy_specs,
        out_shape=[jax.ShapeDtypeStruct((m, d), BF16)] + copy_shapes,
        compiler_params=_params("parallel", "arbitrary"),
        name="gate",
    )(z, g, o, proj, proj, proj, *weights)
    return outs[0], (tuple(outs[1:]) if emit_w else tuple(weights))


def _mixout_kernel(m_ref, wout_ref, h_ref, gpost_ref, gnext_ref, h2_ref, xn_ref, *copy_ref, sub):
    wout = _weight(wout_ref, *copy_ref)
    for r in range(m_ref.shape[0] // sub):
        rows = slice(r * sub, (r + 1) * sub)
        h2 = h_ref[rows, :] + _rms(_dot(m_ref[rows, :], wout), gpost_ref[...])
        h2_ref[rows, :] = h2
        xn_ref[rows, :] = _rms(h2, gnext_ref[...]).astype(BF16)


def _mixout(merged, w_out, h, g_post, g_next, tm):
    m, d = h.shape
    emit_w = w_out.dtype != BF16
    assert not emit_w or m == tm
    row_spec = pl.BlockSpec((tm, d), lambda i: (i, 0))
    gain_spec = pl.BlockSpec((1, d), lambda i: (0, 0))
    w_spec = pl.BlockSpec((d, d), lambda i: (0, 0))
    copy_specs, copy_shapes = _with_copies([w_spec], [w_out], emit_w)
    outs = pl.pallas_call(
        functools.partial(_mixout_kernel, sub=min(tm, TILES['mix_sub'])),
        grid=(m // tm,),
        in_specs=[row_spec, w_spec, row_spec, gain_spec, gain_spec],
        out_specs=[row_spec, row_spec] + copy_specs,
        out_shape=[jax.ShapeDtypeStruct((m, d), F32), jax.ShapeDtypeStruct((m, d), BF16)] + copy_shapes,
        compiler_params=_params("parallel"),
        name="mixout",
    )(merged, w_out, h, g_post, g_next)
    return outs[:2], (outs[2] if emit_w else w_out)


def _row_tile(m, target):
    return target if m % target == 0 else m


FFN1_W = ('w_ff1_gate', 'w_ff1_up', 'w_ff1_down')
FFN2_W = ('w_ff2_gate', 'w_ff2_up', 'w_ff2_down')
GATE_W = ('w_pool_out', 'w_glu_val', 'w_glu_gate', 'w_xa_out')
TILES = dict(tm=512, tm_ffn2=1024, tf=512, tm_proj=2048, tn_proj=1024, tm_gate=1024, tn=512, tm_mem=512,
             mix_sub=128, pool_seq=512, xa_seq=512, pool_step=32, xa_step=4)


def _layer(x, w, mixers):
    m = x.shape[0]
    tm = _row_tile(m, TILES['tm'])
    wb = dict(w)
    (h, xn), wb_ffn1 = _ffn(x, w['g_ff1_pre'], [w[k] for k in FFN1_W], w['g_ff1_post'], tm, TILES['tf'],
                            g_next=w['g_mix_pre'])
    proj, wb['w_in'] = _proj(xn, w['w_in'], _row_tile(m, TILES['tm_proj']), TILES['tn_proj'])
    z, g, o, states = mixers(proj)
    merged, wb_gate = _gate(z, g, o, proj, [w[k] for k in GATE_W], _row_tile(m, TILES['tm_gate']), TILES['tn'])
    (h, xn), wb['w_out'] = _mixout(merged, w['w_out'], h, w['g_mix_post'], w['g_ff2_pre'], tm)
    (y,), wb_ffn2 = _ffn(h, xn, [w[k] for k in FFN2_W], w['g_ff2_post'], _row_tile(m, TILES['tm_ffn2']),
                         TILES['tf'])
    wb.update(zip(FFN1_W + GATE_W + FFN2_W, wb_ffn1 + wb_gate + wb_ffn2))
    return y, states, wb


def kernel(x_prompt, x_sample, mem_prompt, cache_mem_k, cache_mem_v, state_pool, state_ssm_re, state_ssm_im, g_ff1_pre, w_ff1_gate, w_ff1_up, w_ff1_down, g_ff1_post, g_mix_pre, w_in, w_pool_grp, pool_scale, w_pool_out, ssm_a_re, ssm_a_im, ssm_log_step, ssm_b_re, ssm_b_im, ssm_c_re, ssm_c_im, ssm_d, w_glu_val, w_glu_gate, g_mem, w_mem_k, w_mem_v, w_xa_out, w_out, g_mix_post, g_ff2_pre, w_ff2_gate, w_ff2_up, w_ff2_down, g_ff2_post):
    bp, lp, d = x_prompt.shape
    bs, ls, _ = x_sample.shape
    depth = w_in.shape[0]
    assert depth == 1 and ls == 1 and lp >= POOL_BUF and lp % SSM_TIME_TILE == 0

    mats = dict(w_ff1_gate=w_ff1_gate, w_ff1_up=w_ff1_up, w_ff1_down=w_ff1_down, w_in=w_in,
                w_pool_out=w_pool_out, w_glu_val=w_glu_val, w_glu_gate=w_glu_gate, w_xa_out=w_xa_out,
                w_out=w_out, w_ff2_gate=w_ff2_gate, w_ff2_up=w_ff2_up, w_ff2_down=w_ff2_down)
    w = {k: v[0] for k, v in mats.items()}
    w.update(g_ff1_pre=g_ff1_pre, g_ff1_post=g_ff1_post, g_mix_pre=g_mix_pre, g_mix_post=g_mix_post,
             g_ff2_pre=g_ff2_pre, g_ff2_post=g_ff2_post)
    w_grp = w_pool_grp[0].astype(BF16)
    scale = pool_scale
    d_skip = ssm_d

    bd, cd, lb, tab = _ssm_disc(ssm_a_re[0], ssm_a_im[0], ssm_log_step[0], ssm_b_re[0], ssm_b_im[0],
                                ssm_c_re[0], ssm_c_im[0])

    def sample_mixers(proj):
        z, new_buf = _pool_step(proj, state_pool[0].swapaxes(0, 1), w_grp, scale, TILES['pool_step'])
        g, x_re, x_im = _ssm_step(proj, state_ssm_re[0].reshape(bs, SSM_LANES),
                                  state_ssm_im[0].reshape(bs, SSM_LANES), lb, bd, cd, d_skip)
        q4 = proj[:, POOL_WIDTH + SSM_WIDTH:OFF_GATE].reshape(bs, 1, XA_HEADS, XA_HEAD_DIM)
        o = _xa_step(q4, cache_mem_k[0], cache_mem_v[0], TILES['xa_step'])
        return z, g, o.reshape(bs, XA_WIDTH).astype(BF16), (new_buf, x_re, x_im)

    y_sample, (pool_s, re_s, im_s), wb = _layer(x_sample.reshape(bs * ls, d), w, sample_mixers)

    mem = mem_prompt.reshape(bp * N_MEM, d)
    mem_k, mem_k_heads, mem_v, mem_v_heads = _memproj(mem, g_mem, w_mem_k[0], w_mem_v[0])

    def prompt_mixers(proj):
        z = _pool_seq(proj, w_grp, scale, bp, lp, TILES['pool_seq'])
        zeros = jnp.zeros((bp, 1, SSM_LANES), F32)
        g, last_re, last_im = _ssm_seq(proj, zeros, zeros, tab, bd, cd, d_skip, bp, lp)
        o = _xa_seq(proj, mem_k, mem_v, bp, lp, TILES['xa_seq'])
        pool_rows = proj.reshape(bp, lp, IN_WIDTH)[:, lp - POOL_BUF:, :POOL_WIDTH]
        return z, g, o, (pool_rows, last_re, last_im)

    y_prompt, (pool_p, re_p, im_p), _ = _layer(x_prompt.reshape(bp * lp, d), wb, prompt_mixers)

    kv_shape = (1, bp, N_MEM, XA_HEADS, XA_HEAD_DIM)
    st_shape = (SSM_GROUPS, SSM_STATE)
    return (y_prompt.reshape(bp, lp, d), y_sample.reshape(bs, ls, d),
            mem_k_heads.reshape(kv_shape), mem_v_heads.reshape(kv_shape),
            pool_p[None],
            re_p.reshape((1, bp) + st_shape), im_p.reshape((1, bp) + st_shape),
            pool_s.swapaxes(0, 1)[None],
            re_s.reshape((1, bs) + st_shape), im_s.reshape((1, bs) + st_shape))
```

```python
import functools
import math

import jax
import jax.numpy as jnp
from jax import lax
from jax.experimental import pallas as pl
from jax.experimental.pallas import tpu as pltpu

F32 = jnp.float32
BF16 = jnp.bfloat16

D_MODEL = 2048
PAST_LEN = 16384
POOL_WIDTH = D_MODEL // 2
POOL_WINDOWS = (2, 4, 8, 16)
POOL_GROUP_DIM = POOL_WIDTH // len(POOL_WINDOWS)
POOL_BUF = max(POOL_WINDOWS) - 1
SSM_WIDTH = D_MODEL // 2
SSM_GROUP_CH = 16
SSM_GROUPS = SSM_WIDTH // SSM_GROUP_CH
SSM_STATE = 64
SSM_LANES = SSM_GROUPS * SSM_STATE
XA_HEADS = 4
XA_HEAD_DIM = D_MODEL // 8
XA_WIDTH = XA_HEADS * XA_HEAD_DIM
XA_SCALE = XA_HEAD_DIM ** -0.5
N_MEM = 256
OFF_GATE = POOL_WIDTH + SSM_WIDTH + XA_WIDTH
IN_WIDTH = OFF_GATE + 3 * D_MODEL
RMS_EPS = 1e-6

SUBLANES = 8
LANES = 128
MXU_DIM = 256
VMEM_LIMIT_BYTES = 60 * 1024 * 1024
DOWN_CHUNKS = 4
NORM_ROWS = 2 * SUBLANES

SSM_CHUNKS = SSM_WIDTH // MXU_DIM
SSM_CHUNK_LANES = SSM_LANES // SSM_CHUNKS
SSM_TIME_TILE = 256
SSM_SEG = SSM_TIME_TILE // SUBLANES
SCAN_LANES = 4 * LANES


def _params(*sem):
    return pltpu.CompilerParams(dimension_semantics=sem, vmem_limit_bytes=VMEM_LIMIT_BYTES)


def _rms(x, g, half=False):
    scale = lax.rsqrt(jnp.mean(x * x, axis=-1, keepdims=True) + RMS_EPS)
    return (x * (0.5 * scale if half else scale)) * g


def _dot(a, b):
    return jnp.dot(a, b, preferred_element_type=F32)


def _cmul(a_re, a_im, b_re, b_im):
    return a_re * b_re - a_im * b_im, a_re * b_im + a_im * b_re


def _weight(w_ref, copy_ref=None):
    w = w_ref[...]
    if w.dtype != BF16:
        w = w.astype(BF16)
    if copy_ref is not None:
        copy_ref[...] = w
    return w


def _with_copies(specs, weights, emit_w):
    if not emit_w:
        return [], []
    return list(specs), [jax.ShapeDtypeStruct(w.shape, BF16) for w in weights]


def _col_tiled(w, tn):
    tiled = lambda k: pl.BlockSpec((None, k, tn), lambda i, j: (j, 0, 0))
    if w.dtype == BF16:
        assert w.ndim == 3 and w.shape[2] == tn, "bf16 weights come tile-major from an earlier call"
        return tiled(w.shape[1]), None, None
    k, n = w.shape
    return pl.BlockSpec((k, tn), lambda i, j: (0, j)), tiled(k), jax.ShapeDtypeStruct((n // tn, k, tn), BF16)


def _swiglu_accumulate(xn, wg, wu, wd_cols, o_ref):
    g = _dot(xn, wg)
    u = _dot(xn, wu)
    hid = ((g * jax.nn.sigmoid(g)) * u).astype(BF16)
    cols = o_ref.shape[1] // DOWN_CHUNKS
    for n in range(DOWN_CHUNKS):
        sl = slice(n * cols, (n + 1) * cols)
        o_ref[:, sl] += _dot(hid, wd_cols(sl))


def _ffn_stream_kernel(x_ref, gpre_ref, wg_hbm, wu_hbm, wd_hbm, gpost_ref, gnext_ref, o_ref, xnext_ref,
                       xn_ref, wg_buf, wu_buf, wd_buf, sem):
    i = pl.program_id(0)
    n_tiles, _, tf = wg_hbm.shape

    def slot(k):
        return 0 if k == 0 else 2 - k % 2

    def copies(k):
        s = slot(k)
        return (pltpu.make_async_copy(wg_hbm.at[k], wg_buf.at[s], sem.at[0, s]),
                pltpu.make_async_copy(wu_hbm.at[k], wu_buf.at[s], sem.at[1, s]),
                pltpu.make_async_copy(wd_hbm.at[pl.ds(k * tf, tf), :], wd_buf.at[s], sem.at[2, s]))

    def start(k):
        for c in copies(k):
            c.start()

    @pl.when(i == 0)
    def _():
        start(0)

    xn_ref[...] = _rms(x_ref[...], gpre_ref[...]).astype(BF16)
    o_ref[...] = jnp.zeros_like(o_ref)
    xn = xn_ref[...]
    for k in range(n_tiles):
        s = slot(k)
        if k + 1 < n_tiles:
            start(k + 1)
        else:
            @pl.when(i + 1 < pl.num_programs(0))
            def _():
                start(0)
        for c in copies(k):
            c.wait()
        _swiglu_accumulate(xn, wg_buf[s], wu_buf[s], lambda cols, s=s: wd_buf[s, :, cols], o_ref)

    out = x_ref[...] + _rms(o_ref[...], gpost_ref[...], half=True)
    o_ref[...] = out
    xnext_ref[...] = _rms(out, gnext_ref[...]).astype(BF16)


def _ffn_stream(x, g_pre, weights, g_post, g_next, tm):
    m, d = x.shape
    w_gate, w_up, w_down = weights
    n_tiles, _, tf = w_gate.shape
    row_spec = pl.BlockSpec((tm, d), lambda i: (i, 0))
    gain_spec = pl.BlockSpec((1, d), lambda i: (0, 0))
    hbm_spec = pl.BlockSpec(memory_space=pl.ANY)
    w_slots = 3
    return pl.pallas_call(
        _ffn_stream_kernel,
        grid=(m // tm,),
        in_specs=[row_spec, gain_spec, hbm_spec, hbm_spec, hbm_spec, gain_spec, gain_spec],
        out_specs=[row_spec, row_spec],
        out_shape=[jax.ShapeDtypeStruct((m, d), F32), jax.ShapeDtypeStruct((m, d), BF16)],
        scratch_shapes=[pltpu.VMEM((tm, d), BF16),
                        pltpu.VMEM((w_slots, d, tf), BF16), pltpu.VMEM((w_slots, d, tf), BF16),
                        pltpu.VMEM((w_slots, tf, d), BF16), pltpu.SemaphoreType.DMA((3, w_slots))],
        compiler_params=_params("arbitrary"),
        name="ffn_stream",
    )(x, g_pre, w_gate, w_up, w_down, g_post, g_next)


def _ffn_kernel(*refs, first, emit_w):
    refs = list(refs)
    x_ref, pre_ref, wg_ref, wu_ref, wd_ref, gpost_ref = refs[:6]
    del refs[:6]
    gnext_ref = refs.pop(0) if first else None
    o_ref = refs.pop(0)
    xnext_ref = refs.pop(0) if first else None
    wgb_ref, wub_ref, wdb_ref = (refs.pop(0), refs.pop(0), refs.pop(0)) if emit_w else (None, None, None)
    if first:
        xn_ref, = refs
    else:
        xn_ref = pre_ref
        xbuf_ref, xsem = refs
    i = pl.program_id(0)
    j = pl.program_id(1)
    tm = o_ref.shape[0]

    def x_copy():
        return pltpu.make_async_copy(x_ref.at[pl.ds(i * tm, tm), :], xbuf_ref, xsem)

    @pl.when(j == 0)
    def _():
        if first:
            xn_ref[...] = _rms(x_ref[...], pre_ref[...]).astype(BF16)
        else:
            x_copy().start()
        o_ref[...] = jnp.zeros_like(o_ref)

    wd = _weight(wd_ref, wdb_ref)
    _swiglu_accumulate(xn_ref[...], _weight(wg_ref, wgb_ref), _weight(wu_ref, wub_ref),
                       lambda cols: wd[:, cols], o_ref)

    @pl.when(j == pl.num_programs(1) - 1)
    def _():
        if first:
            out = x_ref[...] + _rms(o_ref[...], gpost_ref[...], half=True)
            o_ref[...] = out
            xnext_ref[...] = _rms(out, gnext_ref[...]).astype(BF16)
        else:
            x_copy().wait()
            for r in range(tm // NORM_ROWS):
                rs = slice(r * NORM_ROWS, (r + 1) * NORM_ROWS)
                o_ref[rs, :] = xbuf_ref[rs, :] + _rms(o_ref[rs, :], gpost_ref[...], half=True)


def _ffn(x, pre, weights, g_post, tm, tf, g_next=None):
    m, d = x.shape
    w_gate, w_up, w_down = weights
    f = w_down.shape[0]
    first = g_next is not None
    emit_w = w_gate.dtype != BF16
    assert not emit_w or m == tm, "weight copies need every weight block visited exactly once"
    if first and not emit_w:
        return tuple(_ffn_stream(x, pre, weights, g_post, g_next, tm)), tuple(weights)
    row_spec = pl.BlockSpec((tm, d), lambda i, j: (i, 0))
    gain_spec = pl.BlockSpec((1, d), lambda i, j: (0, 0))
    wg_spec, wg_copy, wg_shape = _col_tiled(w_gate, tf)
    wu_spec, wu_copy, wu_shape = _col_tiled(w_up, tf)
    wd_spec = pl.BlockSpec((tf, d), lambda i, j: (j, 0))
    w_specs = [wg_spec, wu_spec, wd_spec]
    copy_specs = [wg_copy, wu_copy, wd_spec] if emit_w else []
    copy_shapes = [wg_shape, wu_shape, jax.ShapeDtypeStruct(w_down.shape, BF16)] if emit_w else []
    outs = pl.pallas_call(
        functools.partial(_ffn_kernel, first=first, emit_w=emit_w),
        grid=(m // tm, f // tf),
        in_specs=([row_spec, gain_spec] if first else [pl.BlockSpec(memory_space=pl.ANY), row_spec])
        + w_specs + [gain_spec] + ([gain_spec] if first else []),
        out_specs=[row_spec] + ([row_spec] if first else []) + copy_specs,
        out_shape=[jax.ShapeDtypeStruct((m, d), F32)] + ([jax.ShapeDtypeStruct((m, d), BF16)] if first else [])
        + copy_shapes,
        scratch_shapes=[pltpu.VMEM((tm, d), BF16)] if first
        else [pltpu.VMEM((tm, d), F32), pltpu.SemaphoreType.DMA(())],
        compiler_params=_params("parallel", "arbitrary"),
        name="ffn",
    )(*((x, pre, w_gate, w_up, w_down, g_post) + ((g_next,) if first else ())))
    n_out = 2 if first else 1
    return outs[:n_out], (tuple(outs[n_out:]) if emit_w else tuple(weights))


def _proj_kernel(xn_ref, w_ref, o_ref, *copy_ref):
    o_ref[...] = _dot(xn_ref[...], _weight(w_ref, *copy_ref))


def _proj(xn, w, tm, tn):
    m, d = xn.shape
    emit_w = w.dtype != BF16
    assert not emit_w or m == tm
    n = w.shape[1] if emit_w else w.shape[0] * tn
    w_spec, copy_spec, copy_shape = _col_tiled(w, tn)
    copy_specs, copy_shapes = ([copy_spec], [copy_shape]) if emit_w else ([], [])
    outs = pl.pallas_call(
        _proj_kernel,
        grid=(m // tm, n // tn),
        in_specs=[pl.BlockSpec((tm, d), lambda i, j: (i, 0)), w_spec],
        out_specs=[pl.BlockSpec((tm, tn), lambda i, j: (i, j))] + copy_specs,
        out_shape=[jax.ShapeDtypeStruct((m, n), F32)] + copy_shapes,
        compiler_params=_params("parallel", "arbitrary"),
        name="proj",
    )(xn, w)
    return outs[0], (outs[1] if emit_w else w)


def _memproj_kernel(x_ref, g_ref, wk_ref, wv_ref, k_ref, kh_ref, v_ref, vh_ref):
    xn = _rms(x_ref[...], g_ref[...]).astype(BF16)
    for w_ref, o_ref, oh_ref in ((wk_ref, k_ref, kh_ref), (wv_ref, v_ref, vh_ref)):
        res = _dot(xn, _weight(w_ref))
        o_ref[...] = res
        for h in range(XA_HEADS):
            for p in range(XA_LANE_SPLIT):
                lane0 = h * XA_HEAD_DIM + p * LANES
                oh_ref[:, p * XA_HEADS + h, :] = res[:, lane0:lane0 + LANES]


def _memproj(x, g, w_k, w_v):
    m, d = x.shape
    n = w_k.shape[1]
    assert n == XA_WIDTH
    tm = _row_tile(m, TILES['tm_mem'])
    w_spec = pl.BlockSpec((d, n), lambda i: (0, 0), pipeline_mode=pl.Buffered(1))
    flat_spec = pl.BlockSpec((tm, n), lambda i: (i, 0))
    head_spec = pl.BlockSpec((tm, SUBLANES, LANES), lambda i: (i, 0, 0))
    flat_shape = jax.ShapeDtypeStruct((m, n), F32)
    head_shape = jax.ShapeDtypeStruct((m, SUBLANES, LANES), F32)
    k, kh, v, vh = pl.pallas_call(
        _memproj_kernel,
        grid=(m // tm,),
        in_specs=[pl.BlockSpec((tm, d), lambda i: (i, 0)), pl.BlockSpec((1, d), lambda i: (0, 0)), w_spec, w_spec],
        out_specs=[flat_spec, head_spec, flat_spec, head_spec],
        out_shape=[flat_shape, head_shape, flat_shape, head_shape],
        compiler_params=_params("parallel"),
        name="memproj",
    )(x, g, w_k, w_v)
    return k, _heads_from_sublanes(kh), v, _heads_from_sublanes(vh)


def _pool_group_out(k, diff, wgrp_ref, scale_ref, z_ref):
    ch = slice(k * POOL_GROUP_DIM, (k + 1) * POOL_GROUP_DIM)
    zk = _dot(diff.astype(BF16), wgrp_ref[k]) * scale_ref[:, ch]
    z_ref[:, ch] = zk.astype(BF16)


def _pool_seq_kernel(u_ref, wgrp_ref, scale_ref, z_ref, ext_ref, *, halo):
    t = pl.program_id(1)
    tt = u_ref.shape[0]

    @pl.when(t == 0)
    def _():
        ext_ref[0:halo, :] = jnp.zeros((halo, POOL_WIDTH), F32)

    u = u_ref[...]
    ext_ref[halo:halo + tt, :] = u
    pos = t * tt + lax.broadcasted_iota(jnp.int32, (tt, 1), 0)
    for k, w in enumerate(POOL_WINDOWS):
        ch = slice(k * POOL_GROUP_DIM, (k + 1) * POOL_GROUP_DIM)
        a = ext_ref[:, ch]
        d = 1
        while d < w:
            a = a + pltpu.roll(a, d, axis=0)
            d *= 2
        count = jnp.minimum(pos + 1, w).astype(F32)
        diff = a[halo:, :] / count - u[:, ch]
        _pool_group_out(k, diff, wgrp_ref, scale_ref, z_ref)
    ext_ref[0:halo, :] = ext_ref[tt:tt + halo, :]


def _pool_seq(proj, w_grp, scale, batch, seq, tt):
    halo = 2 * SUBLANES
    nt = seq // tt
    return pl.pallas_call(
        functools.partial(_pool_seq_kernel, halo=halo),
        grid=(batch, nt),
        in_specs=[
            pl.BlockSpec((tt, POOL_WIDTH), lambda b, t: (b * nt + t, 0)),
            pl.BlockSpec(w_grp.shape, lambda b, t: (0, 0, 0)),
            pl.BlockSpec((1, POOL_WIDTH), lambda b, t: (0, 0)),
        ],
        out_specs=pl.BlockSpec((tt, POOL_WIDTH), lambda b, t: (b * nt + t, 0)),
        out_shape=jax.ShapeDtypeStruct((batch * seq, POOL_WIDTH), BF16),
        scratch_shapes=[pltpu.VMEM((tt + halo, POOL_WIDTH), F32)],
        compiler_params=_params("parallel", "arbitrary"),
        name="pool_seq",
    )(proj, w_grp, scale)


def _pool_step_kernel(u_ref, buf_ref, wgrp_ref, scale_ref, z_ref, nbuf_ref):
    u = u_ref[...]
    for k, w in enumerate(POOL_WINDOWS):
        ch = slice(k * POOL_GROUP_DIM, (k + 1) * POOL_GROUP_DIM)
        s = u[:, ch]
        for j in range(1, w):
            s = s + buf_ref[POOL_BUF - j, :, ch]
        diff = s / float(min(PAST_LEN + 1, w)) - u[:, ch]
        _pool_group_out(k, diff, wgrp_ref, scale_ref, z_ref)
    nbuf_ref[0:POOL_BUF - 1] = buf_ref[1:POOL_BUF]
    nbuf_ref[POOL_BUF - 1] = u


def _pool_step(proj, buf, w_grp, scale, tb):
    m = proj.shape[0]
    buf_spec = pl.BlockSpec((POOL_BUF, tb, POOL_WIDTH), lambda i: (0, i, 0))
    return pl.pallas_call(
        _pool_step_kernel,
        grid=(m // tb,),
        in_specs=[
            pl.BlockSpec((tb, POOL_WIDTH), lambda i: (i, 0)),
            buf_spec,
            pl.BlockSpec(w_grp.shape, lambda i: (0, 0, 0)),
            pl.BlockSpec((1, POOL_WIDTH), lambda i: (0, 0)),
        ],
        out_specs=[pl.BlockSpec((tb, POOL_WIDTH), lambda i: (i, 0)), buf_spec],
        out_shape=[jax.ShapeDtypeStruct((m, POOL_WIDTH), BF16), jax.ShapeDtypeStruct(buf.shape, F32)],
        compiler_params=_params("parallel"),
        name="pool_step",
    )(proj, buf, w_grp, scale)


def _zoh_lambda(a_re, a_im, log_step):
    dt = jnp.exp(log_step)
    mag = jnp.exp(a_re * dt)
    ang = a_im * dt
    return mag * jnp.cos(ang), mag * jnp.sin(ang)


def _tile_diag(blocks, rows_per_group):
    groups = SSM_GROUPS // SSM_CHUNKS
    rows, c = blocks.shape
    wide = groups * c
    spread = (lax.broadcasted_iota(jnp.int32, (c, wide), 1) % c
              == lax.broadcasted_iota(jnp.int32, (c, wide), 0)).astype(BF16)
    tiled = _dot(blocks.astype(BF16), spread)
    keep = (lax.broadcasted_iota(jnp.int32, (rows, wide), 0) // rows_per_group
            == lax.broadcasted_iota(jnp.int32, (rows, wide), 1) // c)
    return jnp.where(keep, tiled, 0.0).astype(BF16)


def _ssm_disc_kernel(a_re_ref, a_im_ref, ls_ref, a_re3_ref, a_im3_ref, ls3_ref, bt_ref, ct_ref,
                     bd_ref, cd_ref, lb_ref, tab_ref):
    a_re = a_re3_ref[...]
    a_im = a_im3_ref[...]
    lb_re, lb_im = _zoh_lambda(a_re, a_im, ls3_ref[...])
    den = a_re * a_re + a_im * a_im
    n_re = lb_re - 1.0
    f_re = (n_re * a_re + lb_im * a_im) / den
    f_im = (lb_im * a_re - n_re * a_im) / den
    b_re = bt_ref[0]
    b_im = bt_ref[1]
    bb = (f_re * b_re - f_im * b_im, f_re * b_im + f_im * b_re)
    groups = SSM_GROUPS // SSM_CHUNKS
    for part in range(2):
        for c in range(SSM_CHUNKS):
            grp = slice(c * groups, (c + 1) * groups)
            bd_ref[part, c] = _tile_diag(
                bb[part][grp].reshape(groups * SSM_GROUP_CH, SSM_STATE), SSM_GROUP_CH)
            cd_ref[part, c] = _tile_diag(
                ct_ref[part, grp].reshape(groups * SSM_STATE, SSM_GROUP_CH), SSM_STATE)

    lb_re, lb_im = _zoh_lambda(a_re_ref[...], a_im_ref[...], ls_ref[...])
    lb_ref[0] = lb_re
    lb_ref[1] = lb_im
    s_re, s_im = lb_re, lb_im
    for _ in range(int(math.log2(SSM_SEG))):
        s_re, s_im = _cmul(s_re, s_im, s_re, s_im)
    powers = [(s_re, s_im)]
    for _ in range(1, SUBLANES):
        powers.append(_cmul(*powers[-1], s_re, s_im))
    zero = jnp.zeros_like(lb_re)
    for r in range(SUBLANES):
        tab_ref[0, r] = lb_re
        tab_ref[1, r] = lb_im
        for i in range(3):
            d = 1 << i
            tab_ref[2 + 2 * i, r] = powers[d - 1][0] if r >= d else zero
            tab_ref[3 + 2 * i, r] = powers[d - 1][1] if r >= d else zero
        tab_ref[8, r] = powers[r][0]
        tab_ref[9, r] = powers[r][1]


def _ssm_disc(a_re, a_im, log_step, b_re, b_im, c_re, c_im):
    g, n, h = b_re.shape
    bt = jnp.stack([b_re, b_im]).swapaxes(2, 3)
    ct = jnp.stack([c_re, c_im]).swapaxes(2, 3)
    per_chunk = g // SSM_CHUNKS
    bd, cd, lb, tab = pl.pallas_call(
        _ssm_disc_kernel,
        out_shape=[jax.ShapeDtypeStruct((2, SSM_CHUNKS, per_chunk * h, per_chunk * n), BF16),
                   jax.ShapeDtypeStruct((2, SSM_CHUNKS, per_chunk * n, per_chunk * h), BF16),
                   jax.ShapeDtypeStruct((2, g, n), F32),
                   jax.ShapeDtypeStruct((10, SUBLANES, g, n), F32)],
        name="ssm_disc",
    )(a_re, a_im, log_step[:, None], a_re[:, None, :], a_im[:, None, :], log_step[:, None, None], bt, ct)
    return bd, cd, lb.reshape(2, g * n), tab.reshape(10, SUBLANES, g * n)


def _ssm_in(ub, bd_ref, x_ref):
    for c in range(SSM_CHUNKS):
        x_ref[:, c * SSM_CHUNK_LANES:(c + 1) * SSM_CHUNK_LANES] = _dot(
            ub[:, c * MXU_DIM:(c + 1) * MXU_DIM], bd_ref[c])


def _ssm_out(c, u, x_re_ref, x_im_ref, cdr_ref, cdi_ref, d_ref):
    ch = slice(c * MXU_DIM, (c + 1) * MXU_DIM)
    st = slice(c * SSM_CHUNK_LANES, (c + 1) * SSM_CHUNK_LANES)
    y = _dot(x_re_ref[:, st].astype(BF16), cdr_ref[c]) - _dot(x_im_ref[:, st].astype(BF16), cdi_ref[c])
    return jax.nn.gelu(y + d_ref[:, ch] * u[:, ch])


def _ssm_seq_kernel(u_ref, h_re_ref, h_im_ref, tab_ref, bdr_ref, bdi_ref, cdr_ref, cdi_ref, d_ref,
                    g_ref, last_re_ref, last_im_ref, xr_ref, xi_ref, cr_ref, ci_ref):
    t = pl.program_id(1)
    tt = u_ref.shape[0]
    seg = tt // SUBLANES

    @pl.when(t == 0)
    def _():
        cr_ref[...] = jnp.broadcast_to(h_re_ref[...], cr_ref.shape)
        ci_ref[...] = jnp.broadcast_to(h_im_ref[...], ci_ref.shape)

    def regroup(x, rows_major, rows_minor):
        w = x.shape[1]
        return jnp.swapaxes(x.reshape(rows_major, rows_minor, w), 0, 1).reshape(tt, w)

    u = regroup(u_ref[...], SUBLANES, seg)
    ub = u.astype(BF16)
    _ssm_in(ub, bdr_ref, xr_ref)
    _ssm_in(ub, bdi_ref, xi_ref)

    first_row = lax.broadcasted_iota(jnp.int32, (SUBLANES, SCAN_LANES), 0) == 0
    for lc in range(SSM_LANES // SCAN_LANES):
        sl = slice(lc * SCAN_LANES, (lc + 1) * SCAN_LANES)
        l_re = tab_ref[0, :, sl]
        l_im = tab_ref[1, :, sl]

        def rows(j):
            return pl.ds(j * SUBLANES, SUBLANES)

        def advance(j, x, sl=sl, l_re=l_re, l_im=l_im):
            x_re, x_im = _cmul(l_re, l_im, *x)
            x_re = x_re + xr_ref[rows(j), sl]
            x_im = x_im + xi_ref[rows(j), sl]
            xr_ref[rows(j), sl] = x_re
            xi_ref[rows(j), sl] = x_im
            return x_re, x_im

        zero = jnp.zeros((SUBLANES, SCAN_LANES), F32)
        f_re, f_im = zero, zero
        for j in range(seg):
            f_re, f_im = advance(j, (f_re, f_im))

        c_re = cr_ref[:, sl]
        c_im = ci_ref[:, sl]
        for i in range(3):
            s_re = pltpu.roll(f_re, 1 << i, axis=0)
            s_im = pltpu.roll(f_im, 1 << i, axis=0)
            m_re, m_im = _cmul(tab_ref[2 + 2 * i, :, sl], tab_ref[3 + 2 * i, :, sl], s_re, s_im)
            f_re, f_im = f_re + m_re, f_im + m_im
        m_re, m_im = _cmul(tab_ref[8, :, sl], tab_ref[9, :, sl], c_re, c_im)
        f_re, f_im = f_re + m_re, f_im + m_im
        e_re = jnp.where(first_row, c_re, pltpu.roll(f_re, 1, axis=0))
        e_im = jnp.where(first_row, c_im, pltpu.roll(f_im, 1, axis=0))
        cr_ref[:, sl] = jnp.broadcast_to(f_re[SUBLANES - 1:SUBLANES, :], (SUBLANES, SCAN_LANES))
        ci_ref[:, sl] = jnp.broadcast_to(f_im[SUBLANES - 1:SUBLANES, :], (SUBLANES, SCAN_LANES))

        def fix(j, w, sl=sl, l_re=l_re, l_im=l_im):
            w_re, w_im = _cmul(l_re, l_im, *w)
            xr_ref[rows(j), sl] += w_re
            xi_ref[rows(j), sl] += w_im
            return w_re, w_im

        w = (e_re, e_im)
        for j in range(seg):
            w = fix(j, w)

    for c in range(SSM_CHUNKS):
        ch = slice(c * MXU_DIM, (c + 1) * MXU_DIM)
        g = _ssm_out(c, u, xr_ref, xi_ref, cdr_ref, cdi_ref, d_ref)
        g_ref[:, ch] = regroup(g, seg, SUBLANES).astype(BF16)

    @pl.when(t == pl.num_programs(1) - 1)
    def _():
        last_re_ref[...] = cr_ref[0:1, :]
        last_im_ref[...] = ci_ref[0:1, :]


def _const_spec(shape):
    return pl.BlockSpec(shape, lambda *_: (0,) * len(shape))


def _part_spec(stacked, part):
    rest = stacked.shape[1:]
    return pl.BlockSpec((None,) + rest, lambda *_: (part,) + (0,) * len(rest))


def _ssm_seq(proj, h_re, h_im, tab, bd, cd, d_skip, batch, seq):
    tt = SSM_TIME_TILE
    nt = seq // tt
    col_block = POOL_WIDTH // SSM_WIDTH
    state_spec = pl.BlockSpec((None, 1, SSM_LANES), lambda b, t: (b, 0, 0))
    state_shape = jax.ShapeDtypeStruct((batch, 1, SSM_LANES), F32)
    return pl.pallas_call(
        _ssm_seq_kernel,
        grid=(batch, nt),
        in_specs=[
            pl.BlockSpec((tt, SSM_WIDTH), lambda b, t: (b * nt + t, col_block)),
            state_spec, state_spec,
            _const_spec(tab.shape),
            _part_spec(bd, 0), _part_spec(bd, 1), _part_spec(cd, 0), _part_spec(cd, 1),
            _const_spec(d_skip.shape),
        ],
        out_specs=[pl.BlockSpec((tt, SSM_WIDTH), lambda b, t: (b * nt + t, 0)), state_spec, state_spec],
        out_shape=[jax.ShapeDtypeStruct((batch * seq, SSM_WIDTH), BF16), state_shape, state_shape],
        scratch_shapes=[
            pltpu.VMEM((tt, SSM_LANES), F32), pltpu.VMEM((tt, SSM_LANES), F32),
            pltpu.VMEM((SUBLANES, SSM_LANES), F32), pltpu.VMEM((SUBLANES, SSM_LANES), F32),
        ],
        compiler_params=_params("parallel", "arbitrary"),
        name="ssm_seq",
    )(proj, h_re, h_im, tab, bd, bd, cd, cd, d_skip)


def _ssm_step_kernel(u_ref, h_re_ref, h_im_ref, lb_ref, bdr_ref, bdi_ref, cdr_ref, cdi_ref, d_ref,
                     g_ref, x_re_ref, x_im_ref):
    u = u_ref[...]
    ub = u.astype(BF16)
    _ssm_in(ub, bdr_ref, x_re_ref)
    _ssm_in(ub, bdi_ref, x_im_ref)
    m_re, m_im = _cmul(lb_ref[0:1, :], lb_ref[1:2, :], h_re_ref[...], h_im_ref[...])
    x_re_ref[...] += m_re
    x_im_ref[...] += m_im
    for c in range(SSM_CHUNKS):
        ch = slice(c * MXU_DIM, (c + 1) * MXU_DIM)
        g_ref[:, ch] = _ssm_out(c, u, x_re_ref, x_im_ref, cdr_ref, cdi_ref, d_ref).astype(BF16)


def _ssm_step(proj, h_re, h_im, lb, bd, cd, d_skip):
    m = proj.shape[0]
    state_spec = pl.BlockSpec((m, SSM_LANES), lambda i: (0, 0))
    state_shape = jax.ShapeDtypeStruct((m, SSM_LANES), F32)
    return pl.pallas_call(
        _ssm_step_kernel,
        grid=(1,),
        in_specs=[
            pl.BlockSpec((m, SSM_WIDTH), lambda i: (0, POOL_WIDTH // SSM_WIDTH)),
            state_spec, state_spec,
            _const_spec(lb.shape),
            _part_spec(bd, 0), _part_spec(bd, 1), _part_spec(cd, 0), _part_spec(cd, 1),
            _const_spec(d_skip.shape),
        ],
        out_specs=[pl.BlockSpec((m, SSM_WIDTH), lambda i: (0, 0)), state_spec, state_spec],
        out_shape=[jax.ShapeDtypeStruct((m, SSM_WIDTH), BF16), state_shape, state_shape],
        compiler_params=_params("arbitrary"),
        name="ssm_step",
    )(proj, h_re, h_im, lb, bd, bd, cd, cd, d_skip)


def _xa_seq_kernel(q_ref, k_ref, v_ref, o_ref):
    for h in range(XA_HEADS):
        ch = slice(h * XA_HEAD_DIM, (h + 1) * XA_HEAD_DIM)
        q = q_ref[:, ch].astype(BF16)
        k = k_ref[:, ch].astype(BF16)
        v = v_ref[:, ch].astype(BF16)
        s = lax.dot_general(q, k, (((1,), (1,)), ((), ())), preferred_element_type=F32) * XA_SCALE
        e = jnp.exp(s - jnp.max(s, axis=-1, keepdims=True))
        p = e / jnp.sum(e, axis=-1, keepdims=True)
        o_ref[:, ch] = _dot(p.astype(BF16), v).astype(BF16)


def _xa_seq(proj, mem_k, mem_v, batch, seq, tq):
    nt = seq // tq
    q_block = (POOL_WIDTH + SSM_WIDTH) // XA_WIDTH
    kv_spec = pl.BlockSpec((N_MEM, XA_WIDTH), lambda b, t: (b, 0))
    return pl.pallas_call(
        _xa_seq_kernel,
        grid=(batch, nt),
        in_specs=[pl.BlockSpec((tq, XA_WIDTH), lambda b, t: (b * nt + t, q_block)), kv_spec, kv_spec],
        out_specs=pl.BlockSpec((tq, XA_WIDTH), lambda b, t: (b * nt + t, 0)),
        out_shape=jax.ShapeDtypeStruct((batch * seq, XA_WIDTH), BF16),
        compiler_params=_params("parallel", "parallel"),
        name="xa_seq",
    )(proj, mem_k, mem_v)


XA_LANE_SPLIT = XA_HEAD_DIM // LANES


def _heads_on_sublanes(x):
    lead = x.shape[:-2]
    x = x.reshape(lead + (XA_HEADS, XA_LANE_SPLIT, LANES)).swapaxes(-3, -2)
    return x.reshape(lead + (XA_LANE_SPLIT * XA_HEADS, LANES))


def _heads_from_sublanes(x):
    lead = x.shape[:-2]
    x = x.reshape(lead + (XA_LANE_SPLIT, XA_HEADS, LANES)).swapaxes(-3, -2)
    return x.reshape(lead + (XA_HEADS, XA_HEAD_DIM))


def _xa_step_kernel(q_ref, k_ref, v_ref, o_ref):
    q = q_ref[...]
    part = jnp.sum(k_ref[...] * q, axis=-1, keepdims=True)
    s = (part + pltpu.roll(part, XA_HEADS, axis=2)) * XA_SCALE
    e = jnp.exp(s - jnp.max(s, axis=1, keepdims=True))
    r = 1.0 / jnp.sum(e, axis=1, keepdims=True)
    o_ref[...] = jnp.sum(e * v_ref[...], axis=1, keepdims=True) * r


def _xa_step(q4, cache_k, cache_v, tb):
    assert XA_LANE_SPLIT * XA_HEADS == SUBLANES
    m = q4.shape[0]
    kv_spec = pl.BlockSpec((tb, N_MEM, SUBLANES, LANES), lambda i: (i, 0, 0, 0))
    q_spec = pl.BlockSpec((tb, 1, SUBLANES, LANES), lambda i: (i, 0, 0, 0))
    o = pl.pallas_call(
        _xa_step_kernel,
        grid=(m // tb,),
        in_specs=[q_spec, kv_spec, kv_spec],
        out_specs=q_spec,
        out_shape=jax.ShapeDtypeStruct((m, 1, SUBLANES, LANES), F32),
        compiler_params=_params("parallel"),
        name="xa_step",
    )(_heads_on_sublanes(q4), _heads_on_sublanes(cache_k), _heads_on_sublanes(cache_v))
    return _heads_from_sublanes(o)


def _gate_kernel(z_ref, g_ref, o_ref, ga_ref, gb_ref, gc_ref, wpo_ref, wv_ref, wg_ref, wxo_ref, m_ref, *copy_refs):
    copies = copy_refs if copy_refs else (None,) * 4
    g = g_ref[...]
    o_pool = _dot(z_ref[...], _weight(wpo_ref, copies[0]))
    o_ssm = _dot(g, _weight(wv_ref, copies[1])) * jax.nn.sigmoid(_dot(g, _weight(wg_ref, copies[2])))
    o_xa = _dot(o_ref[...], _weight(wxo_ref, copies[3]))
    merged = (jax.nn.sigmoid(ga_ref[...]) * o_pool + jax.nn.sigmoid(gb_ref[...]) * o_ssm
              + jax.nn.sigmoid(gc_ref[...]) * o_xa)
    m_ref[...] = merged.astype(BF16)


def _gate(z, g, o, proj, weights, tm, tn):
    m, half = z.shape
    emit_w = weights[0].dtype != BF16
    assert not emit_w or m == tm
    d = weights[0].shape[1] if emit_w else weights[0].shape[0] * tn
    gate0 = OFF_GATE // tn
    per_gate = d // tn
    branch_spec = pl.BlockSpec((tm, half), lambda i, n: (i, 0))
    w_specs, copy_specs, copy_shapes = map(list, zip(*[_col_tiled(w, tn) for w in weights]))
    if not emit_w:
        copy_specs, copy_shapes = [], []

    def gate_spec(k):
        return pl.BlockSpec((tm, tn), lambda i, n: (i, gate0 + k * per_gate + n))

    outs = pl.pallas_call(
        _gate_kernel,
        grid=(m // tm, d // tn),
        in_specs=[branch_spec] * 3 + [gate_spec(0), gate_spec(1), gate_spec(2)] + w_specs,
        out_specs=[pl.BlockSpec((tm, tn), lambda i, n: (i, n))] + copy_specs,
        out_shape=[jax.ShapeDtypeStruct((m, d), BF16)] + copy_shapes,
        compiler_params=_params("parallel", "arbitrary"),
        name="gate",
    )(z, g, o, proj, proj, proj, *weights)
    return outs[0], (tuple(outs[1:]) if emit_w else tuple(weights))


def _mixout_kernel(m_ref, wout_ref, h_ref, gpost_ref, gnext_ref, h2_ref, xn_ref, *copy_ref, sub):
    wout = _weight(wout_ref, *copy_ref)
    for r in range(m_ref.shape[0] // sub):
        rows = slice(r * sub, (r + 1) * sub)
        h2 = h_ref[rows, :] + _rms(_dot(m_ref[rows, :], wout), gpost_ref[...])
        h2_ref[rows, :] = h2
        xn_ref[rows, :] = _rms(h2, gnext_ref[...]).astype(BF16)


def _mixout(merged, w_out, h, g_post, g_next, tm):
    m, d = h.shape
    emit_w = w_out.dtype != BF16
    assert not emit_w or m == tm
    row_spec = pl.BlockSpec((tm, d), lambda i: (i, 0))
    gain_spec = pl.BlockSpec((1, d), lambda i: (0, 0))
    w_spec = pl.BlockSpec((d, d), lambda i: (0, 0))
    copy_specs, copy_shapes = _with_copies([w_spec], [w_out], emit_w)
    outs = pl.pallas_call(
        functools.partial(_mixout_kernel, sub=min(tm, TILES['mix_sub'])),
        grid=(m // tm,),
        in_specs=[row_spec, w_spec, row_spec, gain_spec, gain_spec],
        out_specs=[row_spec, row_spec] + copy_specs,
        out_shape=[jax.ShapeDtypeStruct((m, d), F32), jax.ShapeDtypeStruct((m, d), BF16)] + copy_shapes,
        compiler_params=_params("parallel"),
        name="mixout",
    )(merged, w_out, h, g_post, g_next)
    return outs[:2], (outs[2] if emit_w else w_out)


def _row_tile(m, target):
    return target if m % target == 0 else m


FFN1_W = ('w_ff1_gate', 'w_ff1_up', 'w_ff1_down')
FFN2_W = ('w_ff2_gate', 'w_ff2_up', 'w_ff2_down')
GATE_W = ('w_pool_out', 'w_glu_val', 'w_glu_gate', 'w_xa_out')
TILES = dict(tm=512, tm_ffn2=1024, tf=512, tm_proj=2048, tn_proj=1024, tm_gate=1024, tn=512, tm_mem=512,
             mix_sub=128, pool_seq=512, xa_seq=512, pool_step=32, xa_step=4)


def _layer(x, w, mixers):
    m = x.shape[0]
    tm = _row_tile(m, TILES['tm'])
    wb = dict(w)
    (h, xn), wb_ffn1 = _ffn(x, w['g_ff1_pre'], [w[k] for k in FFN1_W], w['g_ff1_post'], tm, TILES['tf'],
                            g_next=w['g_mix_pre'])
    proj, wb['w_in'] = _proj(xn, w['w_in'], _row_tile(m, TILES['tm_proj']), TILES['tn_proj'])
    z, g, o, states = mixers(proj)
    merged, wb_gate = _gate(z, g, o, proj, [w[k] for k in GATE_W], _row_tile(m, TILES['tm_gate']), TILES['tn'])
    (h, xn), wb['w_out'] = _mixout(merged, w['w_out'], h, w['g_mix_post'], w['g_ff2_pre'], tm)
    (y,), wb_ffn2 = _ffn(h, xn, [w[k] for k in FFN2_W], w['g_ff2_post'], _row_tile(m, TILES['tm_ffn2']),
                         TILES['tf'])
    wb.update(zip(FFN1_W + GATE_W + FFN2_W, wb_ffn1 + wb_gate + wb_ffn2))
    return y, states, wb


def kernel(x_prompt, x_sample, mem_prompt, cache_mem_k, cache_mem_v, state_pool, state_ssm_re, state_ssm_im, g_ff1_pre, w_ff1_gate, w_ff1_up, w_ff1_down, g_ff1_post, g_mix_pre, w_in, w_pool_grp, pool_scale, w_pool_out, ssm_a_re, ssm_a_im, ssm_log_step, ssm_b_re, ssm_b_im, ssm_c_re, ssm_c_im, ssm_d, w_glu_val, w_glu_gate, g_mem, w_mem_k, w_mem_v, w_xa_out, w_out, g_mix_post, g_ff2_pre, w_ff2_gate, w_ff2_up, w_ff2_down, g_ff2_post):
    bp, lp, d = x_prompt.shape
    bs, ls, _ = x_sample.shape
    depth = w_in.shape[0]
    assert depth == 1 and ls == 1 and lp >= POOL_BUF and lp % SSM_TIME_TILE == 0

    mats = dict(w_ff1_gate=w_ff1_gate, w_ff1_up=w_ff1_up, w_ff1_down=w_ff1_down, w_in=w_in,
                w_pool_out=w_pool_out, w_glu_val=w_glu_val, w_glu_gate=w_glu_gate, w_xa_out=w_xa_out,
                w_out=w_out, w_ff2_gate=w_ff2_gate, w_ff2_up=w_ff2_up, w_ff2_down=w_ff2_down)
    w = {k: v[0] for k, v in mats.items()}
    w.update(g_ff1_pre=g_ff1_pre, g_ff1_post=g_ff1_post, g_mix_pre=g_mix_pre, g_mix_post=g_mix_post,
             g_ff2_pre=g_ff2_pre, g_ff2_post=g_ff2_post)
    w_grp = w_pool_grp[0].astype(BF16)
    scale = pool_scale
    d_skip = ssm_d

    bd, cd, lb, tab = _ssm_disc(ssm_a_re[0], ssm_a_im[0], ssm_log_step[0], ssm_b_re[0], ssm_b_im[0],
                                ssm_c_re[0], ssm_c_im[0])

    def sample_mixers(proj):
        z, new_buf = _pool_step(proj, state_pool[0].swapaxes(0, 1), w_grp, scale, TILES['pool_step'])
        g, x_re, x_im = _ssm_step(proj, state_ssm_re[0].reshape(bs, SSM_LANES),
                                  state_ssm_im[0].reshape(bs, SSM_LANES), lb, bd, cd, d_skip)
        q4 = proj[:, POOL_WIDTH + SSM_WIDTH:OFF_GATE].reshape(bs, 1, XA_HEADS, XA_HEAD_DIM)
        o = _xa_step(q4, cache_mem_k[0], cache_mem_v[0], TILES['xa_step'])
        return z, g, o.reshape(bs, XA_WIDTH).astype(BF16), (new_buf, x_re, x_im)

    y_sample, (pool_s, re_s, im_s), wb = _layer(x_sample.reshape(bs * ls, d), w, sample_mixers)

    mem = mem_prompt.reshape(bp * N_MEM, d)
    mem_k, mem_k_heads, mem_v, mem_v_heads = _memproj(mem, g_mem, w_mem_k[0], w_mem_v[0])

    def prompt_mixers(proj):
        z = _pool_seq(proj, w_grp, scale, bp, lp, TILES['pool_seq'])
        zeros = jnp.zeros((bp, 1, SSM_LANES), F32)
        g, last_re, last_im = _ssm_seq(proj, zeros, zeros, tab, bd, cd, d_skip, bp, lp)
        o = _xa_seq(proj, mem_k, mem_v, bp, lp, TILES['xa_seq'])
        pool_rows = proj.reshape(bp, lp, IN_WIDTH)[:, lp - POOL_BUF:, :POOL_WIDTH]
        return z, g, o, (pool_rows, last_re, last_im)

    y_prompt, (pool_p, re_p, im_p), _ = _layer(x_prompt.reshape(bp * lp, d), wb, prompt_mixers)

    kv_shape = (1, bp, N_MEM, XA_HEADS, XA_HEAD_DIM)
    st_shape = (SSM_GROUPS, SSM_STATE)
    return (y_prompt.reshape(bp, lp, d), y_sample.reshape(bs, ls, d),
            mem_k_heads.reshape(kv_shape), mem_v_heads.reshape(kv_shape),
            pool_p[None],
            re_p.reshape((1, bp) + st_shape), im_p.reshape((1, bp) + st_shape),
            pool_s.swapaxes(0, 1)[None],
            re_s.reshape((1, bs) + st_shape), im_s.reshape((1, bs) + st_shape))
```

```python
import functools
import math

import jax
import jax.numpy as jnp
from jax import lax
from jax.experimental import pallas as pl
from jax.experimental.pallas import tpu as pltpu

F32 = jnp.float32
BF16 = jnp.bfloat16

D_MODEL = 2048
PAST_LEN = 16384
POOL_WIDTH = D_MODEL // 2
POOL_WINDOWS = (2, 4, 8, 16)
POOL_GROUP_DIM = POOL_WIDTH // len(POOL_WINDOWS)
POOL_BUF = max(POOL_WINDOWS) - 1
SSM_WIDTH = D_MODEL // 2
SSM_GROUP_CH = 16
SSM_GROUPS = SSM_WIDTH // SSM_GROUP_CH
SSM_STATE = 64
SSM_LANES = SSM_GROUPS * SSM_STATE
XA_HEADS = 4
XA_HEAD_DIM = D_MODEL // 8
XA_WIDTH = XA_HEADS * XA_HEAD_DIM
XA_SCALE = XA_HEAD_DIM ** -0.5
N_MEM = 256
OFF_GATE = POOL_WIDTH + SSM_WIDTH + XA_WIDTH
IN_WIDTH = OFF_GATE + 3 * D_MODEL
RMS_EPS = 1e-6

SUBLANES = 8
LANES = 128
MXU_DIM = 256
VMEM_LIMIT_BYTES = 60 * 1024 * 1024
DOWN_CHUNKS = 4
NORM_ROWS = 2 * SUBLANES
W_SLOTS = 4
W_AHEAD = 2

SSM_CHUNKS = SSM_WIDTH // MXU_DIM
SSM_CHUNK_LANES = SSM_LANES // SSM_CHUNKS
SSM_TIME_TILE = 256
SSM_SEG = SSM_TIME_TILE // SUBLANES
SCAN_LANES = 4 * LANES
SCAN_TILES = 10


def _params(*sem):
    return pltpu.CompilerParams(dimension_semantics=sem, vmem_limit_bytes=VMEM_LIMIT_BYTES)


def _rms(x, g, half=False):
    scale = lax.rsqrt(jnp.mean(x * x, axis=-1, keepdims=True) + RMS_EPS)
    return (x * (0.5 * scale if half else scale)) * g


def _dot(a, b):
    return jnp.dot(a, b, preferred_element_type=F32)


def _cmul(a_re, a_im, b_re, b_im):
    return a_re * b_re - a_im * b_im, a_re * b_im + a_im * b_re


def _weight(w_ref, copy_ref=None):
    w = w_ref[...]
    if w.dtype != BF16:
        w = w.astype(BF16)
    if copy_ref is not None:
        copy_ref[...] = w
    return w


def _with_copies(specs, weights, emit_w):
    if not emit_w:
        return [], []
    return list(specs), [jax.ShapeDtypeStruct(w.shape, BF16) for w in weights]


def _col_tiled(w, tn):
    tiled = lambda k: pl.BlockSpec((None, k, tn), lambda i, j: (j, 0, 0))
    if w.dtype == BF16:
        assert w.ndim == 3 and w.shape[2] == tn, "bf16 weights come tile-major from an earlier call"
        return tiled(w.shape[1]), None, None
    k, n = w.shape
    return pl.BlockSpec((k, tn), lambda i, j: (0, j)), tiled(k), jax.ShapeDtypeStruct((n // tn, k, tn), BF16)


def _swiglu_accumulate(xn, wg, wu, wd_cols, o_ref):
    g = _dot(xn, wg)
    u = _dot(xn, wu)
    hid = ((g * jax.nn.sigmoid(g)) * u).astype(BF16)
    cols = o_ref.shape[1] // DOWN_CHUNKS
    for n in range(DOWN_CHUNKS):
        sl = slice(n * cols, (n + 1) * cols)
        o_ref[:, sl] += _dot(hid, wd_cols(sl))


def _ffn_stream_kernel(x_ref, gpre_ref, wg_hbm, wu_hbm, wd_hbm, gpost_ref, gnext_ref, o_ref, xnext_ref,
                       xn_ref, wg_buf, wu_buf, wd_buf, sem):
    i = pl.program_id(0)
    n_tiles, _, tf = wg_hbm.shape

    def copies(k):
        s = k % W_SLOTS
        return (pltpu.make_async_copy(wg_hbm.at[k], wg_buf.at[s], sem.at[0, s]),
                pltpu.make_async_copy(wu_hbm.at[k], wu_buf.at[s], sem.at[1, s]),
                pltpu.make_async_copy(wd_hbm.at[pl.ds(k * tf, tf), :], wd_buf.at[s], sem.at[2, s]))

    def start(k):
        for c in copies(k):
            c.start()

    @pl.when(i == 0)
    def _():
        for k in range(W_AHEAD):
            start(k)

    xn_ref[...] = _rms(x_ref[...], gpre_ref[...]).astype(BF16)
    o_ref[...] = jnp.zeros_like(o_ref)
    xn = xn_ref[...]
    for k in range(n_tiles):
        s = k % W_SLOTS
        if k + W_AHEAD < n_tiles:
            start(k + W_AHEAD)
        else:
            @pl.when(i + 1 < pl.num_programs(0))
            def _(k=k):
                start(k + W_AHEAD - n_tiles)
        for c in copies(k):
            c.wait()
        _swiglu_accumulate(xn, wg_buf[s], wu_buf[s], lambda cols, s=s: wd_buf[s, :, cols], o_ref)

    out = x_ref[...] + _rms(o_ref[...], gpost_ref[...], half=True)
    o_ref[...] = out
    xnext_ref[...] = _rms(out, gnext_ref[...]).astype(BF16)


def _ffn_stream(x, g_pre, weights, g_post, g_next, tm):
    m, d = x.shape
    w_gate, w_up, w_down = weights
    n_tiles, _, tf = w_gate.shape
    row_spec = pl.BlockSpec((tm, d), lambda i: (i, 0))
    gain_spec = pl.BlockSpec((1, d), lambda i: (0, 0))
    hbm_spec = pl.BlockSpec(memory_space=pl.ANY)
    w_slots = W_SLOTS
    for k in range(n_tiles - W_AHEAD, n_tiles):
        pending = set(range(k, n_tiles)) | set(range(k + W_AHEAD - n_tiles))
        assert (k + W_AHEAD - n_tiles) % W_SLOTS not in {j % W_SLOTS for j in pending}
    return pl.pallas_call(
        _ffn_stream_kernel,
        grid=(m // tm,),
        in_specs=[row_spec, gain_spec, hbm_spec, hbm_spec, hbm_spec, gain_spec, gain_spec],
        out_specs=[row_spec, row_spec],
        out_shape=[jax.ShapeDtypeStruct((m, d), F32), jax.ShapeDtypeStruct((m, d), BF16)],
        scratch_shapes=[pltpu.VMEM((tm, d), BF16),
                        pltpu.VMEM((w_slots, d, tf), BF16), pltpu.VMEM((w_slots, d, tf), BF16),
                        pltpu.VMEM((w_slots, tf, d), BF16), pltpu.SemaphoreType.DMA((3, w_slots))],
        compiler_params=_params("arbitrary"),
        name="ffn_stream",
    )(x, g_pre, w_gate, w_up, w_down, g_post, g_next)


def _ffn_kernel(*refs, first, emit_w):
    refs = list(refs)
    x_ref, pre_ref, wg_ref, wu_ref, wd_ref, gpost_ref = refs[:6]
    del refs[:6]
    gnext_ref = refs.pop(0) if first else None
    o_ref = refs.pop(0)
    xnext_ref = refs.pop(0) if first else None
    wgb_ref, wub_ref, wdb_ref = (refs.pop(0), refs.pop(0), refs.pop(0)) if emit_w else (None, None, None)
    if first:
        xn_ref, = refs
    else:
        xn_ref = pre_ref
        xbuf_ref, xsem = refs
    i = pl.program_id(0)
    j = pl.program_id(1)
    tm = o_ref.shape[0]

    def x_copy():
        return pltpu.make_async_copy(x_ref.at[pl.ds(i * tm, tm), :], xbuf_ref, xsem)

    @pl.when(j == 0)
    def _():
        if first:
            xn_ref[...] = _rms(x_ref[...], pre_ref[...]).astype(BF16)
        else:
            x_copy().start()
        o_ref[...] = jnp.zeros_like(o_ref)

    wd = _weight(wd_ref, wdb_ref)
    _swiglu_accumulate(xn_ref[...], _weight(wg_ref, wgb_ref), _weight(wu_ref, wub_ref),
                       lambda cols: wd[:, cols], o_ref)

    @pl.when(j == pl.num_programs(1) - 1)
    def _():
        if first:
            out = x_ref[...] + _rms(o_ref[...], gpost_ref[...], half=True)
            o_ref[...] = out
            xnext_ref[...] = _rms(out, gnext_ref[...]).astype(BF16)
        else:
            x_copy().wait()
            for r in range(tm // NORM_ROWS):
                rs = slice(r * NORM_ROWS, (r + 1) * NORM_ROWS)
                o_ref[rs, :] = xbuf_ref[rs, :] + _rms(o_ref[rs, :], gpost_ref[...], half=True)


def _ffn(x, pre, weights, g_post, tm, tf, g_next=None):
    m, d = x.shape
    w_gate, w_up, w_down = weights
    f = w_down.shape[0]
    first = g_next is not None
    emit_w = w_gate.dtype != BF16
    assert not emit_w or m == tm, "weight copies need every weight block visited exactly once"
    if first and not emit_w:
        return tuple(_ffn_stream(x, pre, weights, g_post, g_next, tm)), tuple(weights)
    row_spec = pl.BlockSpec((tm, d), lambda i, j: (i, 0))
    gain_spec = pl.BlockSpec((1, d), lambda i, j: (0, 0))
    wg_spec, wg_copy, wg_shape = _col_tiled(w_gate, tf)
    wu_spec, wu_copy, wu_shape = _col_tiled(w_up, tf)
    wd_spec = pl.BlockSpec((tf, d), lambda i, j: (j, 0))
    w_specs = [wg_spec, wu_spec, wd_spec]
    copy_specs = [wg_copy, wu_copy, wd_spec] if emit_w else []
    copy_shapes = [wg_shape, wu_shape, jax.ShapeDtypeStruct(w_down.shape, BF16)] if emit_w else []
    outs = pl.pallas_call(
        functools.partial(_ffn_kernel, first=first, emit_w=emit_w),
        grid=(m // tm, f // tf),
        in_specs=([row_spec, gain_spec] if first else [pl.BlockSpec(memory_space=pl.ANY), row_spec])
        + w_specs + [gain_spec] + ([gain_spec] if first else []),
        out_specs=[row_spec] + ([row_spec] if first else []) + copy_specs,
        out_shape=[jax.ShapeDtypeStruct((m, d), F32)] + ([jax.ShapeDtypeStruct((m, d), BF16)] if first else [])
        + copy_shapes,
        scratch_shapes=[pltpu.VMEM((tm, d), BF16)] if first
        else [pltpu.VMEM((tm, d), F32), pltpu.SemaphoreType.DMA(())],
        compiler_params=_params("parallel", "arbitrary"),
        name="ffn",
    )(*((x, pre, w_gate, w_up, w_down, g_post) + ((g_next,) if first else ())))
    n_out = 2 if first else 1
    return outs[:n_out], (tuple(outs[n_out:]) if emit_w else tuple(weights))


def _proj_kernel(xn_ref, w_ref, o_ref, *copy_ref):
    o_ref[...] = _dot(xn_ref[...], _weight(w_ref, *copy_ref))


def _proj(xn, w, tm, tn):
    m, d = xn.shape
    emit_w = w.dtype != BF16
    assert not emit_w or m == tm
    n = w.shape[1] if emit_w else w.shape[0] * tn
    w_spec, copy_spec, copy_shape = _col_tiled(w, tn)
    copy_specs, copy_shapes = ([copy_spec], [copy_shape]) if emit_w else ([], [])
    outs = pl.pallas_call(
        _proj_kernel,
        grid=(m // tm, n // tn),
        in_specs=[pl.BlockSpec((tm, d), lambda i, j: (i, 0)), w_spec],
        out_specs=[pl.BlockSpec((tm, tn), lambda i, j: (i, j))] + copy_specs,
        out_shape=[jax.ShapeDtypeStruct((m, n), F32)] + copy_shapes,
        compiler_params=_params("parallel", "arbitrary"),
        name="proj",
    )(xn, w)
    return outs[0], (outs[1] if emit_w else w)


def _memproj_kernel(x_ref, g_ref, wk_ref, wv_ref, k_ref, kh_ref, v_ref, vh_ref):
    xn = _rms(x_ref[...], g_ref[...]).astype(BF16)
    for w_ref, o_ref, oh_ref in ((wk_ref, k_ref, kh_ref), (wv_ref, v_ref, vh_ref)):
        res = _dot(xn, _weight(w_ref))
        o_ref[...] = res
        for h in range(XA_HEADS):
            for p in range(XA_LANE_SPLIT):
                lane0 = h * XA_HEAD_DIM + p * LANES
                oh_ref[:, p * XA_HEADS + h, :] = res[:, lane0:lane0 + LANES]


def _memproj(x, g, w_k, w_v):
    m, d = x.shape
    n = w_k.shape[1]
    assert n == XA_WIDTH
    tm = _row_tile(m, TILES['tm_mem'])
    w_spec = pl.BlockSpec((d, n), lambda i: (0, 0), pipeline_mode=pl.Buffered(1))
    flat_spec = pl.BlockSpec((tm, n), lambda i: (i, 0))
    head_spec = pl.BlockSpec((tm, SUBLANES, LANES), lambda i: (i, 0, 0))
    flat_shape = jax.ShapeDtypeStruct((m, n), F32)
    head_shape = jax.ShapeDtypeStruct((m, SUBLANES, LANES), F32)
    k, kh, v, vh = pl.pallas_call(
        _memproj_kernel,
        grid=(m // tm,),
        in_specs=[pl.BlockSpec((tm, d), lambda i: (i, 0)), pl.BlockSpec((1, d), lambda i: (0, 0)), w_spec, w_spec],
        out_specs=[flat_spec, head_spec, flat_spec, head_spec],
        out_shape=[flat_shape, head_shape, flat_shape, head_shape],
        compiler_params=_params("parallel"),
        name="memproj",
    )(x, g, w_k, w_v)
    return k, _heads_from_sublanes(kh), v, _heads_from_sublanes(vh)


def _pool_group_out(k, diff, wgrp_ref, scale_ref, z_ref):
    ch = slice(k * POOL_GROUP_DIM, (k + 1) * POOL_GROUP_DIM)
    zk = _dot(diff.astype(BF16), wgrp_ref[k]) * scale_ref[:, ch]
    z_ref[:, ch] = zk.astype(BF16)


def _pool_seq_kernel(u_ref, wgrp_ref, scale_ref, z_ref, ext_ref, *, halo):
    t = pl.program_id(1)
    tt = u_ref.shape[0]

    @pl.when(t == 0)
    def _():
        ext_ref[0:halo, :] = jnp.zeros((halo, POOL_WIDTH), F32)

    u = u_ref[...]
    ext_ref[halo:halo + tt, :] = u
    pos = t * tt + lax.broadcasted_iota(jnp.int32, (tt, 1), 0)
    for k, w in enumerate(POOL_WINDOWS):
        ch = slice(k * POOL_GROUP_DIM, (k + 1) * POOL_GROUP_DIM)
        a = ext_ref[:, ch]
        d = 1
        while d < w:
            a = a + pltpu.roll(a, d, axis=0)
            d *= 2
        count = jnp.minimum(pos + 1, w).astype(F32)
        diff = a[halo:, :] / count - u[:, ch]
        _pool_group_out(k, diff, wgrp_ref, scale_ref, z_ref)
    ext_ref[0:halo, :] = ext_ref[tt:tt + halo, :]


def _pool_seq(proj, w_grp, scale, batch, seq, tt):
    halo = 2 * SUBLANES
    nt = seq // tt
    return pl.pallas_call(
        functools.partial(_pool_seq_kernel, halo=halo),
        grid=(batch, nt),
        in_specs=[
            pl.BlockSpec((tt, POOL_WIDTH), lambda b, t: (b * nt + t, 0)),
            pl.BlockSpec(w_grp.shape, lambda b, t: (0, 0, 0)),
            pl.BlockSpec((1, POOL_WIDTH), lambda b, t: (0, 0)),
        ],
        out_specs=pl.BlockSpec((tt, POOL_WIDTH), lambda b, t: (b * nt + t, 0)),
        out_shape=jax.ShapeDtypeStruct((batch * seq, POOL_WIDTH), BF16),
        scratch_shapes=[pltpu.VMEM((tt + halo, POOL_WIDTH), F32)],
        compiler_params=_params("parallel", "arbitrary"),
        name="pool_seq",
    )(proj, w_grp, scale)


def _pool_step_kernel(u_ref, buf_ref, wgrp_ref, scale_ref, z_ref, nbuf_ref):
    u = u_ref[...]
    for k, w in enumerate(POOL_WINDOWS):
        ch = slice(k * POOL_GROUP_DIM, (k + 1) * POOL_GROUP_DIM)
        s = u[:, ch]
        for j in range(1, w):
            s = s + buf_ref[POOL_BUF - j, :, ch]
        diff = s / float(min(PAST_LEN + 1, w)) - u[:, ch]
        _pool_group_out(k, diff, wgrp_ref, scale_ref, z_ref)
    nbuf_ref[0:POOL_BUF - 1] = buf_ref[1:POOL_BUF]
    nbuf_ref[POOL_BUF - 1] = u


def _pool_step(proj, buf, w_grp, scale, tb):
    m = proj.shape[0]
    buf_spec = pl.BlockSpec((POOL_BUF, tb, POOL_WIDTH), lambda i: (0, i, 0))
    return pl.pallas_call(
        _pool_step_kernel,
        grid=(m // tb,),
        in_specs=[
            pl.BlockSpec((tb, POOL_WIDTH), lambda i: (i, 0)),
            buf_spec,
            pl.BlockSpec(w_grp.shape, lambda i: (0, 0, 0)),
            pl.BlockSpec((1, POOL_WIDTH), lambda i: (0, 0)),
        ],
        out_specs=[pl.BlockSpec((tb, POOL_WIDTH), lambda i: (i, 0)), buf_spec],
        out_shape=[jax.ShapeDtypeStruct((m, POOL_WIDTH), BF16), jax.ShapeDtypeStruct(buf.shape, F32)],
        compiler_params=_params("parallel"),
        name="pool_step",
    )(proj, buf, w_grp, scale)


def _zoh_lambda(a_re, a_im, log_step):
    dt = jnp.exp(log_step)
    mag = jnp.exp(a_re * dt)
    ang = a_im * dt
    return mag * jnp.cos(ang), mag * jnp.sin(ang)


def _tile_diag(blocks, rows_per_group):
    groups = SSM_GROUPS // SSM_CHUNKS
    rows, c = blocks.shape
    wide = groups * c
    spread = (lax.broadcasted_iota(jnp.int32, (c, wide), 1) % c
              == lax.broadcasted_iota(jnp.int32, (c, wide), 0)).astype(BF16)
    tiled = _dot(blocks.astype(BF16), spread)
    keep = (lax.broadcasted_iota(jnp.int32, (rows, wide), 0) // rows_per_group
            == lax.broadcasted_iota(jnp.int32, (rows, wide), 1) // c)
    return jnp.where(keep, tiled, 0.0).astype(BF16)


def _ssm_disc_kernel(a_re_ref, a_im_ref, ls_ref, a_re3_ref, a_im3_ref, ls3_ref, bt_ref, ct_ref,
                     bd_ref, cd_ref, lb_ref, tab_ref):
    a_re = a_re3_ref[...]
    a_im = a_im3_ref[...]
    lb_re, lb_im = _zoh_lambda(a_re, a_im, ls3_ref[...])
    den = a_re * a_re + a_im * a_im
    n_re = lb_re - 1.0
    f_re = (n_re * a_re + lb_im * a_im) / den
    f_im = (lb_im * a_re - n_re * a_im) / den
    b_re = bt_ref[0]
    b_im = bt_ref[1]
    bb = (f_re * b_re - f_im * b_im, f_re * b_im + f_im * b_re)
    groups = SSM_GROUPS // SSM_CHUNKS
    for part in range(2):
        for c in range(SSM_CHUNKS):
            grp = slice(c * groups, (c + 1) * groups)
            bd_ref[part, c] = _tile_diag(
                bb[part][grp].reshape(groups * SSM_GROUP_CH, SSM_STATE), SSM_GROUP_CH)
            cd_ref[part, c] = _tile_diag(
                ct_ref[part, grp].reshape(groups * SSM_STATE, SSM_GROUP_CH), SSM_STATE)

    lb_re, lb_im = _zoh_lambda(a_re_ref[...], a_im_ref[...], ls_ref[...])
    lb_ref[0] = lb_re
    lb_ref[1] = lb_im
    s_re, s_im = lb_re, lb_im
    for _ in range(int(math.log2(SSM_SEG))):
        s_re, s_im = _cmul(s_re, s_im, s_re, s_im)
    powers = [(s_re, s_im)]
    for _ in range(1, SUBLANES):
        powers.append(_cmul(*powers[-1], s_re, s_im))
    zero = jnp.zeros_like(lb_re)
    for r in range(SUBLANES):
        tab_ref[0, r] = lb_re
        tab_ref[1, r] = lb_im
        for i in range(3):
            d = 1 << i
            tab_ref[2 + 2 * i, r] = powers[d - 1][0] if r >= d else zero
            tab_ref[3 + 2 * i, r] = powers[d - 1][1] if r >= d else zero
        tab_ref[8, r] = powers[r][0]
        tab_ref[9, r] = powers[r][1]


def _ssm_disc(a_re, a_im, log_step, b_re, b_im, c_re, c_im):
    g, n, h = b_re.shape
    bt = jnp.stack([b_re, b_im]).swapaxes(2, 3)
    ct = jnp.stack([c_re, c_im]).swapaxes(2, 3)
    per_chunk = g // SSM_CHUNKS
    bd, cd, lb, tab = pl.pallas_call(
        _ssm_disc_kernel,
        out_shape=[jax.ShapeDtypeStruct((2, SSM_CHUNKS, per_chunk * h, per_chunk * n), BF16),
                   jax.ShapeDtypeStruct((2, SSM_CHUNKS, per_chunk * n, per_chunk * h), BF16),
                   jax.ShapeDtypeStruct((2, g, n), F32),
                   jax.ShapeDtypeStruct((SCAN_TILES, SUBLANES, g, n), F32)],
        name="ssm_disc",
    )(a_re, a_im, log_step[:, None], a_re[:, None, :], a_im[:, None, :], log_step[:, None, None], bt, ct)
    return bd, cd, lb.reshape(2, g * n), tab.reshape(SCAN_TILES, SUBLANES, g * n)


def _ssm_in(ub, bd_ref, x_ref):
    for c in range(SSM_CHUNKS):
        x_ref[:, c * SSM_CHUNK_LANES:(c + 1) * SSM_CHUNK_LANES] = _dot(
            ub[:, c * MXU_DIM:(c + 1) * MXU_DIM], bd_ref[c])


def _ssm_out(c, u, x_re_ref, x_im_ref, cdr_ref, cdi_ref, d_ref):
    ch = slice(c * MXU_DIM, (c + 1) * MXU_DIM)
    st = slice(c * SSM_CHUNK_LANES, (c + 1) * SSM_CHUNK_LANES)
    y = _dot(x_re_ref[:, st].astype(BF16), cdr_ref[c]) - _dot(x_im_ref[:, st].astype(BF16), cdi_ref[c])
    return jax.nn.gelu(y + d_ref[:, ch] * u[:, ch])


def _ssm_seq_kernel(u_ref, h_re_ref, h_im_ref, tab_ref, bdr_ref, bdi_ref, cdr_ref, cdi_ref, d_ref,
                    g_ref, last_re_ref, last_im_ref, xr_ref, xi_ref, cr_ref, ci_ref):
    t = pl.program_id(1)
    tt = u_ref.shape[0]
    seg = tt // SUBLANES

    @pl.when(t == 0)
    def _():
        cr_ref[...] = jnp.broadcast_to(h_re_ref[...], cr_ref.shape)
        ci_ref[...] = jnp.broadcast_to(h_im_ref[...], ci_ref.shape)

    def regroup(x, rows_major, rows_minor):
        w = x.shape[1]
        return jnp.swapaxes(x.reshape(rows_major, rows_minor, w), 0, 1).reshape(tt, w)

    u = regroup(u_ref[...], SUBLANES, seg)
    ub = u.astype(BF16)
    _ssm_in(ub, bdr_ref, xr_ref)
    _ssm_in(ub, bdi_ref, xi_ref)

    first_row = lax.broadcasted_iota(jnp.int32, (SUBLANES, SCAN_LANES), 0) == 0
    for lc in range(SSM_LANES // SCAN_LANES):
        sl = slice(lc * SCAN_LANES, (lc + 1) * SCAN_LANES)
        l_re = tab_ref[0, :, sl]
        l_im = tab_ref[1, :, sl]

        def rows(j):
            return pl.ds(j * SUBLANES, SUBLANES)

        def advance(j, x, sl=sl, l_re=l_re, l_im=l_im):
            x_re, x_im = _cmul(l_re, l_im, *x)
            x_re = x_re + xr_ref[rows(j), sl]
            x_im = x_im + xi_ref[rows(j), sl]
            xr_ref[rows(j), sl] = x_re
            xi_ref[rows(j), sl] = x_im
            return x_re, x_im

        zero = jnp.zeros((SUBLANES, SCAN_LANES), F32)
        f_re, f_im = zero, zero
        for j in range(seg):
            f_re, f_im = advance(j, (f_re, f_im))

        c_re = cr_ref[:, sl]
        c_im = ci_ref[:, sl]
        for i in range(3):
            s_re = pltpu.roll(f_re, 1 << i, axis=0)
            s_im = pltpu.roll(f_im, 1 << i, axis=0)
            m_re, m_im = _cmul(tab_ref[2 + 2 * i, :, sl], tab_ref[3 + 2 * i, :, sl], s_re, s_im)
            f_re, f_im = f_re + m_re, f_im + m_im
        m_re, m_im = _cmul(tab_ref[8, :, sl], tab_ref[9, :, sl], c_re, c_im)
        f_re, f_im = f_re + m_re, f_im + m_im
        e_re = jnp.where(first_row, c_re, pltpu.roll(f_re, 1, axis=0))
        e_im = jnp.where(first_row, c_im, pltpu.roll(f_im, 1, axis=0))
        cr_ref[:, sl] = jnp.broadcast_to(f_re[SUBLANES - 1:SUBLANES, :], (SUBLANES, SCAN_LANES))
        ci_ref[:, sl] = jnp.broadcast_to(f_im[SUBLANES - 1:SUBLANES, :], (SUBLANES, SCAN_LANES))

        def fix(j, w, sl=sl, l_re=l_re, l_im=l_im):
            w_re, w_im = _cmul(l_re, l_im, *w)
            xr_ref[rows(j), sl] += w_re
            xi_ref[rows(j), sl] += w_im
            return w_re, w_im

        w = (e_re, e_im)
        for j in range(seg):
            w = fix(j, w)

    for c in range(SSM_CHUNKS):
        ch = slice(c * MXU_DIM, (c + 1) * MXU_DIM)
        g = _ssm_out(c, u, xr_ref, xi_ref, cdr_ref, cdi_ref, d_ref)
        g_ref[:, ch] = regroup(g, seg, SUBLANES).astype(BF16)

    @pl.when(t == pl.num_programs(1) - 1)
    def _():
        last_re_ref[...] = cr_ref[0:1, :]
        last_im_ref[...] = ci_ref[0:1, :]


def _const_spec(shape):
    return pl.BlockSpec(shape, lambda *_: (0,) * len(shape))


def _part_spec(stacked, part):
    rest = stacked.shape[1:]
    return pl.BlockSpec((None,) + rest, lambda *_: (part,) + (0,) * len(rest))


def _ssm_seq(proj, h_re, h_im, tab, bd, cd, d_skip, batch, seq):
    tt = SSM_TIME_TILE
    nt = seq // tt
    col_block = POOL_WIDTH // SSM_WIDTH
    state_spec = pl.BlockSpec((None, 1, SSM_LANES), lambda b, t: (b, 0, 0))
    state_shape = jax.ShapeDtypeStruct((batch, 1, SSM_LANES), F32)
    return pl.pallas_call(
        _ssm_seq_kernel,
        grid=(batch, nt),
        in_specs=[
            pl.BlockSpec((tt, SSM_WIDTH), lambda b, t: (b * nt + t, col_block)),
            state_spec, state_spec,
            _const_spec(tab.shape),
            _part_spec(bd, 0), _part_spec(bd, 1), _part_spec(cd, 0), _part_spec(cd, 1),
            _const_spec(d_skip.shape),
        ],
        out_specs=[pl.BlockSpec((tt, SSM_WIDTH), lambda b, t: (b * nt + t, 0)), state_spec, state_spec],
        out_shape=[jax.ShapeDtypeStruct((batch * seq, SSM_WIDTH), BF16), state_shape, state_shape],
        scratch_shapes=[
            pltpu.VMEM((tt, SSM_LANES), F32), pltpu.VMEM((tt, SSM_LANES), F32),
            pltpu.VMEM((SUBLANES, SSM_LANES), F32), pltpu.VMEM((SUBLANES, SSM_LANES), F32),
        ],
        compiler_params=_params("parallel", "arbitrary"),
        name="ssm_seq",
    )(proj, h_re, h_im, tab, bd, bd, cd, cd, d_skip)


def _ssm_step_kernel(u_ref, h_re_ref, h_im_ref, lb_ref, bdr_ref, bdi_ref, cdr_ref, cdi_ref, d_ref,
                     g_ref, x_re_ref, x_im_ref):
    u = u_ref[...]
    ub = u.astype(BF16)
    _ssm_in(ub, bdr_ref, x_re_ref)
    _ssm_in(ub, bdi_ref, x_im_ref)
    m_re, m_im = _cmul(lb_ref[0:1, :], lb_ref[1:2, :], h_re_ref[...], h_im_ref[...])
    x_re_ref[...] += m_re
    x_im_ref[...] += m_im
    for c in range(SSM_CHUNKS):
        ch = slice(c * MXU_DIM, (c + 1) * MXU_DIM)
        g_ref[:, ch] = _ssm_out(c, u, x_re_ref, x_im_ref, cdr_ref, cdi_ref, d_ref).astype(BF16)


def _ssm_step(proj, h_re, h_im, lb, bd, cd, d_skip):
    m = proj.shape[0]
    state_spec = pl.BlockSpec((m, SSM_LANES), lambda i: (0, 0))
    state_shape = jax.ShapeDtypeStruct((m, SSM_LANES), F32)
    return pl.pallas_call(
        _ssm_step_kernel,
        grid=(1,),
        in_specs=[
            pl.BlockSpec((m, SSM_WIDTH), lambda i: (0, POOL_WIDTH // SSM_WIDTH)),
            state_spec, state_spec,
            _const_spec(lb.shape),
            _part_spec(bd, 0), _part_spec(bd, 1), _part_spec(cd, 0), _part_spec(cd, 1),
            _const_spec(d_skip.shape),
        ],
        out_specs=[pl.BlockSpec((m, SSM_WIDTH), lambda i: (0, 0)), state_spec, state_spec],
        out_shape=[jax.ShapeDtypeStruct((m, SSM_WIDTH), BF16), state_shape, state_shape],
        compiler_params=_params("arbitrary"),
        name="ssm_step",
    )(proj, h_re, h_im, lb, bd, bd, cd, cd, d_skip)


def _xa_seq_kernel(q_ref, k_ref, v_ref, o_ref):
    for h in range(XA_HEADS):
        ch = slice(h * XA_HEAD_DIM, (h + 1) * XA_HEAD_DIM)
        q = q_ref[:, ch].astype(BF16)
        k = k_ref[:, ch].astype(BF16)
        v = v_ref[:, ch].astype(BF16)
        s = lax.dot_general(q, k, (((1,), (1,)), ((), ())), preferred_element_type=F32) * XA_SCALE
        e = jnp.exp(s - jnp.max(s, axis=-1, keepdims=True))
        p = e / jnp.sum(e, axis=-1, keepdims=True)
        o_ref[:, ch] = _dot(p.astype(BF16), v).astype(BF16)


def _xa_seq(proj, mem_k, mem_v, batch, seq, tq):
    nt = seq // tq
    q_block = (POOL_WIDTH + SSM_WIDTH) // XA_WIDTH
    kv_spec = pl.BlockSpec((N_MEM, XA_WIDTH), lambda b, t: (b, 0))
    return pl.pallas_call(
        _xa_seq_kernel,
        grid=(batch, nt),
        in_specs=[pl.BlockSpec((tq, XA_WIDTH), lambda b, t: (b * nt + t, q_block)), kv_spec, kv_spec],
        out_specs=pl.BlockSpec((tq, XA_WIDTH), lambda b, t: (b * nt + t, 0)),
        out_shape=jax.ShapeDtypeStruct((batch * seq, XA_WIDTH), BF16),
        compiler_params=_params("parallel", "parallel"),
        name="xa_seq",
    )(proj, mem_k, mem_v)


XA_LANE_SPLIT = XA_HEAD_DIM // LANES


def _heads_on_sublanes(x):
    lead = x.shape[:-2]
    x = x.reshape(lead + (XA_HEADS, XA_LANE_SPLIT, LANES)).swapaxes(-3, -2)
    return x.reshape(lead + (XA_LANE_SPLIT * XA_HEADS, LANES))


def _heads_from_sublanes(x):
    lead = x.shape[:-2]
    x = x.reshape(lead + (XA_LANE_SPLIT, XA_HEADS, LANES)).swapaxes(-3, -2)
    return x.reshape(lead + (XA_HEADS, XA_HEAD_DIM))


def _xa_step_kernel(q_ref, k_ref, v_ref, o_ref):
    q = q_ref[...]
    part = jnp.sum(k_ref[...] * q, axis=-1, keepdims=True)
    s = (part + pltpu.roll(part, XA_HEADS, axis=2)) * XA_SCALE
    e = jnp.exp(s - jnp.max(s, axis=1, keepdims=True))
    r = 1.0 / jnp.sum(e, axis=1, keepdims=True)
    o_ref[...] = jnp.sum(e * v_ref[...], axis=1, keepdims=True) * r


def _xa_step(q4, cache_k, cache_v, tb):
    assert XA_LANE_SPLIT * XA_HEADS == SUBLANES
    m = q4.shape[0]
    kv_spec = pl.BlockSpec((tb, N_MEM, SUBLANES, LANES), lambda i: (i, 0, 0, 0))
    q_spec = pl.BlockSpec((tb, 1, SUBLANES, LANES), lambda i: (i, 0, 0, 0))
    o = pl.pallas_call(
        _xa_step_kernel,
        grid=(m // tb,),
        in_specs=[q_spec, kv_spec, kv_spec],
        out_specs=q_spec,
        out_shape=jax.ShapeDtypeStruct((m, 1, SUBLANES, LANES), F32),
        compiler_params=_params("parallel"),
        name="xa_step",
    )(_heads_on_sublanes(q4), _heads_on_sublanes(cache_k), _heads_on_sublanes(cache_v))
    return _heads_from_sublanes(o)


def _gate_kernel(z_ref, g_ref, o_ref, ga_ref, gb_ref, gc_ref, wpo_ref, wv_ref, wg_ref, wxo_ref, m_ref, *copy_refs):
    copies = copy_refs if copy_refs else (None,) * 4
    g = g_ref[...]
    o_pool = _dot(z_ref[...], _weight(wpo_ref, copies[0]))
    o_ssm = _dot(g, _weight(wv_ref, copies[1])) * jax.nn.sigmoid(_dot(g, _weight(wg_ref, copies[2])))
    o_xa = _dot(o_ref[...], _weight(wxo_ref, copies[3]))
    merged = (jax.nn.sigmoid(ga_ref[...]) * o_pool + jax.nn.sigmoid(gb_ref[...]) * o_ssm
              + jax.nn.sigmoid(gc_ref[...]) * o_xa)
    m_ref[...] = merged.astype(BF16)


def _gate(z, g, o, proj, weights, tm, tn):
    m, half = z.shape
    emit_w = weights[0].dtype != BF16
    assert not emit_w or m == tm
    d = weights[0].shape[1] if emit_w else weights[0].shape[0] * tn
    gate0 = OFF_GATE // tn
    per_gate = d // tn
    branch_spec = pl.BlockSpec((tm, half), lambda i, n: (i, 0))
    w_specs, copy_specs, copy_shapes = map(list, zip(*[_col_tiled(w, tn) for w in weights]))
    if not emit_w:
        copy_specs, copy_shapes = [], []

    def gate_spec(k):
        return pl.BlockSpec((tm, tn), lambda i, n: (i, gate0 + k * per_gate + n))

    outs = pl.pallas_call(
        _gate_kernel,
        grid=(m // tm, d // tn),
        in_specs=[branch_spec] * 3 + [gate_spec(0), gate_spec(1), gate_spec(2)] + w_specs,
        out_specs=[pl.BlockSpec((tm, tn), lambda i, n: (i, n))] + copy_specs,
        out_shape=[jax.ShapeDtypeStruct((m, d), BF16)] + copy_shapes,
        compiler_params=_params("parallel", "arbitrary"),
        name="gate",
    )(z, g, o, proj, proj, proj, *weights)
    return outs[0], (tuple(outs[1:]) if emit_w else tuple(weights))


def _mixout_kernel(m_ref, wout_ref, h_ref, gpost_ref, gnext_ref, h2_ref, xn_ref, *copy_ref, sub):
    wout = _weight(wout_ref, *copy_ref)
    for r in range(m_ref.shape[0] // sub):
        rows = slice(r * sub, (r + 1) * sub)
        h2 = h_ref[rows, :] + _rms(_dot(m_ref[rows, :], wout), gpost_ref[...])
        h2_ref[rows, :] = h2
        xn_ref[rows, :] = _rms(h2, gnext_ref[...]).astype(BF16)


def _mixout(merged, w_out, h, g_post, g_next, tm):
    m, d = h.shape
    emit_w = w_out.dtype != BF16
    assert not emit_w or m == tm
    row_spec = pl.BlockSpec((tm, d), lambda i: (i, 0))
    gain_spec = pl.BlockSpec((1, d), lambda i: (0, 0))
    w_spec = pl.BlockSpec((d, d), lambda i: (0, 0))
    copy_specs, copy_shapes = _with_copies([w_spec], [w_out], emit_w)
    outs = pl.pallas_call(
        functools.partial(_mixout_kernel, sub=min(tm, TILES['mix_sub'])),
        grid=(m // tm,),
        in_specs=[row_spec, w_spec, row_spec, gain_spec, gain_spec],
        out_specs=[row_spec, row_spec] + copy_specs,
        out_shape=[jax.ShapeDtypeStruct((m, d), F32), jax.ShapeDtypeStruct((m, d), BF16)] + copy_shapes,
        compiler_params=_params("parallel"),
        name="mixout",
    )(merged, w_out, h, g_post, g_next)
    return outs[:2], (outs[2] if emit_w else w_out)


def _row_tile(m, target):
    return target if m % target == 0 else m


FFN1_W = ('w_ff1_gate', 'w_ff1_up', 'w_ff1_down')
FFN2_W = ('w_ff2_gate', 'w_ff2_up', 'w_ff2_down')
GATE_W = ('w_pool_out', 'w_glu_val', 'w_glu_gate', 'w_xa_out')
TILES = dict(tm=512, tm_ffn2=1024, tf=512, tm_proj=2048, tn_proj=1024, tm_gate=1024, tn=512, tm_mem=512,
             mix_sub=128, pool_seq=1024, xa_seq=1024, pool_step=32, xa_step=4)


def _layer(x, w, mixers):
    m = x.shape[0]
    tm = _row_tile(m, TILES['tm'])
    wb = dict(w)
    (h, xn), wb_ffn1 = _ffn(x, w['g_ff1_pre'], [w[k] for k in FFN1_W], w['g_ff1_post'], tm, TILES['tf'],
                            g_next=w['g_mix_pre'])
    proj, wb['w_in'] = _proj(xn, w['w_in'], _row_tile(m, TILES['tm_proj']), TILES['tn_proj'])
    z, g, o, states = mixers(proj)
    merged, wb_gate = _gate(z, g, o, proj, [w[k] for k in GATE_W], _row_tile(m, TILES['tm_gate']), TILES['tn'])
    (h, xn), wb['w_out'] = _mixout(merged, w['w_out'], h, w['g_mix_post'], w['g_ff2_pre'], tm)
    (y,), wb_ffn2 = _ffn(h, xn, [w[k] for k in FFN2_W], w['g_ff2_post'], _row_tile(m, TILES['tm_ffn2']),
                         TILES['tf'])
    wb.update(zip(FFN1_W + GATE_W + FFN2_W, wb_ffn1 + wb_gate + wb_ffn2))
    return y, states, wb


def kernel(x_prompt, x_sample, mem_prompt, cache_mem_k, cache_mem_v, state_pool, state_ssm_re, state_ssm_im, g_ff1_pre, w_ff1_gate, w_ff1_up, w_ff1_down, g_ff1_post, g_mix_pre, w_in, w_pool_grp, pool_scale, w_pool_out, ssm_a_re, ssm_a_im, ssm_log_step, ssm_b_re, ssm_b_im, ssm_c_re, ssm_c_im, ssm_d, w_glu_val, w_glu_gate, g_mem, w_mem_k, w_mem_v, w_xa_out, w_out, g_mix_post, g_ff2_pre, w_ff2_gate, w_ff2_up, w_ff2_down, g_ff2_post):
    bp, lp, d = x_prompt.shape
    bs, ls, _ = x_sample.shape
    depth = w_in.shape[0]
    assert depth == 1 and ls == 1 and lp >= POOL_BUF and lp % SSM_TIME_TILE == 0

    mats = dict(w_ff1_gate=w_ff1_gate, w_ff1_up=w_ff1_up, w_ff1_down=w_ff1_down, w_in=w_in,
                w_pool_out=w_pool_out, w_glu_val=w_glu_val, w_glu_gate=w_glu_gate, w_xa_out=w_xa_out,
                w_out=w_out, w_ff2_gate=w_ff2_gate, w_ff2_up=w_ff2_up, w_ff2_down=w_ff2_down)
    w = {k: v[0] for k, v in mats.items()}
    w.update(g_ff1_pre=g_ff1_pre, g_ff1_post=g_ff1_post, g_mix_pre=g_mix_pre, g_mix_post=g_mix_post,
             g_ff2_pre=g_ff2_pre, g_ff2_post=g_ff2_post)
    w_grp = w_pool_grp[0].astype(BF16)
    scale = pool_scale
    d_skip = ssm_d

    bd, cd, lb, tab = _ssm_disc(ssm_a_re[0], ssm_a_im[0], ssm_log_step[0], ssm_b_re[0], ssm_b_im[0],
                                ssm_c_re[0], ssm_c_im[0])

    def sample_mixers(proj):
        z, new_buf = _pool_step(proj, state_pool[0].swapaxes(0, 1), w_grp, scale, TILES['pool_step'])
        g, x_re, x_im = _ssm_step(proj, state_ssm_re[0].reshape(bs, SSM_LANES),
                                  state_ssm_im[0].reshape(bs, SSM_LANES), lb, bd, cd, d_skip)
        q4 = proj[:, POOL_WIDTH + SSM_WIDTH:OFF_GATE].reshape(bs, 1, XA_HEADS, XA_HEAD_DIM)
        o = _xa_step(q4, cache_mem_k[0], cache_mem_v[0], TILES['xa_step'])
        return z, g, o.reshape(bs, XA_WIDTH).astype(BF16), (new_buf, x_re, x_im)

    y_sample, (pool_s, re_s, im_s), wb = _layer(x_sample.reshape(bs * ls, d), w, sample_mixers)

    mem = mem_prompt.reshape(bp * N_MEM, d)
    mem_k, mem_k_heads, mem_v, mem_v_heads = _memproj(mem, g_mem, w_mem_k[0], w_mem_v[0])

    def prompt_mixers(proj):
        z = _pool_seq(proj, w_grp, scale, bp, lp, TILES['pool_seq'])
        zeros = jnp.zeros((bp, 1, SSM_LANES), F32)
        g, last_re, last_im = _ssm_seq(proj, zeros, zeros, tab, bd, cd, d_skip, bp, lp)
        o = _xa_seq(proj, mem_k, mem_v, bp, lp, TILES['xa_seq'])
        pool_rows = proj.reshape(bp, lp, IN_WIDTH)[:, lp - POOL_BUF:, :POOL_WIDTH]
        return z, g, o, (pool_rows, last_re, last_im)

    y_prompt, (pool_p, re_p, im_p), _ = _layer(x_prompt.reshape(bp * lp, d), wb, prompt_mixers)

    kv_shape = (1, bp, N_MEM, XA_HEADS, XA_HEAD_DIM)
    st_shape = (SSM_GROUPS, SSM_STATE)
    return (y_prompt.reshape(bp, lp, d), y_sample.reshape(bs, ls, d),
            mem_k_heads.reshape(kv_shape), mem_v_heads.reshape(kv_shape),
            pool_p[None],
            re_p.reshape((1, bp) + st_shape), im_p.reshape((1, bp) + st_shape),
            pool_s.swapaxes(0, 1)[None],
            re_s.reshape((1, bs) + st_shape), im_s.reshape((1, bs) + st_shape))
```

```python
import functools
import math

import jax
import jax.numpy as jnp
from jax import lax
from jax.experimental import pallas as pl
from jax.experimental.pallas import tpu as pltpu

F32 = jnp.float32
BF16 = jnp.bfloat16

D_MODEL = 2048
PAST_LEN = 16384
POOL_WIDTH = D_MODEL // 2
POOL_WINDOWS = (2, 4, 8, 16)
POOL_GROUP_DIM = POOL_WIDTH // len(POOL_WINDOWS)
POOL_BUF = max(POOL_WINDOWS) - 1
SSM_WIDTH = D_MODEL // 2
SSM_GROUP_CH = 16
SSM_GROUPS = SSM_WIDTH // SSM_GROUP_CH
SSM_STATE = 64
SSM_LANES = SSM_GROUPS * SSM_STATE
XA_HEADS = 4
XA_HEAD_DIM = D_MODEL // 8
XA_WIDTH = XA_HEADS * XA_HEAD_DIM
XA_SCALE = XA_HEAD_DIM ** -0.5
N_MEM = 256
OFF_GATE = POOL_WIDTH + SSM_WIDTH + XA_WIDTH
IN_WIDTH = OFF_GATE + 3 * D_MODEL
RMS_EPS = 1e-6

SUBLANES = 8
LANES = 128
MXU_DIM = 256
VMEM_LIMIT_BYTES = 60 * 1024 * 1024
DOWN_CHUNKS = 4
NORM_ROWS = 2 * SUBLANES
W_SLOTS = 4
W_AHEAD = 2

SSM_CHUNKS = SSM_WIDTH // MXU_DIM
SSM_CHUNK_LANES = SSM_LANES // SSM_CHUNKS
SSM_TIME_TILE = 256
SSM_SEG = SSM_TIME_TILE // SUBLANES
SCAN_LANES = 4 * LANES
SCAN_TILES = 10


def _params(*sem):
    return pltpu.CompilerParams(dimension_semantics=sem, vmem_limit_bytes=VMEM_LIMIT_BYTES)


def _rms(x, g, half=False):
    scale = lax.rsqrt(jnp.mean(x * x, axis=-1, keepdims=True) + RMS_EPS)
    return (x * (0.5 * scale if half else scale)) * g


def _dot(a, b):
    return jnp.dot(a, b, preferred_element_type=F32)


def _cmul(a_re, a_im, b_re, b_im):
    return a_re * b_re - a_im * b_im, a_re * b_im + a_im * b_re


def _weight(w_ref, copy_ref=None):
    w = w_ref[...]
    if w.dtype != BF16:
        w = w.astype(BF16)
    if copy_ref is not None:
        copy_ref[...] = w
    return w


def _with_copies(specs, weights, emit_w):
    if not emit_w:
        return [], []
    return list(specs), [jax.ShapeDtypeStruct(w.shape, BF16) for w in weights]


def _col_tiled(w, tn):
    tiled = lambda k: pl.BlockSpec((None, k, tn), lambda i, j: (j, 0, 0))
    if w.dtype == BF16:
        assert w.ndim == 3 and w.shape[2] == tn, "bf16 weights come tile-major from an earlier call"
        return tiled(w.shape[1]), None, None
    k, n = w.shape
    return pl.BlockSpec((k, tn), lambda i, j: (0, j)), tiled(k), jax.ShapeDtypeStruct((n // tn, k, tn), BF16)


def _swiglu_accumulate(xn, wg, wu, wd_cols, o_ref):
    g = _dot(xn, wg)
    u = _dot(xn, wu)
    hid = ((g * jax.nn.sigmoid(g)) * u).astype(BF16)
    cols = o_ref.shape[1] // DOWN_CHUNKS
    for n in range(DOWN_CHUNKS):
        sl = slice(n * cols, (n + 1) * cols)
        o_ref[:, sl] += _dot(hid, wd_cols(sl))


def _ffn_stream_kernel(x_ref, gpre_ref, wg_hbm, wu_hbm, wd_hbm, gpost_ref, gnext_ref, o_ref, xnext_ref,
                       xn_ref, wg_buf, wu_buf, wd_buf, sem):
    i = pl.program_id(0)
    n_tiles, _, tf = wg_hbm.shape

    def copies(k):
        s = k % W_SLOTS
        return (pltpu.make_async_copy(wg_hbm.at[k], wg_buf.at[s], sem.at[0, s]),
                pltpu.make_async_copy(wu_hbm.at[k], wu_buf.at[s], sem.at[1, s]),
                pltpu.make_async_copy(wd_hbm.at[pl.ds(k * tf, tf), :], wd_buf.at[s], sem.at[2, s]))

    def start(k):
        for c in copies(k):
            c.start()

    @pl.when(i == 0)
    def _():
        for k in range(W_AHEAD):
            start(k)

    xn_ref[...] = _rms(x_ref[...], gpre_ref[...]).astype(BF16)
    o_ref[...] = jnp.zeros_like(o_ref)
    xn = xn_ref[...]
    for k in range(n_tiles):
        s = k % W_SLOTS
        if k + W_AHEAD < n_tiles:
            start(k + W_AHEAD)
        else:
            @pl.when(i + 1 < pl.num_programs(0))
            def _(k=k):
                start(k + W_AHEAD - n_tiles)
        for c in copies(k):
            c.wait()
        _swiglu_accumulate(xn, wg_buf[s], wu_buf[s], lambda cols, s=s: wd_buf[s, :, cols], o_ref)

    out = x_ref[...] + _rms(o_ref[...], gpost_ref[...], half=True)
    o_ref[...] = out
    xnext_ref[...] = _rms(out, gnext_ref[...]).astype(BF16)


def _ffn_stream(x, g_pre, weights, g_post, g_next, tm):
    m, d = x.shape
    w_gate, w_up, w_down = weights
    n_tiles, _, tf = w_gate.shape
    row_spec = pl.BlockSpec((tm, d), lambda i: (i, 0))
    gain_spec = pl.BlockSpec((1, d), lambda i: (0, 0))
    hbm_spec = pl.BlockSpec(memory_space=pl.ANY)
    w_slots = W_SLOTS
    for k in range(n_tiles - W_AHEAD, n_tiles):
        pending = set(range(k, n_tiles)) | set(range(k + W_AHEAD - n_tiles))
        assert (k + W_AHEAD - n_tiles) % W_SLOTS not in {j % W_SLOTS for j in pending}
    return pl.pallas_call(
        _ffn_stream_kernel,
        grid=(m // tm,),
        in_specs=[row_spec, gain_spec, hbm_spec, hbm_spec, hbm_spec, gain_spec, gain_spec],
        out_specs=[row_spec, row_spec],
        out_shape=[jax.ShapeDtypeStruct((m, d), F32), jax.ShapeDtypeStruct((m, d), BF16)],
        scratch_shapes=[pltpu.VMEM((tm, d), BF16),
                        pltpu.VMEM((w_slots, d, tf), BF16), pltpu.VMEM((w_slots, d, tf), BF16),
                        pltpu.VMEM((w_slots, tf, d), BF16), pltpu.SemaphoreType.DMA((3, w_slots))],
        compiler_params=_params("arbitrary"),
        name="ffn_stream",
    )(x, g_pre, w_gate, w_up, w_down, g_post, g_next)


def _ffn_kernel(*refs, first, emit_w):
    refs = list(refs)
    x_ref, pre_ref, wg_ref, wu_ref, wd_ref, gpost_ref = refs[:6]
    del refs[:6]
    gnext_ref = refs.pop(0) if first else None
    o_ref = refs.pop(0)
    xnext_ref = refs.pop(0) if first else None
    wgb_ref, wub_ref, wdb_ref = (refs.pop(0), refs.pop(0), refs.pop(0)) if emit_w else (None, None, None)
    if first:
        xn_ref, = refs
    else:
        xn_ref = pre_ref
        xbuf_ref, xsem = refs
    i = pl.program_id(0)
    j = pl.program_id(1)
    tm = o_ref.shape[0]

    def x_copy():
        return pltpu.make_async_copy(x_ref.at[pl.ds(i * tm, tm), :], xbuf_ref, xsem)

    @pl.when(j == 0)
    def _():
        if first:
            xn_ref[...] = _rms(x_ref[...], pre_ref[...]).astype(BF16)
        else:
            x_copy().start()
        o_ref[...] = jnp.zeros_like(o_ref)

    wd = _weight(wd_ref, wdb_ref)
    _swiglu_accumulate(xn_ref[...], _weight(wg_ref, wgb_ref), _weight(wu_ref, wub_ref),
                       lambda cols: wd[:, cols], o_ref)

    @pl.when(j == pl.num_programs(1) - 1)
    def _():
        if first:
            out = x_ref[...] + _rms(o_ref[...], gpost_ref[...], half=True)
            o_ref[...] = out
            xnext_ref[...] = _rms(out, gnext_ref[...]).astype(BF16)
        else:
            x_copy().wait()
            for r in range(tm // NORM_ROWS):
                rs = slice(r * NORM_ROWS, (r + 1) * NORM_ROWS)
                o_ref[rs, :] = xbuf_ref[rs, :] + _rms(o_ref[rs, :], gpost_ref[...], half=True)


def _ffn(x, pre, weights, g_post, tm, tf, g_next=None):
    m, d = x.shape
    w_gate, w_up, w_down = weights
    f = w_down.shape[0]
    first = g_next is not None
    emit_w = w_gate.dtype != BF16
    assert not emit_w or m == tm, "weight copies need every weight block visited exactly once"
    if first and not emit_w:
        return tuple(_ffn_stream(x, pre, weights, g_post, g_next, tm)), tuple(weights)
    row_spec = pl.BlockSpec((tm, d), lambda i, j: (i, 0))
    gain_spec = pl.BlockSpec((1, d), lambda i, j: (0, 0))
    wg_spec, wg_copy, wg_shape = _col_tiled(w_gate, tf)
    wu_spec, wu_copy, wu_shape = _col_tiled(w_up, tf)
    wd_spec = pl.BlockSpec((tf, d), lambda i, j: (j, 0))
    w_specs = [wg_spec, wu_spec, wd_spec]
    copy_specs = [wg_copy, wu_copy, wd_spec] if emit_w else []
    copy_shapes = [wg_shape, wu_shape, jax.ShapeDtypeStruct(w_down.shape, BF16)] if emit_w else []
    outs = pl.pallas_call(
        functools.partial(_ffn_kernel, first=first, emit_w=emit_w),
        grid=(m // tm, f // tf),
        in_specs=([row_spec, gain_spec] if first else [pl.BlockSpec(memory_space=pl.ANY), row_spec])
        + w_specs + [gain_spec] + ([gain_spec] if first else []),
        out_specs=[row_spec] + ([row_spec] if first else []) + copy_specs,
        out_shape=[jax.ShapeDtypeStruct((m, d), F32)] + ([jax.ShapeDtypeStruct((m, d), BF16)] if first else [])
        + copy_shapes,
        scratch_shapes=[pltpu.VMEM((tm, d), BF16)] if first
        else [pltpu.VMEM((tm, d), F32), pltpu.SemaphoreType.DMA(())],
        compiler_params=_params("parallel", "arbitrary"),
        name="ffn",
    )(*((x, pre, w_gate, w_up, w_down, g_post) + ((g_next,) if first else ())))
    n_out = 2 if first else 1
    return outs[:n_out], (tuple(outs[n_out:]) if emit_w else tuple(weights))


def _proj_kernel(xn_ref, w_ref, o_ref, *copy_ref):
    o_ref[...] = _dot(xn_ref[...], _weight(w_ref, *copy_ref))


def _proj(xn, w, tm, tn):
    m, d = xn.shape
    emit_w = w.dtype != BF16
    assert not emit_w or m == tm
    n = w.shape[1] if emit_w else w.shape[0] * tn
    w_spec, copy_spec, copy_shape = _col_tiled(w, tn)
    copy_specs, copy_shapes = ([copy_spec], [copy_shape]) if emit_w else ([], [])
    outs = pl.pallas_call(
        _proj_kernel,
        grid=(m // tm, n // tn),
        in_specs=[pl.BlockSpec((tm, d), lambda i, j: (i, 0)), w_spec],
        out_specs=[pl.BlockSpec((tm, tn), lambda i, j: (i, j))] + copy_specs,
        out_shape=[jax.ShapeDtypeStruct((m, n), F32)] + copy_shapes,
        compiler_params=_params("parallel", "arbitrary"),
        name="proj",
    )(xn, w)
    return outs[0], (outs[1] if emit_w else w)


def _memproj_kernel(x_ref, g_ref, wk_ref, wv_ref, k_ref, kh_ref, v_ref, vh_ref):
    xn = _rms(x_ref[...], g_ref[...]).astype(BF16)
    for w_ref, o_ref, oh_ref in ((wk_ref, k_ref, kh_ref), (wv_ref, v_ref, vh_ref)):
        res = _dot(xn, _weight(w_ref))
        o_ref[...] = res
        for h in range(XA_HEADS):
            for p in range(XA_LANE_SPLIT):
                lane0 = h * XA_HEAD_DIM + p * LANES
                oh_ref[:, p * XA_HEADS + h, :] = res[:, lane0:lane0 + LANES]


def _memproj(x, g, w_k, w_v):
    m, d = x.shape
    n = w_k.shape[1]
    assert n == XA_WIDTH
    tm = _row_tile(m, TILES['tm_mem'])
    w_spec = pl.BlockSpec((d, n), lambda i: (0, 0), pipeline_mode=pl.Buffered(1))
    flat_spec = pl.BlockSpec((tm, n), lambda i: (i, 0))
    head_spec = pl.BlockSpec((tm, SUBLANES, LANES), lambda i: (i, 0, 0))
    flat_shape = jax.ShapeDtypeStruct((m, n), F32)
    head_shape = jax.ShapeDtypeStruct((m, SUBLANES, LANES), F32)
    k, kh, v, vh = pl.pallas_call(
        _memproj_kernel,
        grid=(m // tm,),
        in_specs=[pl.BlockSpec((tm, d), lambda i: (i, 0)), pl.BlockSpec((1, d), lambda i: (0, 0)), w_spec, w_spec],
        out_specs=[flat_spec, head_spec, flat_spec, head_spec],
        out_shape=[flat_shape, head_shape, flat_shape, head_shape],
        compiler_params=_params("parallel"),
        name="memproj",
    )(x, g, w_k, w_v)
    return k, _heads_from_sublanes(kh), v, _heads_from_sublanes(vh)


def _pool_group_out(k, diff, wgrp_ref, scale_ref, z_ref):
    ch = slice(k * POOL_GROUP_DIM, (k + 1) * POOL_GROUP_DIM)
    zk = _dot(diff.astype(BF16), wgrp_ref[k]) * scale_ref[:, ch]
    z_ref[:, ch] = zk.astype(BF16)


def _pool_seq_kernel(u_ref, wgrp_ref, scale_ref, z_ref, ext_ref, *, halo):
    t = pl.program_id(1)
    tt = u_ref.shape[0]

    @pl.when(t == 0)
    def _():
        ext_ref[0:halo, :] = jnp.zeros((halo, POOL_WIDTH), F32)

    u = u_ref[...]
    ext_ref[halo:halo + tt, :] = u
    pos = t * tt + lax.broadcasted_iota(jnp.int32, (tt, 1), 0)
    for k, w in enumerate(POOL_WINDOWS):
        ch = slice(k * POOL_GROUP_DIM, (k + 1) * POOL_GROUP_DIM)
        a = ext_ref[:, ch]
        d = 1
        while d < w:
            a = a + pltpu.roll(a, d, axis=0)
            d *= 2
        count = jnp.minimum(pos + 1, w).astype(F32)
        diff = a[halo:, :] / count - u[:, ch]
        _pool_group_out(k, diff, wgrp_ref, scale_ref, z_ref)
    ext_ref[0:halo, :] = ext_ref[tt:tt + halo, :]


def _pool_seq(proj, w_grp, scale, batch, seq, tt):
    halo = 2 * SUBLANES
    nt = seq // tt
    return pl.pallas_call(
        functools.partial(_pool_seq_kernel, halo=halo),
        grid=(batch, nt),
        in_specs=[
            pl.BlockSpec((tt, POOL_WIDTH), lambda b, t: (b * nt + t, 0)),
            pl.BlockSpec(w_grp.shape, lambda b, t: (0, 0, 0)),
            pl.BlockSpec((1, POOL_WIDTH), lambda b, t: (0, 0)),
        ],
        out_specs=pl.BlockSpec((tt, POOL_WIDTH), lambda b, t: (b * nt + t, 0)),
        out_shape=jax.ShapeDtypeStruct((batch * seq, POOL_WIDTH), BF16),
        scratch_shapes=[pltpu.VMEM((tt + halo, POOL_WIDTH), F32)],
        compiler_params=_params("parallel", "arbitrary"),
        name="pool_seq",
    )(proj, w_grp, scale)


def _pool_step_kernel(u_ref, buf_ref, wgrp_ref, scale_ref, z_ref, nbuf_ref):
    u = u_ref[...]
    for k, w in enumerate(POOL_WINDOWS):
        ch = slice(k * POOL_GROUP_DIM, (k + 1) * POOL_GROUP_DIM)
        s = u[:, ch]
        for j in range(1, w):
            s = s + buf_ref[POOL_BUF - j, :, ch]
        diff = s / float(min(PAST_LEN + 1, w)) - u[:, ch]
        _pool_group_out(k, diff, wgrp_ref, scale_ref, z_ref)
    nbuf_ref[0:POOL_BUF - 1] = buf_ref[1:POOL_BUF]
    nbuf_ref[POOL_BUF - 1] = u


def _pool_step(proj, buf, w_grp, scale, tb):
    m = proj.shape[0]
    buf_spec = pl.BlockSpec((POOL_BUF, tb, POOL_WIDTH), lambda i: (0, i, 0))
    return pl.pallas_call(
        _pool_step_kernel,
        grid=(m // tb,),
        in_specs=[
            pl.BlockSpec((tb, POOL_WIDTH), lambda i: (i, 0)),
            buf_spec,
            pl.BlockSpec(w_grp.shape, lambda i: (0, 0, 0)),
            pl.BlockSpec((1, POOL_WIDTH), lambda i: (0, 0)),
        ],
        out_specs=[pl.BlockSpec((tb, POOL_WIDTH), lambda i: (i, 0)), buf_spec],
        out_shape=[jax.ShapeDtypeStruct((m, POOL_WIDTH), BF16), jax.ShapeDtypeStruct(buf.shape, F32)],
        compiler_params=_params("parallel"),
        name="pool_step",
    )(proj, buf, w_grp, scale)


def _zoh_lambda(a_re, a_im, log_step):
    dt = jnp.exp(log_step)
    mag = jnp.exp(a_re * dt)
    ang = a_im * dt
    return mag * jnp.cos(ang), mag * jnp.sin(ang)


def _tile_diag(blocks, rows_per_group):
    groups = SSM_GROUPS // SSM_CHUNKS
    rows, c = blocks.shape
    wide = groups * c
    spread = (lax.broadcasted_iota(jnp.int32, (c, wide), 1) % c
              == lax.broadcasted_iota(jnp.int32, (c, wide), 0)).astype(BF16)
    tiled = _dot(blocks.astype(BF16), spread)
    keep = (lax.broadcasted_iota(jnp.int32, (rows, wide), 0) // rows_per_group
            == lax.broadcasted_iota(jnp.int32, (rows, wide), 1) // c)
    return jnp.where(keep, tiled, 0.0).astype(BF16)


def _ssm_disc_kernel(a_re_ref, a_im_ref, ls_ref, a_re3_ref, a_im3_ref, ls3_ref, bt_ref, ct_ref,
                     bd_ref, cd_ref, lb_ref, tab_ref):
    a_re = a_re3_ref[...]
    a_im = a_im3_ref[...]
    lb_re, lb_im = _zoh_lambda(a_re, a_im, ls3_ref[...])
    den = a_re * a_re + a_im * a_im
    n_re = lb_re - 1.0
    f_re = (n_re * a_re + lb_im * a_im) / den
    f_im = (lb_im * a_re - n_re * a_im) / den
    b_re = bt_ref[0]
    b_im = bt_ref[1]
    bb = (f_re * b_re - f_im * b_im, f_re * b_im + f_im * b_re)
    groups = SSM_GROUPS // SSM_CHUNKS
    for part in range(2):
        for c in range(SSM_CHUNKS):
            grp = slice(c * groups, (c + 1) * groups)
            bd_ref[part, c] = _tile_diag(
                bb[part][grp].reshape(groups * SSM_GROUP_CH, SSM_STATE), SSM_GROUP_CH)
            cd_ref[part, c] = _tile_diag(
                ct_ref[part, grp].reshape(groups * SSM_STATE, SSM_GROUP_CH), SSM_STATE)

    lb_re, lb_im = _zoh_lambda(a_re_ref[...], a_im_ref[...], ls_ref[...])
    lb_ref[0] = lb_re
    lb_ref[1] = lb_im
    s_re, s_im = lb_re, lb_im
    for _ in range(int(math.log2(SSM_SEG))):
        s_re, s_im = _cmul(s_re, s_im, s_re, s_im)
    powers = [(s_re, s_im)]
    for _ in range(1, SUBLANES):
        powers.append(_cmul(*powers[-1], s_re, s_im))
    zero = jnp.zeros_like(lb_re)
    for r in range(SUBLANES):
        tab_ref[0, r] = lb_re
        tab_ref[1, r] = lb_im
        for i in range(3):
            d = 1 << i
            tab_ref[2 + 2 * i, r] = powers[d - 1][0] if r >= d else zero
            tab_ref[3 + 2 * i, r] = powers[d - 1][1] if r >= d else zero
        tab_ref[8, r] = powers[r][0]
        tab_ref[9, r] = powers[r][1]


def _ssm_disc(a_re, a_im, log_step, b_re, b_im, c_re, c_im):
    g, n, h = b_re.shape
    bt = jnp.stack([b_re, b_im]).swapaxes(2, 3)
    ct = jnp.stack([c_re, c_im]).swapaxes(2, 3)
    per_chunk = g // SSM_CHUNKS
    bd, cd, lb, tab = pl.pallas_call(
        _ssm_disc_kernel,
        out_shape=[jax.ShapeDtypeStruct((2, SSM_CHUNKS, per_chunk * h, per_chunk * n), BF16),
                   jax.ShapeDtypeStruct((2, SSM_CHUNKS, per_chunk * n, per_chunk * h), BF16),
                   jax.ShapeDtypeStruct((2, g, n), F32),
                   jax.ShapeDtypeStruct((SCAN_TILES, SUBLANES, g, n), F32)],
        name="ssm_disc",
    )(a_re, a_im, log_step[:, None], a_re[:, None, :], a_im[:, None, :], log_step[:, None, None], bt, ct)
    return bd, cd, lb.reshape(2, g * n), tab.reshape(SCAN_TILES, SUBLANES, g * n)


def _ssm_in(ub, bd_ref, x_ref):
    for c in range(SSM_CHUNKS):
        x_ref[:, c * SSM_CHUNK_LANES:(c + 1) * SSM_CHUNK_LANES] = _dot(
            ub[:, c * MXU_DIM:(c + 1) * MXU_DIM], bd_ref[c])


def _ssm_out(c, u, x_re_ref, x_im_ref, cdr_ref, cdi_ref, d_ref):
    ch = slice(c * MXU_DIM, (c + 1) * MXU_DIM)
    st = slice(c * SSM_CHUNK_LANES, (c + 1) * SSM_CHUNK_LANES)
    y = _dot(x_re_ref[:, st].astype(BF16), cdr_ref[c]) - _dot(x_im_ref[:, st].astype(BF16), cdi_ref[c])
    return jax.nn.gelu(y + d_ref[:, ch] * u[:, ch])


def _ssm_seq_kernel(u_ref, h_re_ref, h_im_ref, tab_ref, bdr_ref, bdi_ref, cdr_ref, cdi_ref, d_ref,
                    g_ref, last_re_ref, last_im_ref, xr_ref, xi_ref, cr_ref, ci_ref):
    t = pl.program_id(1)
    tt = u_ref.shape[0]
    seg = tt // SUBLANES

    @pl.when(t == 0)
    def _():
        cr_ref[...] = jnp.broadcast_to(h_re_ref[...], cr_ref.shape)
        ci_ref[...] = jnp.broadcast_to(h_im_ref[...], ci_ref.shape)

    def regroup(x, rows_major, rows_minor):
        w = x.shape[1]
        return jnp.swapaxes(x.reshape(rows_major, rows_minor, w), 0, 1).reshape(tt, w)

    u = regroup(u_ref[...], SUBLANES, seg)
    ub = u.astype(BF16)
    _ssm_in(ub, bdr_ref, xr_ref)
    _ssm_in(ub, bdi_ref, xi_ref)

    first_row = lax.broadcasted_iota(jnp.int32, (SUBLANES, SCAN_LANES), 0) == 0
    for lc in range(SSM_LANES // SCAN_LANES):
        sl = slice(lc * SCAN_LANES, (lc + 1) * SCAN_LANES)
        l_re = tab_ref[0, :, sl]
        l_im = tab_ref[1, :, sl]

        def rows(j):
            return pl.ds(j * SUBLANES, SUBLANES)

        def advance(j, x, sl=sl, l_re=l_re, l_im=l_im):
            x_re, x_im = _cmul(l_re, l_im, *x)
            x_re = x_re + xr_ref[rows(j), sl]
            x_im = x_im + xi_ref[rows(j), sl]
            xr_ref[rows(j), sl] = x_re
            xi_ref[rows(j), sl] = x_im
            return x_re, x_im

        zero = jnp.zeros((SUBLANES, SCAN_LANES), F32)
        f_re, f_im = zero, zero
        for j in range(seg):
            f_re, f_im = advance(j, (f_re, f_im))

        c_re = cr_ref[:, sl]
        c_im = ci_ref[:, sl]
        for i in range(3):
            s_re = pltpu.roll(f_re, 1 << i, axis=0)
            s_im = pltpu.roll(f_im, 1 << i, axis=0)
            m_re, m_im = _cmul(tab_ref[2 + 2 * i, :, sl], tab_ref[3 + 2 * i, :, sl], s_re, s_im)
            f_re, f_im = f_re + m_re, f_im + m_im
        m_re, m_im = _cmul(tab_ref[8, :, sl], tab_ref[9, :, sl], c_re, c_im)
        f_re, f_im = f_re + m_re, f_im + m_im
        e_re = jnp.where(first_row, c_re, pltpu.roll(f_re, 1, axis=0))
        e_im = jnp.where(first_row, c_im, pltpu.roll(f_im, 1, axis=0))
        cr_ref[:, sl] = jnp.broadcast_to(f_re[SUBLANES - 1:SUBLANES, :], (SUBLANES, SCAN_LANES))
        ci_ref[:, sl] = jnp.broadcast_to(f_im[SUBLANES - 1:SUBLANES, :], (SUBLANES, SCAN_LANES))

        def fix(j, w, sl=sl, l_re=l_re, l_im=l_im):
            w_re, w_im = _cmul(l_re, l_im, *w)
            xr_ref[rows(j), sl] += w_re
            xi_ref[rows(j), sl] += w_im
            return w_re, w_im

        w = (e_re, e_im)
        for j in range(seg):
            w = fix(j, w)

    for c in range(SSM_CHUNKS):
        ch = slice(c * MXU_DIM, (c + 1) * MXU_DIM)
        g = _ssm_out(c, u, xr_ref, xi_ref, cdr_ref, cdi_ref, d_ref)
        g_ref[:, ch] = regroup(g, seg, SUBLANES).astype(BF16)

    @pl.when(t == pl.num_programs(1) - 1)
    def _():
        last_re_ref[...] = cr_ref[0:1, :]
        last_im_ref[...] = ci_ref[0:1, :]


def _const_spec(shape):
    return pl.BlockSpec(shape, lambda *_: (0,) * len(shape))


def _part_spec(stacked, part):
    rest = stacked.shape[1:]
    return pl.BlockSpec((None,) + rest, lambda *_: (part,) + (0,) * len(rest))


def _ssm_seq(proj, h_re, h_im, tab, bd, cd, d_skip, batch, seq):
    tt = SSM_TIME_TILE
    nt = seq // tt
    col_block = POOL_WIDTH // SSM_WIDTH
    state_spec = pl.BlockSpec((None, 1, SSM_LANES), lambda b, t: (b, 0, 0))
    state_shape = jax.ShapeDtypeStruct((batch, 1, SSM_LANES), F32)
    return pl.pallas_call(
        _ssm_seq_kernel,
        grid=(batch, nt),
        in_specs=[
            pl.BlockSpec((tt, SSM_WIDTH), lambda b, t: (b * nt + t, col_block)),
            state_spec, state_spec,
            _const_spec(tab.shape),
            _part_spec(bd, 0), _part_spec(bd, 1), _part_spec(cd, 0), _part_spec(cd, 1),
            _const_spec(d_skip.shape),
        ],
        out_specs=[pl.BlockSpec((tt, SSM_WIDTH), lambda b, t: (b * nt + t, 0)), state_spec, state_spec],
        out_shape=[jax.ShapeDtypeStruct((batch * seq, SSM_WIDTH), BF16), state_shape, state_shape],
        scratch_shapes=[
            pltpu.VMEM((tt, SSM_LANES), F32), pltpu.VMEM((tt, SSM_LANES), F32),
            pltpu.VMEM((SUBLANES, SSM_LANES), F32), pltpu.VMEM((SUBLANES, SSM_LANES), F32),
        ],
        compiler_params=_params("parallel", "arbitrary"),
        name="ssm_seq",
    )(proj, h_re, h_im, tab, bd, bd, cd, cd, d_skip)


def _ssm_step_kernel(u_ref, h_re_ref, h_im_ref, lb_ref, bdr_ref, bdi_ref, cdr_ref, cdi_ref, d_ref,
                     g_ref, x_re_ref, x_im_ref):
    u = u_ref[...]
    ub = u.astype(BF16)
    _ssm_in(ub, bdr_ref, x_re_ref)
    _ssm_in(ub, bdi_ref, x_im_ref)
    m_re, m_im = _cmul(lb_ref[0:1, :], lb_ref[1:2, :], h_re_ref[...], h_im_ref[...])
    x_re_ref[...] += m_re
    x_im_ref[...] += m_im
    for c in range(SSM_CHUNKS):
        ch = slice(c * MXU_DIM, (c + 1) * MXU_DIM)
        g_ref[:, ch] = _ssm_out(c, u, x_re_ref, x_im_ref, cdr_ref, cdi_ref, d_ref).astype(BF16)


def _ssm_step(proj, h_re, h_im, lb, bd, cd, d_skip):
    m = proj.shape[0]
    state_spec = pl.BlockSpec((m, SSM_LANES), lambda i: (0, 0))
    state_shape = jax.ShapeDtypeStruct((m, SSM_LANES), F32)
    return pl.pallas_call(
        _ssm_step_kernel,
        grid=(1,),
        in_specs=[
            pl.BlockSpec((m, SSM_WIDTH), lambda i: (0, POOL_WIDTH // SSM_WIDTH)),
            state_spec, state_spec,
            _const_spec(lb.shape),
            _part_spec(bd, 0), _part_spec(bd, 1), _part_spec(cd, 0), _part_spec(cd, 1),
            _const_spec(d_skip.shape),
        ],
        out_specs=[pl.BlockSpec((m, SSM_WIDTH), lambda i: (0, 0)), state_spec, state_spec],
        out_shape=[jax.ShapeDtypeStruct((m, SSM_WIDTH), BF16), state_shape, state_shape],
        compiler_params=_params("arbitrary"),
        name="ssm_step",
    )(proj, h_re, h_im, lb, bd, bd, cd, cd, d_skip)


def _xa_seq_kernel(q_ref, k_ref, v_ref, o_ref):
    for h in range(XA_HEADS):
        ch = slice(h * XA_HEAD_DIM, (h + 1) * XA_HEAD_DIM)
        q = q_ref[:, ch].astype(BF16)
        k = k_ref[:, ch].astype(BF16)
        v = v_ref[:, ch].astype(BF16)
        s = lax.dot_general(q, k, (((1,), (1,)), ((), ())), preferred_element_type=F32) * XA_SCALE
        e = jnp.exp(s - jnp.max(s, axis=-1, keepdims=True))
        p = e / jnp.sum(e, axis=-1, keepdims=True)
        o_ref[:, ch] = _dot(p.astype(BF16), v).astype(BF16)


def _xa_seq(proj, mem_k, mem_v, batch, seq, tq):
    nt = seq // tq
    q_block = (POOL_WIDTH + SSM_WIDTH) // XA_WIDTH
    kv_spec = pl.BlockSpec((N_MEM, XA_WIDTH), lambda b, t: (b, 0))
    return pl.pallas_call(
        _xa_seq_kernel,
        grid=(batch, nt),
        in_specs=[pl.BlockSpec((tq, XA_WIDTH), lambda b, t: (b * nt + t, q_block)), kv_spec, kv_spec],
        out_specs=pl.BlockSpec((tq, XA_WIDTH), lambda b, t: (b * nt + t, 0)),
        out_shape=jax.ShapeDtypeStruct((batch * seq, XA_WIDTH), BF16),
        compiler_params=_params("parallel", "parallel"),
        name="xa_seq",
    )(proj, mem_k, mem_v)


XA_LANE_SPLIT = XA_HEAD_DIM // LANES


def _heads_on_sublanes(x):
    lead = x.shape[:-2]
    x = x.reshape(lead + (XA_HEADS, XA_LANE_SPLIT, LANES)).swapaxes(-3, -2)
    return x.reshape(lead + (XA_LANE_SPLIT * XA_HEADS, LANES))


def _heads_from_sublanes(x):
    lead = x.shape[:-2]
    x = x.reshape(lead + (XA_LANE_SPLIT, XA_HEADS, LANES)).swapaxes(-3, -2)
    return x.reshape(lead + (XA_HEADS, XA_HEAD_DIM))


def _xa_step_kernel(q_ref, k_ref, v_ref, o_ref):
    q = q_ref[...]
    part = jnp.sum(k_ref[...] * q, axis=-1, keepdims=True)
    s = (part + pltpu.roll(part, XA_HEADS, axis=2)) * XA_SCALE
    e = jnp.exp(s - jnp.max(s, axis=1, keepdims=True))
    r = 1.0 / jnp.sum(e, axis=1, keepdims=True)
    o_ref[...] = jnp.sum(e * v_ref[...], axis=1, keepdims=True) * r


def _xa_step(q4, cache_k, cache_v, tb):
    assert XA_LANE_SPLIT * XA_HEADS == SUBLANES
    m = q4.shape[0]
    kv_spec = pl.BlockSpec((tb, N_MEM, SUBLANES, LANES), lambda i: (i, 0, 0, 0))
    q_spec = pl.BlockSpec((tb, 1, SUBLANES, LANES), lambda i: (i, 0, 0, 0))
    o = pl.pallas_call(
        _xa_step_kernel,
        grid=(m // tb,),
        in_specs=[q_spec, kv_spec, kv_spec],
        out_specs=q_spec,
        out_shape=jax.ShapeDtypeStruct((m, 1, SUBLANES, LANES), F32),
        compiler_params=_params("parallel"),
        name="xa_step",
    )(_heads_on_sublanes(q4), _heads_on_sublanes(cache_k), _heads_on_sublanes(cache_v))
    return _heads_from_sublanes(o)


def _gate_kernel(z_ref, g_ref, o_ref, ga_ref, gb_ref, gc_ref, wpo_ref, wv_ref, wg_ref, wxo_ref, m_ref, *copy_refs):
    copies = copy_refs if copy_refs else (None,) * 4
    g = g_ref[...]
    o_pool = _dot(z_ref[...], _weight(wpo_ref, copies[0]))
    o_ssm = _dot(g, _weight(wv_ref, copies[1])) * jax.nn.sigmoid(_dot(g, _weight(wg_ref, copies[2])))
    o_xa = _dot(o_ref[...], _weight(wxo_ref, copies[3]))
    merged = (jax.nn.sigmoid(ga_ref[...]) * o_pool + jax.nn.sigmoid(gb_ref[...]) * o_ssm
              + jax.nn.sigmoid(gc_ref[...]) * o_xa)
    m_ref[...] = merged.astype(BF16)


def _gate(z, g, o, proj, weights, tm, tn):
    m, half = z.shape
    emit_w = weights[0].dtype != BF16
    assert not emit_w or m == tm
    d = weights[0].shape[1] if emit_w else weights[0].shape[0] * tn
    gate0 = OFF_GATE // tn
    per_gate = d // tn
    branch_spec = pl.BlockSpec((tm, half), lambda i, n: (i, 0))
    w_specs, copy_specs, copy_shapes = map(list, zip(*[_col_tiled(w, tn) for w in weights]))
    if not emit_w:
        copy_specs, copy_shapes = [], []

    def gate_spec(k):
        return pl.BlockSpec((tm, tn), lambda i, n: (i, gate0 + k * per_gate + n))

    outs = pl.pallas_call(
        _gate_kernel,
        grid=(m // tm, d // tn),
        in_specs=[branch_spec] * 3 + [gate_spec(0), gate_spec(1), gate_spec(2)] + w_specs,
        out_specs=[pl.BlockSpec((tm, tn), lambda i, n: (i, n))] + copy_specs,
        out_shape=[jax.ShapeDtypeStruct((m, d), BF16)] + copy_shapes,
        compiler_params=_params("parallel", "arbitrary"),
        name="gate",
    )(z, g, o, proj, proj, proj, *weights)
    return outs[0], (tuple(outs[1:]) if emit_w else tuple(weights))


def _mixout_kernel(m_ref, wout_ref, h_ref, gpost_ref, gnext_ref, h2_ref, xn_ref, *copy_ref, sub):
    wout = _weight(wout_ref, *copy_ref)
    for r in range(m_ref.shape[0] // sub):
        rows = slice(r * sub, (r + 1) * sub)
        h2 = h_ref[rows, :] + _rms(_dot(m_ref[rows, :], wout), gpost_ref[...])
        h2_ref[rows, :] = h2
        xn_ref[rows, :] = _rms(h2, gnext_ref[...]).astype(BF16)


def _mixout(merged, w_out, h, g_post, g_next, tm):
    m, d = h.shape
    emit_w = w_out.dtype != BF16
    assert not emit_w or m == tm
    row_spec = pl.BlockSpec((tm, d), lambda i: (i, 0))
    gain_spec = pl.BlockSpec((1, d), lambda i: (0, 0))
    w_spec = pl.BlockSpec((d, d), lambda i: (0, 0))
    copy_specs, copy_shapes = _with_copies([w_spec], [w_out], emit_w)
    outs = pl.pallas_call(
        functools.partial(_mixout_kernel, sub=min(tm, TILES['mix_sub'])),
        grid=(m // tm,),
        in_specs=[row_spec, w_spec, row_spec, gain_spec, gain_spec],
        out_specs=[row_spec, row_spec] + copy_specs,
        out_shape=[jax.ShapeDtypeStruct((m, d), F32), jax.ShapeDtypeStruct((m, d), BF16)] + copy_shapes,
        compiler_params=_params("parallel"),
        name="mixout",
    )(merged, w_out, h, g_post, g_next)
    return outs[:2], (outs[2] if emit_w else w_out)


def _row_tile(m, target):
    return target if m % target == 0 else m


FFN1_W = ('w_ff1_gate', 'w_ff1_up', 'w_ff1_down')
FFN2_W = ('w_ff2_gate', 'w_ff2_up', 'w_ff2_down')
GATE_W = ('w_pool_out', 'w_glu_val', 'w_glu_gate', 'w_xa_out')
TILES = dict(tm=512, tm_ffn2=1024, tf=512, tm_proj=2048, tn_proj=1024, tm_gate=1024, tn=512, tm_mem=512,
             mix_sub=128, pool_seq=2048, xa_seq=2048, pool_step=32, xa_step=4)


def _layer(x, w, mixers):
    m = x.shape[0]
    tm = _row_tile(m, TILES['tm'])
    wb = dict(w)
    (h, xn), wb_ffn1 = _ffn(x, w['g_ff1_pre'], [w[k] for k in FFN1_W], w['g_ff1_post'], tm, TILES['tf'],
                            g_next=w['g_mix_pre'])
    proj, wb['w_in'] = _proj(xn, w['w_in'], _row_tile(m, TILES['tm_proj']), TILES['tn_proj'])
    z, g, o, states = mixers(proj)
    merged, wb_gate = _gate(z, g, o, proj, [w[k] for k in GATE_W], _row_tile(m, TILES['tm_gate']), TILES['tn'])
    (h, xn), wb['w_out'] = _mixout(merged, w['w_out'], h, w['g_mix_post'], w['g_ff2_pre'], tm)
    (y,), wb_ffn2 = _ffn(h, xn, [w[k] for k in FFN2_W], w['g_ff2_post'], _row_tile(m, TILES['tm_ffn2']),
                         TILES['tf'])
    wb.update(zip(FFN1_W + GATE_W + FFN2_W, wb_ffn1 + wb_gate + wb_ffn2))
    return y, states, wb


def kernel(x_prompt, x_sample, mem_prompt, cache_mem_k, cache_mem_v, state_pool, state_ssm_re, state_ssm_im, g_ff1_pre, w_ff1_gate, w_ff1_up, w_ff1_down, g_ff1_post, g_mix_pre, w_in, w_pool_grp, pool_scale, w_pool_out, ssm_a_re, ssm_a_im, ssm_log_step, ssm_b_re, ssm_b_im, ssm_c_re, ssm_c_im, ssm_d, w_glu_val, w_glu_gate, g_mem, w_mem_k, w_mem_v, w_xa_out, w_out, g_mix_post, g_ff2_pre, w_ff2_gate, w_ff2_up, w_ff2_down, g_ff2_post):
    bp, lp, d = x_prompt.shape
    bs, ls, _ = x_sample.shape
    depth = w_in.shape[0]
    assert depth == 1 and ls == 1 and lp >= POOL_BUF and lp % SSM_TIME_TILE == 0

    mats = dict(w_ff1_gate=w_ff1_gate, w_ff1_up=w_ff1_up, w_ff1_down=w_ff1_down, w_in=w_in,
                w_pool_out=w_pool_out, w_glu_val=w_glu_val, w_glu_gate=w_glu_gate, w_xa_out=w_xa_out,
                w_out=w_out, w_ff2_gate=w_ff2_gate, w_ff2_up=w_ff2_up, w_ff2_down=w_ff2_down)
    w = {k: v[0] for k, v in mats.items()}
    w.update(g_ff1_pre=g_ff1_pre, g_ff1_post=g_ff1_post, g_mix_pre=g_mix_pre, g_mix_post=g_mix_post,
             g_ff2_pre=g_ff2_pre, g_ff2_post=g_ff2_post)
    w_grp = w_pool_grp[0].astype(BF16)
    scale = pool_scale
    d_skip = ssm_d

    bd, cd, lb, tab = _ssm_disc(ssm_a_re[0], ssm_a_im[0], ssm_log_step[0], ssm_b_re[0], ssm_b_im[0],
                                ssm_c_re[0], ssm_c_im[0])

    def sample_mixers(proj):
        z, new_buf = _pool_step(proj, state_pool[0].swapaxes(0, 1), w_grp, scale, TILES['pool_step'])
        g, x_re, x_im = _ssm_step(proj, state_ssm_re[0].reshape(bs, SSM_LANES),
                                  state_ssm_im[0].reshape(bs, SSM_LANES), lb, bd, cd, d_skip)
        q4 = proj[:, POOL_WIDTH + SSM_WIDTH:OFF_GATE].reshape(bs, 1, XA_HEADS, XA_HEAD_DIM)
        o = _xa_step(q4, cache_mem_k[0], cache_mem_v[0], TILES['xa_step'])
        return z, g, o.reshape(bs, XA_WIDTH).astype(BF16), (new_buf, x_re, x_im)

    y_sample, (pool_s, re_s, im_s), wb = _layer(x_sample.reshape(bs * ls, d), w, sample_mixers)

    mem = mem_prompt.reshape(bp * N_MEM, d)
    mem_k, mem_k_heads, mem_v, mem_v_heads = _memproj(mem, g_mem, w_mem_k[0], w_mem_v[0])

    def prompt_mixers(proj):
        z = _pool_seq(proj, w_grp, scale, bp, lp, TILES['pool_seq'])
        zeros = jnp.zeros((bp, 1, SSM_LANES), F32)
        g, last_re, last_im = _ssm_seq(proj, zeros, zeros, tab, bd, cd, d_skip, bp, lp)
        o = _xa_seq(proj, mem_k, mem_v, bp, lp, TILES['xa_seq'])
        pool_rows = proj.reshape(bp, lp, IN_WIDTH)[:, lp - POOL_BUF:, :POOL_WIDTH]
        return z, g, o, (pool_rows, last_re, last_im)

    y_prompt, (pool_p, re_p, im_p), _ = _layer(x_prompt.reshape(bp * lp, d), wb, prompt_mixers)

    kv_shape = (1, bp, N_MEM, XA_HEADS, XA_HEAD_DIM)
    st_shape = (SSM_GROUPS, SSM_STATE)
    return (y_prompt.reshape(bp, lp, d), y_sample.reshape(bs, ls, d),
            mem_k_heads.reshape(kv_shape), mem_v_heads.reshape(kv_shape),
            pool_p[None],
            re_p.reshape((1, bp) + st_shape), im_p.reshape((1, bp) + st_shape),
            pool_s.swapaxes(0, 1)[None],
            re_s.reshape((1, bs) + st_shape), im_s.reshape((1, bs) + st_shape))
```
